```python
import math, functools
import jax, jax.numpy as jnp
from jax import lax
import numpy as np

D_MODEL = 2048
BATCH = 2
SEQ = 4096
DEPTH = 4
DEC_BATCH = 8
DEC_SEQ = 1
PAST_LEN = 16384
PAGE_SIZE = 128

HEAD_DIM = 128
ATTN_HEADS = D_MODEL // 2 // HEAD_DIM
ATTN_KV_HEADS = ATTN_HEADS // 2
GQA_GROUP = ATTN_HEADS // ATTN_KV_HEADS
ATTN_WIDTH = ATTN_HEADS * HEAD_DIM
KV_WIDTH = ATTN_KV_HEADS * HEAD_DIM
IDX_HEADS = 16
IDX_DIM = 64
TOPK_MAX = 256
Q_BLOCK = 128
ROPE_THETA = 10000.0

GDN_DK = 128
GDN_DV = 128
GDN_HEADS = D_MODEL // 4 // GDN_DV
GDN_KW = GDN_HEADS * GDN_DK
GDN_VW = GDN_HEADS * GDN_DV
GDN_CONV = 4
GDN_CONV_DIM = 2 * GDN_KW + GDN_VW
GDN_CHUNK = 64

SC_WIDTH = D_MODEL // 4
SC_CONV = 3

MIX_WIDTH = ATTN_WIDTH + GDN_VW + SC_WIDTH
FFN_DIM = -(-8 * D_MODEL // (3 * 256)) * 256
DN_ALPHA = (2 * DEPTH) ** 0.25
DN_BETA = (8 * DEPTH) ** -0.25
LN_EPS = 1e-5
NORM_EPS = 1e-6
IN_SIZES = (ATTN_WIDTH, KV_WIDTH, KV_WIDTH, IDX_HEADS * IDX_DIM, IDX_DIM, IDX_HEADS,
            GDN_CONV_DIM, GDN_VW, GDN_HEADS, GDN_HEADS, SC_WIDTH, SC_WIDTH, SC_WIDTH)
IN_WIDTH = sum(IN_SIZES)

kernel_name = 'hymba_dsa_gdn_shortconv_deepnorm_step'


def layer_norm(x, g, b):
    xf = x.astype(jnp.float32)
    xc = xf - jnp.mean(xf, -1, keepdims=True)
    var = jnp.mean(xc * xc, -1, keepdims=True)
    return (xc * lax.rsqrt(var + LN_EPS) * g.astype(jnp.float32) + b.astype(jnp.float32)).astype(x.dtype)


def rope(x, pos):
    half = x.shape[-1] // 2
    inv = ROPE_THETA ** (-jnp.arange(half, dtype=jnp.float32) / half)
    ang = pos.astype(jnp.float32)[:, None] * inv
    cos = jnp.cos(ang)[:, None, :]
    sin = jnp.sin(ang)[:, None, :]
    xf = x.astype(jnp.float32)
    x1, x2 = xf[..., :half], xf[..., half:]
    return jnp.concatenate([x1 * cos - x2 * sin, x2 * cos + x1 * sin], -1).astype(x.dtype)


def split_in(x, w_in):
    h = jnp.einsum('btd,df->btf', x, w_in)
    cuts = np.cumsum(IN_SIZES)[:-1].tolist()
    return jnp.split(h, cuts, axis=-1)


def causal_dwconv(xp, w):
    width = w.shape[0]
    t = xp.shape[1] - width + 1
    y = xp[:, 0:t] * w[0]
    for j in range(1, width):
        y = y + xp[:, j:j + t] * w[j]
    return y


def l2norm(x):
    return x * lax.rsqrt(jnp.sum(x * x, -1, keepdims=True) + NORM_EPS)


def attn_inputs(aq, ak, av, iq, ik, pos):
    b, t = aq.shape[:2]
    q = rope(aq.reshape(b, t, ATTN_HEADS, HEAD_DIM), pos)
    k = rope(ak.reshape(b, t, ATTN_KV_HEADS, HEAD_DIM), pos)
    v = av.reshape(b, t, ATTN_KV_HEADS, HEAD_DIM)
    iqr = rope(iq.reshape(b, t, IDX_HEADS, IDX_DIM), pos)
    ikr = rope(ik.reshape(b, t, 1, IDX_DIM), pos)[:, :, 0]
    return q, k, v, iqr, ikr


def indexer_scores(iq, iw, ik):
    s = jnp.einsum('bqhd,bld->bqhl', iq.astype(jnp.float32), ik.astype(jnp.float32)) * IDX_DIM ** -0.5
    w = iw.astype(jnp.float32) * IDX_HEADS ** -0.5
    return jnp.einsum('bqhl,bqh->bql', jax.nn.relu(s), w)


def select_keys(score, qpos, kpos, topk):
    causal = kpos[None, :] <= qpos[:, None]
    _, idx = lax.top_k(jnp.where(causal, score, -jnp.inf), topk)
    valid = idx <= qpos[:, None]
    return idx, valid


def sparse_attend(q, kg, vg, valid):
    b, nq = q.shape[:2]
    qg = q.reshape(b, nq, ATTN_KV_HEADS, GQA_GROUP, HEAD_DIM).astype(jnp.float32)
    logits = jnp.einsum('bqkgd,bqskd->bqkgs', qg, kg.astype(jnp.float32)) * HEAD_DIM ** -0.5
    p = jax.nn.softmax(jnp.where(valid[:, :, None, None, :], logits, -jnp.inf), axis=-1)
    o = jnp.einsum('bqkgs,bqskd->bqkgd', p, vg.astype(jnp.float32))
    return o.reshape(b, nq, ATTN_WIDTH).astype(q.dtype)


def dsa_prompt(q, k, v, iq, ik, iw):
    b, t = q.shape[:2]
    topk = min(TOPK_MAX, t // 4)
    nb = t // Q_BLOCK
    kpos = jnp.arange(t)
    take = jax.vmap(lambda a, i: a[i])

    def blocks(a):
        return jnp.swapaxes(a.reshape(b, nb, Q_BLOCK, *a.shape[2:]), 0, 1)

    def one_block(args):
        qb, iqb, iwb, b0 = args
        qpos = b0 * Q_BLOCK + jnp.arange(Q_BLOCK)
        idx, valid = select_keys(indexer_scores(iqb, iwb, ik), qpos, kpos, topk)
        return sparse_attend(qb, take(k, idx), take(v, idx), valid)

    out = lax.map(one_block, (blocks(q), blocks(iq), blocks(iw), jnp.arange(nb)))
    return jnp.swapaxes(out, 0, 1).reshape(b, t, ATTN_WIDTH)


def dsa_sample(q, k_new, v_new, iq, ik_new, iw, cache_k, cache_v, cache_kidx, page_table, layer):
    bd, td = q.shape[:2]
    past = page_table.shape[1] * PAGE_SIZE
    total = past + td
    ik_past = cache_kidx[layer, page_table].reshape(bd, past, IDX_DIM)
    ik_all = jnp.concatenate([ik_past.astype(ik_new.dtype), ik_new], axis=1)
    qpos = past + jnp.arange(td)
    idx, valid = select_keys(indexer_scores(iq, iw, ik_all), qpos, jnp.arange(total), min(TOPK_MAX, total // 4))
    take = jax.vmap(lambda a, i: a[i])
    in_past = (idx < past)[..., None, None]
    idx_p = jnp.minimum(idx, past - 1)
    pages = take(page_table, idx_p // PAGE_SIZE)
    offs = idx_p % PAGE_SIZE
    idx_n = jnp.clip(idx - past, 0, td - 1)
    kg = jnp.where(in_past, cache_k[layer, pages, offs].astype(k_new.dtype), take(k_new, idx_n))
    vg = jnp.where(in_past, cache_v[layer, pages, offs].astype(v_new.dtype), take(v_new, idx_n))
    return sparse_attend(q, kg, vg, valid)


def gdn_prepare(qkv, b_raw, a_raw, a_log, dt_bias):
    b, t = qkv.shape[:2]
    qkv = jax.nn.silu(qkv.astype(jnp.float32))
    q, k, v = jnp.split(qkv, [GDN_KW, 2 * GDN_KW], axis=-1)
    q = l2norm(q.reshape(b, t, GDN_HEADS, GDN_DK)) * GDN_DK ** -0.5
    k = l2norm(k.reshape(b, t, GDN_HEADS, GDN_DK))
    v = v.reshape(b, t, GDN_HEADS, GDN_DV)
    beta = jax.nn.sigmoid(b_raw.astype(jnp.float32))
    g = -jnp.exp(a_log.astype(jnp.float32)) * jax.nn.softplus(a_raw.astype(jnp.float32) + dt_bias.astype(jnp.float32))
    return q, k, v, g, beta


def gdn_chunked(q, k, v, g, beta, state):
    b, t, h, _ = q.shape
    n = t // GDN_CHUNK

    def chunks(a):
        a = a.reshape(b, n, GDN_CHUNK, *a.shape[2:])
        return jnp.moveaxis(jnp.moveaxis(a, 1, 0), 3, 2)

    qc, kc, vc, gc, bc = (chunks(a) for a in (q, k, v, g, beta))
    gcum = jnp.cumsum(gc, axis=-1)
    ii = jnp.arange(GDN_CHUNK)
    lower = ii[:, None] >= ii[None, :]
    strict = ii[:, None] > ii[None, :]
    diff = gcum[..., :, None] - gcum[..., None, :]
    decay = jnp.where(lower, jnp.exp(jnp.where(lower, diff, 0.0)), 0.0)
    kb = kc * bc[..., None]
    a_mat = jnp.where(strict, jnp.einsum('nbhid,nbhjd->nbhij', kb, kc) * decay, 0.0) + jnp.eye(GDN_CHUNK, dtype=jnp.float32)
    solve = functools.partial(lax.linalg.triangular_solve, left_side=True, lower=True, unit_diagonal=True)
    u = solve(a_mat, vc * bc[..., None])
    w = solve(a_mat, kb * jnp.exp(gcum)[..., None])

    def step(s, inp):
        qi, ki, ui, wi, gi, di = inp
        v_new = ui - jnp.einsum('bhcd,bhde->bhce', wi, s)
        intra = jnp.einsum('bhid,bhjd->bhij', qi, ki) * di
        o = jnp.einsum('bhcd,bhde->bhce', qi * jnp.exp(gi)[..., None], s) + jnp.einsum('bhij,bhje->bhie', intra, v_new)
        g_last = gi[..., -1:]
        s = s * jnp.exp(g_last)[..., None] + jnp.einsum('bhcd,bhce->bhde', ki * jnp.exp(g_last - gi)[..., None], v_new)
        return s, o

    s, o = lax.scan(step, state, (qc, kc, u, w, gcum, decay))
    o = jnp.moveaxis(jnp.moveaxis(o, 2, 3), 0, 1).reshape(b, t, h, GDN_DV)
    return s, o


def gdn_recurrent(q, k, v, g, beta, state):
    def step(s, inp):
        qt, kt, vt, gt, bt = inp
        s = s * jnp.exp(gt)[..., None, None]
        delta = (vt - jnp.einsum('bhd,bhde->bhe', kt, s)) * bt[..., None]
        s = s + jnp.einsum('bhd,bhe->bhde', kt, delta)
        return s, jnp.einsum('bhd,bhde->bhe', qt, s)

    s, o = lax.scan(step, state, tuple(jnp.moveaxis(a, 1, 0) for a in (q, k, v, g, beta)))
    return s, jnp.moveaxis(o, 0, 1)


def gdn_output(o, z, norm_g):
    b, t = o.shape[:2]
    on = o * lax.rsqrt(jnp.mean(o * o, -1, keepdims=True) + NORM_EPS) * norm_g.astype(jnp.float32)
    zg = jax.nn.silu(z.astype(jnp.float32)).reshape(b, t, GDN_HEADS, GDN_DV)
    return (on * zg).reshape(b, t, GDN_VW)


def finish_layer(x, mix, w_out, ln1_g, ln1_b, w_gate, w_up, w_down, ln2_g, ln2_b):
    h = layer_norm(DN_ALPHA * x + jnp.einsum('btm,md->btd', mix.astype(x.dtype), w_out), ln1_g, ln1_b)
    hid = jax.nn.silu(jnp.einsum('btd,df->btf', h, w_gate)) * jnp.einsum('btd,df->btf', h, w_up)
    return layer_norm(DN_ALPHA * h + jnp.einsum('btf,fd->btd', hid, w_down), ln2_g, ln2_b)


def setup_inputs(seed: int = 0) -> dict:
    key = jax.random.key(seed)
    ks = jax.random.split(key, 24)

    def nrm(k, shape, s=1.0):
        return jax.random.normal(k, shape, jnp.float32) * s

    n_pages = PAST_LEN // PAGE_SIZE
    n_used = DEC_BATCH * n_pages
    n_pool = n_used + (n_used + 3) // 4
    page_table = jax.random.permutation(ks[5], n_pool)[:n_used].reshape(DEC_BATCH, n_pages).astype(jnp.int32)
    a_init = jax.random.uniform(ks[11], (DEPTH, GDN_HEADS), jnp.float32, 1.0, 16.0)
    dt = jnp.exp(jax.random.uniform(ks[12], (DEPTH, GDN_HEADS), jnp.float32, math.log(1e-3), math.log(1e-1)))
    return {
        'x_prompt': nrm(ks[0], (BATCH, SEQ, D_MODEL)),
        'x_sample': nrm(ks[1], (DEC_BATCH, DEC_SEQ, D_MODEL)),
        'cache_k': nrm(ks[2], (DEPTH, n_pool, PAGE_SIZE, ATTN_KV_HEADS, HEAD_DIM)),
        'cache_v': nrm(ks[3], (DEPTH, n_pool, PAGE_SIZE, ATTN_KV_HEADS, HEAD_DIM)),
        'cache_kidx': nrm(ks[4], (DEPTH, n_pool, PAGE_SIZE, IDX_DIM)),
        'page_table': page_table,
        'state_gdn': nrm(ks[6], (DEPTH, DEC_BATCH, GDN_HEADS, GDN_DK, GDN_DV), 0.1),
        'state_gdn_conv': nrm(ks[7], (DEPTH, DEC_BATCH, GDN_CONV - 1, GDN_CONV_DIM)),
        'state_sc_conv': nrm(ks[8], (DEPTH, DEC_BATCH, SC_CONV - 1, SC_WIDTH)),
        'w_in': nrm(ks[9], (DEPTH, D_MODEL, IN_WIDTH), D_MODEL ** -0.5),
        'gdn_conv_w': nrm(ks[10], (DEPTH, GDN_CONV, GDN_CONV_DIM), GDN_CONV ** -0.5),
        'gdn_a_log': jnp.log(a_init),
        'gdn_dt_bias': dt + jnp.log(-jnp.expm1(-dt)),
        'gdn_norm_g': 1.0 + nrm(ks[13], (DEPTH, GDN_DV), 0.02),
        'sc_conv_w': nrm(ks[14], (DEPTH, SC_CONV, SC_WIDTH), SC_CONV ** -0.5),
        'w_out': nrm(ks[15], (DEPTH, MIX_WIDTH, D_MODEL), MIX_WIDTH ** -0.5 * DN_BETA),
        'ln1_g': 1.0 + nrm(ks[16], (DEPTH, D_MODEL), 0.02),
        'ln1_b': nrm(ks[17], (DEPTH, D_MODEL), 0.02),
        'w_gate': nrm(ks[18], (DEPTH, D_MODEL, FFN_DIM), D_MODEL ** -0.5),
        'w_up': nrm(ks[19], (DEPTH, D_MODEL, FFN_DIM), D_MODEL ** -0.5 * DN_BETA),
        'w_down': nrm(ks[20], (DEPTH, FFN_DIM, D_MODEL), FFN_DIM ** -0.5 * DN_BETA),
        'ln2_g': 1.0 + nrm(ks[21], (DEPTH, D_MODEL), 0.02),
        'ln2_b': nrm(ks[22], (DEPTH, D_MODEL), 0.02),
    }


def reference(x_prompt, x_sample, cache_k, cache_v, cache_kidx, page_table, state_gdn, state_gdn_conv,
              state_sc_conv, w_in, gdn_conv_w, gdn_a_log, gdn_dt_bias, gdn_norm_g, sc_conv_w, w_out,
              ln1_g, ln1_b, w_gate, w_up, w_down, ln2_g, ln2_b):
    bp, tp = x_prompt.shape[:2]
    ts = x_sample.shape[1]
    past = page_table.shape[1] * PAGE_SIZE
    pos_p = jnp.arange(tp)
    pos_s = past + jnp.arange(ts)
    xp, xs = x_prompt, x_sample
    kp_l, vp_l, ip_l, ks_l, vs_l, is_l = [], [], [], [], [], []
    sp_l, ss_l, gcp_l, gcs_l, ccp_l, ccs_l = [], [], [], [], [], []
    for l in range(DEPTH):
        post = (w_out[l], ln1_g[l], ln1_b[l], w_gate[l], w_up[l], w_down[l], ln2_g[l], ln2_b[l])

        aq, ak, av, iq, ik, iw, gqkv, gz, gb, ga, sb, sc, sx = split_in(xp, w_in[l])
        q, k, v, iqr, ikr = attn_inputs(aq, ak, av, iq, ik, pos_p)
        attn = dsa_prompt(q, k, v, iqr, ikr, iw)
        gbuf = jnp.pad(gqkv, ((0, 0), (GDN_CONV - 1, 0), (0, 0)))
        gq, gk, gv, gg, gbeta = gdn_prepare(causal_dwconv(gbuf, gdn_conv_w[l]), gb, ga, gdn_a_log[l], gdn_dt_bias[l])
        s_new, o = gdn_chunked(gq, gk, gv, gg, gbeta, jnp.zeros((bp, GDN_HEADS, GDN_DK, GDN_DV), jnp.float32))
        sbuf = jnp.pad(sc * sx, ((0, 0), (SC_CONV - 1, 0), (0, 0)))
        mix = jnp.concatenate([attn, gdn_output(o, gz, gdn_norm_g[l]).astype(xp.dtype),
                               (sb * causal_dwconv(sbuf, sc_conv_w[l])).astype(xp.dtype)], axis=-1)
        kp_l.append(k)
        vp_l.append(v)
        ip_l.append(ikr)
        sp_l.append(s_new)
        gcp_l.append(gbuf[:, -(GDN_CONV - 1):])
        ccp_l.append(sbuf[:, -(SC_CONV - 1):])
        xp = finish_layer(xp, mix, *post)

        aq, ak, av, iq, ik, iw, gqkv, gz, gb, ga, sb, sc, sx = split_in(xs, w_in[l])
        q, k, v, iqr, ikr = attn_inputs(aq, ak, av, iq, ik, pos_s)
        attn = dsa_sample(q, k, v, iqr, ikr, iw, cache_k, cache_v, cache_kidx, page_table, l)
        gbuf = jnp.concatenate([state_gdn_conv[l].astype(gqkv.dtype), gqkv], axis=1)
        gq, gk, gv, gg, gbeta = gdn_prepare(causal_dwconv(gbuf, gdn_conv_w[l]), gb, ga, gdn_a_log[l], gdn_dt_bias[l])
        s_new, o = gdn_recurrent(gq, gk, gv, gg, gbeta, state_gdn[l].astype(jnp.float32))
        u = sc * sx
        sbuf = jnp.concatenate([state_sc_conv[l].astype(u.dtype), u], axis=1)
        mix = jnp.concatenate([attn, gdn_output(o, gz, gdn_norm_g[l]).astype(xs.dtype),
                               (sb * causal_dwconv(sbuf, sc_conv_w[l])).astype(xs.dtype)], axis=-1)
        ks_l.append(k)
        vs_l.append(v)
        is_l.append(ikr)
        ss_l.append(s_new)
        gcs_l.append(gbuf[:, -(GDN_CONV - 1):])
        ccs_l.append(sbuf[:, -(SC_CONV - 1):])
        xs = finish_layer(xs, mix, *post)

    return (xp, xs, jnp.stack(kp_l), jnp.stack(vp_l), jnp.stack(ip_l), jnp.stack(ks_l), jnp.stack(vs_l),
            jnp.stack(is_l), jnp.stack(sp_l), jnp.stack(ss_l), jnp.stack(gcp_l), jnp.stack(gcs_l),
            jnp.stack(ccp_l), jnp.stack(ccs_l))
```

```python
import functools
import math

import numpy as np
import jax
import jax.numpy as jnp
from jax import lax
from jax.experimental import pallas as pl
from jax.experimental.pallas import tpu as pltpu

F32 = jnp.float32
BF = jnp.bfloat16
I32 = jnp.int32

D_MODEL = 2048
DEPTH = 4
PAGE = 128
HEAD_DIM = 128
N_Q = 8
N_KV = 4
ATTN_W = N_Q * HEAD_DIM
KV_W = N_KV * HEAD_DIM
IDX_H = 16
IDX_D = 64
TOPK_MAX = 256
ROPE_THETA = 10000.0
GDN_H = 4
GDN_D = 128
GDN_W = GDN_H * GDN_D
GDN_CONV = 4
GDN_CONV_DIM = 3 * GDN_W
GDN_CHUNK = 64
SC_W = 512
SC_CONV = 3
FFN_DIM = 5632
DN_ALPHA = (2 * DEPTH) ** 0.25
LN_EPS = 1e-5
NORM_EPS = 1e-6
IN_SIZES = (ATTN_W, KV_W, KV_W, IDX_H * IDX_D, IDX_D, IDX_H, GDN_CONV_DIM, GDN_W, GDN_H, GDN_H,
            SC_W, SC_W, SC_W)

C_AQ, C_AK, C_AV, C_IQ, C_GQKV, C_GZ, C_SB, C_SC, C_SX, C_SM = (
    0, 1024, 1536, 2048, 3072, 4608, 5120, 5632, 6144, 6656)
SM_IK, SM_IW, SM_GB, SM_GA = 0, 64, 80, 84
IN_PACKED = 6912

NEG_INF = float("-inf")
INT_MIN = -2 ** 31

VMEM_LIMIT = 56 * 1024 * 1024


def _cp(*sem):
    return pltpu.CompilerParams(dimension_semantics=tuple(sem), vmem_limit_bytes=VMEM_LIMIT)


def _sigmoid(x):
    return 1.0 / (1.0 + jnp.exp(-x))


def _softplus(x):
    return jnp.maximum(x, 0.0) + jnp.log1p(jnp.exp(-jnp.abs(x)))


def _dot(a, b):
    return jnp.dot(a, b, preferred_element_type=F32)


def _dot_hi(a, b):
    return jnp.dot(a, b, preferred_element_type=F32, precision=lax.Precision.HIGHEST)


def _dot_nt(a, b):
    return lax.dot_general(a, b, (((1,), (1,)), ((), ())), preferred_element_type=F32)


def _pack_w_in(w_in):
    cuts = np.cumsum(IN_SIZES)[:-1].tolist()
    aq, ak, av, iq, ik, iw, gqkv, gz, gb, ga, sb, sc, sx = jnp.split(w_in, cuts, axis=-1)
    used = C_SM + IDX_D + IDX_H + 2 * GDN_H
    pad = jnp.zeros(w_in.shape[:-1] + (IN_PACKED - used,), w_in.dtype)
    packed = jnp.concatenate([aq, ak, av, iq, gqkv, gz, sb, sc, sx, ik, iw, gb, ga, pad], axis=-1)
    return packed.astype(BF)


def _mm_kernel(x_ref, w_ref, o_ref):
    o_ref[...] = _dot(x_ref[...].astype(BF), w_ref[...])


def _matmul(x, w, tm, tn):
    m, k = x.shape
    n = w.shape[1]
    return pl.pallas_call(
        _mm_kernel,
        grid=(m // tm, n // tn),
        in_specs=[pl.BlockSpec((tm, k), lambda i, j: (i, 0)),
                  pl.BlockSpec((k, tn), lambda i, j: (0, j))],
        out_specs=pl.BlockSpec((tm, tn), lambda i, j: (i, j)),
        out_shape=jax.ShapeDtypeStruct((m, n), F32),
        compiler_params=_cp("parallel", "arbitrary"),
        name="proj_in",
    )(x, w)


def _rope_tables(pos):
    def tab(half):
        inv = ROPE_THETA ** (-jnp.arange(half, dtype=F32) / half)
        ang = pos.astype(F32)[:, None] * inv
        return jnp.cos(ang), jnp.sin(ang)
    c, s = tab(HEAD_DIM // 2)
    c128 = jnp.concatenate([c, c], -1)
    s128 = jnp.concatenate([-s, s], -1)
    c, s = tab(IDX_D // 2)
    c64 = jnp.concatenate([c, c, c, c], -1)
    s64 = jnp.concatenate([-s, s, -s, s], -1)
    return c128, s128, c64, s64


def _rope128(x, cos, sin):
    return x * cos + pltpu.roll(x, HEAD_DIM // 2, 1) * sin


def _rope64x2(x, cos, sin):
    lane = lax.broadcasted_iota(I32, x.shape, 1)
    first = (lane % IDX_D) < (IDX_D // 2)
    partner = jnp.where(first, pltpu.roll(x, 128 - IDX_D // 2, 1), pltpu.roll(x, IDX_D // 2, 1))
    return x * cos + partner * sin


def _attn_prep_kernel(aq, ak, av, iq, sm, c128, s128, c64, s64,
                      kr_o, kb_o, vt_o, qt_o, iqt_o, ikr_o, ikb_o, iwt_o):
    cos = c128[...]
    sin = s128[...]
    for h in range(N_Q):
        sl = slice(h * HEAD_DIM, (h + 1) * HEAD_DIM)
        y = _rope128(aq[:, sl], cos, sin) * (HEAD_DIM ** -0.5)
        qt_o[sl, :] = y.T.astype(BF)
    for h in range(N_KV):
        sl = slice(h * HEAD_DIM, (h + 1) * HEAD_DIM)
        y = _rope128(ak[:, sl], cos, sin)
        kr_o[:, sl] = y
        kb_o[:, sl] = y.astype(BF)
        vt_o[sl, :] = av[:, sl].T.astype(BF)
    cos = c64[...]
    sin = s64[...]
    for j in range(IDX_H * IDX_D // 128):
        sl = slice(j * 128, (j + 1) * 128)
        y = _rope64x2(iq[:, sl], cos, sin) * (IDX_D ** -0.5)
        iqt_o[sl, :] = y.T.astype(BF)
    x = sm[...]
    y = _rope64x2(x, cos, sin)
    ikr_o[...] = y[:, SM_IK:SM_IK + IDX_D]
    ikb_o[...] = y[:, SM_IK:SM_IK + IDX_D].astype(BF)
    iwt_o[...] = x.T[SM_IW:SM_IW + IDX_H, :] * (IDX_H ** -0.5)


def _attn_prep(h3, tabs, tr):
    b, t, _ = h3.shape
    nr = t // tr
    col = lambda w, off: pl.BlockSpec((None, tr, w), lambda bi, r, o=off // w: (bi, r, o))
    tab = pl.BlockSpec((tr, 128), lambda bi, r: (r, 0))
    return pl.pallas_call(
        _attn_prep_kernel,
        grid=(b, nr),
        in_specs=[col(ATTN_W, C_AQ), col(KV_W, C_AK), col(KV_W, C_AV), col(ATTN_W, C_IQ), col(128, C_SM),
                  tab, tab, tab, tab],
        out_specs=[
            pl.BlockSpec((None, tr, KV_W), lambda bi, r: (bi, r, 0)),
            pl.BlockSpec((None, tr, KV_W), lambda bi, r: (bi, r, 0)),
            pl.BlockSpec((None, None, KV_W, tr), lambda bi, r: (bi, r, 0, 0)),
            pl.BlockSpec((None, ATTN_W, tr), lambda bi, r: (bi, 0, r)),
            pl.BlockSpec((None, ATTN_W, tr), lambda bi, r: (bi, 0, r)),
            pl.BlockSpec((None, tr, IDX_D), lambda bi, r: (bi, r, 0)),
            pl.BlockSpec((None, tr, IDX_D), lambda bi, r: (bi, r, 0)),
            pl.BlockSpec((None, IDX_H, tr), lambda bi, r: (bi, 0, r)),
        ],
        out_shape=[
            jax.ShapeDtypeStruct((b, t, KV_W), F32),
            jax.ShapeDtypeStruct((b, t, KV_W), BF),
            jax.ShapeDtypeStruct((b, nr, KV_W, tr), BF),
            jax.ShapeDtypeStruct((b, ATTN_W, t), BF),
            jax.ShapeDtypeStruct((b, ATTN_W, t), BF),
            jax.ShapeDtypeStruct((b, t, IDX_D), F32),
            jax.ShapeDtypeStruct((b, t, IDX_D), BF),
            jax.ShapeDtypeStruct((b, IDX_H, t), F32),
        ],
        compiler_params=_cp("parallel", "parallel"),
        name="attn_prep",
    )(h3, h3, h3, h3, h3, *tabs)


def _sortable(x):
    bits = pltpu.bitcast(x, I32)
    return jnp.where(bits >= 0, bits, bits ^ jnp.int32(0x7FFFFFFF))


def _kth_largest(count_ge, shape, k):
    lo = jnp.where(count_ge(jnp.zeros(shape, I32)) >= k, 0, INT_MIN).astype(I32)

    def body(it, lo):
        cand = lo + (jnp.int32(1) << (30 - it))
        return jnp.where(count_ge(cand) >= k, cand, lo)

    return lax.fori_loop(0, 31, body, lo)


def _dsa_prompt_kernel(qt, iqt, iwt, kb, vt, ikb, o_ref, keys, bias, lg, *, tq, topk):
    i = pl.program_id(1)
    nch = i + 1
    row = lax.broadcasted_iota(I32, (tq, tq), 0)
    lane = lax.broadcasted_iota(I32, (tq, tq), 1)

    def off_of(c):
        return pl.multiple_of(c * tq, tq)

    def causal_of(c):
        return (c * tq + row) <= (i * tq + lane)

    def fold8(x):
        return x.reshape(tq // 8, 8, tq)

    def indexer(c, carry):
        off = off_of(c)
        ikc = ikb[pl.ds(off, tq), :]
        acc = jnp.zeros((tq, tq), F32)
        for h in range(IDX_H):
            s = _dot(ikc, iqt[h * IDX_D:(h + 1) * IDX_D, :])
            acc = acc + jnp.maximum(s, 0.0) * iwt[h:h + 1, :]
        acc = jnp.where(causal_of(c), acc, NEG_INF)
        keys[pl.ds(off, tq), :] = _sortable(acc)
        return carry

    lax.fori_loop(0, nch, indexer, 0)

    def count_ge(cand):
        def body(c, cnt):
            kc = keys[pl.ds(off_of(c), tq), :]
            return cnt + fold8(jnp.where(kc >= cand, 1, 0)).sum(axis=0)
        cnt = lax.fori_loop(0, nch, body, jnp.zeros((8, tq), I32))
        return cnt.sum(axis=0, keepdims=True)

    theta = _kth_largest(count_ge, (1, tq), topk)

    def make_bias(c, carry):
        off = off_of(c)
        sel = keys[pl.ds(off, tq), :] >= theta
        bias[pl.ds(off, tq), :] = jnp.where(causal_of(c), jnp.where(sel, 0.0, NEG_INF), NEG_INF)
        return carry

    lax.fori_loop(0, nch, make_bias, 0)

    for h in range(N_Q):
        g = h // (N_Q // N_KV)
        qh = qt[h * HEAD_DIM:(h + 1) * HEAD_DIM, :]
        gsl = slice(g * HEAD_DIM, (g + 1) * HEAD_DIM)

        def logits(c, m8):
            off = off_of(c)
            s = _dot(kb[pl.ds(off, tq), gsl], qh) + bias[pl.ds(off, tq), :]
            lg[pl.ds(off, tq), :] = s
            return jnp.maximum(m8, fold8(s).max(axis=0))

        m8 = lax.fori_loop(0, nch, logits, jnp.full((8, tq), NEG_INF, F32))
        m = m8.max(axis=0, keepdims=True)

        def attend(c, carry):
            l8, acc = carry
            p = jnp.exp(lg[pl.ds(off_of(c), tq), :] - m)
            l8 = l8 + fold8(p).sum(axis=0)
            acc = acc + _dot(vt[c, gsl, :], p.astype(BF))
            return l8, acc

        l8, acc = lax.fori_loop(0, nch, attend,
                                (jnp.zeros((8, tq), F32), jnp.zeros((HEAD_DIM, tq), F32)))
        inv_l = 1.0 / l8.sum(axis=0, keepdims=True)
        o_ref[:, h * HEAD_DIM:(h + 1) * HEAD_DIM] = (acc * inv_l).T.astype(BF)


def _dsa_prompt(qt, iqt, iwt, kb, vt, ikb, tq):
    b, _, t = qt.shape
    topk = min(TOPK_MAX, t // 4)
    kern = functools.partial(_dsa_prompt_kernel, tq=tq, topk=topk)
    return pl.pallas_call(
        kern,
        grid=(b, t // tq),
        in_specs=[
            pl.BlockSpec((None, ATTN_W, tq), lambda bi, i: (bi, 0, i)),
            pl.BlockSpec((None, ATTN_W, tq), lambda bi, i: (bi, 0, i)),
            pl.BlockSpec((None, IDX_H, tq), lambda bi, i: (bi, 0, i)),
            pl.BlockSpec((None, t, KV_W), lambda bi, i: (bi, 0, 0)),
            pl.BlockSpec((None, t // tq, KV_W, tq), lambda bi, i: (bi, 0, 0, 0)),
            pl.BlockSpec((None, t, IDX_D), lambda bi, i: (bi, 0, 0)),
        ],
        out_specs=pl.BlockSpec((None, tq, ATTN_W), lambda bi, i: (bi, i, 0)),
        out_shape=jax.ShapeDtypeStruct((b, t, ATTN_W), BF),
        scratch_shapes=[pltpu.VMEM((t, tq), I32), pltpu.VMEM((t, tq), F32), pltpu.VMEM((t, tq), F32)],
        compiler_params=_cp("parallel", "arbitrary"),
        name="dsa_prompt",
    )(qt, iqt, iwt, kb, vt, ikb)


def _l2norm(x):
    return x * lax.rsqrt(jnp.sum(x * x, -1, keepdims=True) + NORM_EPS)


def _gdn_activations(y, small, alog, dtb):
    y = y * _sigmoid(y)
    qs, ks, gs, bs = [], [], [], []
    r = y.shape[0]
    for h in range(GDN_H):
        sl = slice(h * GDN_D, (h + 1) * GDN_D)
        qs.append(_l2norm(y[:, sl]) * (GDN_D ** -0.5))
        ks.append(_l2norm(y[:, GDN_W + h * GDN_D:GDN_W + (h + 1) * GDN_D]))
        b_raw = jnp.broadcast_to(small[:, SM_GB + h:SM_GB + h + 1], (r, GDN_D))
        a_raw = jnp.broadcast_to(small[:, SM_GA + h:SM_GA + h + 1], (r, GDN_D))
        bs.append(_sigmoid(b_raw))
        gs.append(-jnp.exp(alog[:, sl]) * _softplus(a_raw + dtb[:, sl]))
    return qs, ks, y[:, 2 * GDN_W:], gs, bs


def _mix_prep_kernel(gq, gq_prev, sm, sb, sc, sx, sc_prev, sx_prev, gw, sw, alog, dtb,
                     qn_o, kn_o, vv_o, g_o, beta_o, scm_o, utail_o, xbuf, ubuf, *, tr):
    r = pl.program_id(1)
    first = r == 0
    xbuf[0:8, :] = jnp.where(first, 0.0, gq_prev[...])
    xbuf[8:, :] = gq[...]
    y = xbuf[5:5 + tr, :] * gw[0:1, :]
    for j in range(1, GDN_CONV):
        y = y + xbuf[5 + j:5 + j + tr, :] * gw[j:j + 1, :]
    qs, ks, v, gs, bs = _gdn_activations(y, sm[...], alog[...], dtb[...])
    for h in range(GDN_H):
        sl = slice(h * GDN_D, (h + 1) * GDN_D)
        qn_o[:, sl] = qs[h]
        kn_o[:, sl] = ks[h]
        g_o[:, sl] = gs[h]
        beta_o[:, sl] = bs[h]
    vv_o[...] = v

    u = sc[...] * sx[...]
    ubuf[0:8, :] = jnp.where(first, 0.0, sc_prev[...] * sx_prev[...])
    ubuf[8:, :] = u
    y = ubuf[6:6 + tr, :] * sw[0:1, :]
    for j in range(1, SC_CONV):
        y = y + ubuf[6 + j:6 + j + tr, :] * sw[j:j + 1, :]
    scm_o[...] = (sb[...] * y).astype(BF)

    @pl.when(r == pl.num_programs(1) - 1)
    def _():
        utail_o[...] = u[tr - 8:, :]


def _mix_prep(h3, gw, sw, alog, dtb, tr):
    b, t, _ = h3.shape
    nr = t // tr
    col = lambda w, off: pl.BlockSpec((None, tr, w), lambda bi, r, o=off // w: (bi, r, o))
    prev = lambda w, off: pl.BlockSpec(
        (None, 8, w), lambda bi, r, o=off // w: (bi, jnp.maximum(r * (tr // 8) - 1, 0), o))
    full = lambda a: pl.BlockSpec(a.shape, lambda bi, r: (0, 0))
    out = pl.BlockSpec((None, tr, GDN_W), lambda bi, r: (bi, r, 0))
    act = jax.ShapeDtypeStruct((b, t, GDN_W), F32)
    return pl.pallas_call(
        functools.partial(_mix_prep_kernel, tr=tr),
        grid=(b, nr),
        in_specs=[col(GDN_CONV_DIM, C_GQKV), prev(GDN_CONV_DIM, C_GQKV), col(128, C_SM),
                  col(SC_W, C_SB), col(SC_W, C_SC), col(SC_W, C_SX), prev(SC_W, C_SC), prev(SC_W, C_SX),
                  full(gw), full(sw), full(alog), full(dtb)],
        out_specs=[out, out, out, out, out, out,
                   pl.BlockSpec((None, 8, SC_W), lambda bi, r: (bi, 0, 0))],
        out_shape=[act, act, act, act, act,
                   jax.ShapeDtypeStruct((b, t, SC_W), BF),
                   jax.ShapeDtypeStruct((b, 8, SC_W), F32)],
        scratch_shapes=[pltpu.VMEM((tr + 8, GDN_CONV_DIM), F32), pltpu.VMEM((tr + 8, SC_W), F32)],
        compiler_params=_cp("parallel", "arbitrary"),
        name="mix_prep",
    )(h3, h3, h3, h3, h3, h3, h3, h3, gw, sw, alog, dtb)


def _cumsum_rows(x):
    row = lax.broadcasted_iota(I32, x.shape, 0)
    s = 1
    while s < x.shape[0]:
        x = x + jnp.where(row >= s, pltpu.roll(x, s, 0), 0.0)
        s *= 2
    return x


def _inv_unit_lower(a, eye):
    n = a.shape[0]
    nm = -a
    x = eye + nm
    p = nm
    s = 2
    while s < n:
        p = _dot_hi(p, p)
        x = x + _dot_hi(x, p)
        s *= 2
    return x


def _gated_rmsnorm(o, z, norm_g):
    on = o * lax.rsqrt(jnp.mean(o * o, -1, keepdims=True) + NORM_EPS) * norm_g
    return on * (z * _sigmoid(z))


def _gdn_chunk_kernel(qn, kn, vv, gb, betab, z, norm_g, o_ref, s_out, s_ref, *, nb):
    c = pl.program_id(0)
    C = GDN_CHUNK

    @pl.when(c == 0)
    def _():
        s_ref[...] = jnp.zeros_like(s_ref)

    ii = lax.broadcasted_iota(I32, (C, C), 0)
    jj = lax.broadcasted_iota(I32, (C, C), 1)
    lower = ii >= jj
    strict = ii > jj
    eye = jnp.where(ii == jj, 1.0, 0.0).astype(F32)
    zpad = jnp.zeros((GDN_D - C, GDN_D), F32)
    ng = norm_g[...]

    for b in range(nb):
        for h in range(GDN_H):
            sl = slice(h * GDN_D, (h + 1) * GDN_D)
            q = qn[b, :, sl]
            k = kn[b, :, sl]
            v = vv[b, :, sl]
            beta = betab[b, :, sl]
            gc = _cumsum_rows(gb[b, :, sl])
            gct = jnp.concatenate([gc, gc], axis=0).T[0:C, 0:C]
            diff = gc[:, 0:C] - gct
            decay = jnp.where(lower, jnp.exp(jnp.where(lower, diff, 0.0)), 0.0)
            kbeta = k * beta
            kbf = k.astype(BF)
            a = jnp.where(strict, _dot_nt(kbeta.astype(BF), kbf) * decay, 0.0)
            tinv = _inv_unit_lower(a, eye)
            u = _dot_hi(tinv, v * beta)
            w = _dot_hi(tinv, kbeta * jnp.exp(gc))
            s = s_ref[b, h]
            sbf = s.astype(BF)
            v_new = u - _dot(w.astype(BF), sbf)
            vnb = v_new.astype(BF)
            intra = _dot_nt(q.astype(BF), kbf) * decay
            o = _dot((q * jnp.exp(gc)).astype(BF), sbf) + _dot(intra.astype(BF), vnb)
            g_last = gc[C - 1:C, :]
            kd = k * jnp.exp(g_last - gc)
            kdt = jnp.concatenate([kd, zpad], axis=0).T.astype(BF)
            vnp = jnp.concatenate([v_new, zpad], axis=0).astype(BF)
            s_ref[b, h] = s * jnp.exp(g_last) + _dot(kdt, vnp)
            o_ref[b, :, sl] = _gated_rmsnorm(o, z[b, :, sl], ng).astype(BF)

    @pl.when(c == pl.num_programs(0) - 1)
    def _():
        s_out[...] = s_ref[...]


def _gdn_chunked(qn, kn, vv, gb, betab, h3, norm_g):
    b, t, _ = qn.shape
    C = GDN_CHUNK
    blk = pl.BlockSpec((b, C, GDN_W), lambda c: (0, c, 0))
    return pl.pallas_call(
        functools.partial(_gdn_chunk_kernel, nb=b),
        grid=(t // C,),
        in_specs=[blk, blk, blk, blk, blk,
                  pl.BlockSpec((b, C, GDN_W), lambda c: (0, c, C_GZ // GDN_W)),
                  pl.BlockSpec((1, GDN_D), lambda c: (0, 0))],
        out_specs=[pl.BlockSpec((b, C, GDN_W), lambda c: (0, c, 0)),
                   pl.BlockSpec((b, GDN_H, GDN_D, GDN_D), lambda c: (0, 0, 0, 0))],
        out_shape=[jax.ShapeDtypeStruct((b, t, GDN_W), BF),
                   jax.ShapeDtypeStruct((b, GDN_H, GDN_D, GDN_D), F32)],
        scratch_shapes=[pltpu.VMEM((b, GDN_H, GDN_D, GDN_D), F32)],
        compiler_params=_cp("arbitrary"),
        name="gdn_chunked",
    )(qn, kn, vv, gb, betab, h3, norm_g)


def _layer_norm(y, g, b):
    mu = jnp.mean(y, -1, keepdims=True)
    yc = y - mu
    var = jnp.mean(yc * yc, -1, keepdims=True)
    return yc * lax.rsqrt(var + LN_EPS) * g + b


def _out_proj_kernel(attn, gdn, scm, x, wo, g, b, o_ref):
    y = _dot(attn[...], wo[0:ATTN_W, :])
    y = y + _dot(gdn[...], wo[ATTN_W:ATTN_W + GDN_W, :])
    y = y + _dot(scm[...], wo[ATTN_W + GDN_W:, :])
    o_ref[...] = _layer_norm(DN_ALPHA * x[...] + y, g[...], b[...])


def _out_proj(attn, gdn, scm, x, wo, g, b, tm):
    m = x.shape[0]
    row = lambda w: pl.BlockSpec((tm, w), lambda i: (i, 0))
    full = lambda a: pl.BlockSpec(a.shape, lambda i: (0, 0))
    return pl.pallas_call(
        _out_proj_kernel,
        grid=(m // tm,),
        in_specs=[row(ATTN_W), row(GDN_W), row(SC_W), row(D_MODEL), full(wo), full(g), full(b)],
        out_specs=row(D_MODEL),
        out_shape=jax.ShapeDtypeStruct((m, D_MODEL), F32),
        compiler_params=_cp("parallel"),
        name="out_proj_ln",
    )(attn, gdn, scm, x, wo, g, b)


def _ffn_kernel(h, wg, wu, wd, g, b, o_ref, hb, acc):
    f = pl.program_id(1)

    @pl.when(f == 0)
    def _():
        hb[...] = h[...].astype(BF)
        acc[...] = jnp.zeros_like(acc)

    x = hb[...]
    gate = _dot(x, wg[...])
    hid = gate * _sigmoid(gate) * _dot(x, wu[...])
    acc[...] += _dot(hid.astype(BF), wd[...])

    @pl.when(f == pl.num_programs(1) - 1)
    def _():
        o_ref[...] = _layer_norm(DN_ALPHA * h[...] + acc[...], g[...], b[...])


def _ffn(h, wg, wu, wd, g, b, tm, tf):
    m = h.shape[0]
    full = lambda a: pl.BlockSpec(a.shape, lambda i, f: (0, 0))
    return pl.pallas_call(
        _ffn_kernel,
        grid=(m // tm, FFN_DIM // tf),
        in_specs=[pl.BlockSpec((tm, D_MODEL), lambda i, f: (i, 0)),
                  pl.BlockSpec((D_MODEL, tf), lambda i, f: (0, f)),
                  pl.BlockSpec((D_MODEL, tf), lambda i, f: (0, f)),
                  pl.BlockSpec((tf, D_MODEL), lambda i, f: (f, 0)),
                  full(g), full(b)],
        out_specs=pl.BlockSpec((tm, D_MODEL), lambda i, f: (i, 0)),
        out_shape=jax.ShapeDtypeStruct((m, D_MODEL), F32),
        scratch_shapes=[pltpu.VMEM((tm, D_MODEL), BF), pltpu.VMEM((tm, D_MODEL), F32)],
        compiler_params=_cp("parallel", "arbitrary"),
        name="ffn_ln",
    )(h, wg, wu, wd, g, b)


def _head_rows(v):
    return jnp.repeat(v.astype(F32), GDN_D)[None, :]


def _prompt_layer(xp, tabs, lw, tm_in=1024, tn_in=768, tr=256, tm_out=512, tm_ffn=512, tf=512):
    b, t, _ = xp.shape
    x2 = xp.reshape(b * t, D_MODEL)
    h = _matmul(x2, lw["w_in"], min(tm_in, b * t), tn_in)
    h3 = h.reshape(b, t, IN_PACKED)
    kr, kb, vt, qt, iqt, ikr, ikb, iwt = _attn_prep(h3, tabs, tr)
    attn = _dsa_prompt(qt, iqt, iwt, kb, vt, ikb, tr)
    qn, kn, vv, gb, betab, scm, utail = _mix_prep(h3, lw["gdn_conv_w"], lw["sc_conv_w"],
                                                 lw["alog"], lw["dtb"], tr)
    gdn, s_new = _gdn_chunked(qn, kn, vv, gb, betab, h3, lw["norm_g"])
    h1 = _out_proj(attn.reshape(b * t, ATTN_W), gdn.reshape(b * t, GDN_W), scm.reshape(b * t, SC_W),
                   x2, lw["w_out"], lw["ln1_g"], lw["ln1_b"], min(tm_out, b * t))
    y = _ffn(h1, lw["w_gate"], lw["w_up"], lw["w_down"], lw["ln2_g"], lw["ln2_b"], min(tm_ffn, b * t), tf)
    outs = dict(
        k=kr.reshape(b, t, N_KV, HEAD_DIM),
        v=h3[:, :, C_AV:C_AV + KV_W].reshape(b, t, N_KV, HEAD_DIM),
        kidx=ikr,
        s=s_new,
        gconv=h3[:, t - (GDN_CONV - 1):, C_GQKV:C_GQKV + GDN_CONV_DIM],
        sconv=utail[:, 8 - (SC_CONV - 1):, :],
        attn=attn, gdn=gdn, scm=scm,
    )
    return y.reshape(b, t, D_MODEL), outs


def _sample_prep_kernel(h, c128, s128, c64, s64, gst, sst, s_in, gw, sw, alog, dtb, norm_g,
                        q_o, k_o, iq_o, ik_o, iw_o, gst_o, sst_o, s_o, gdn_o, scm_o, o_buf, *, nb):
    cos = c128[...]
    sin = s128[...]
    for hh in range(N_Q):
        sl = slice(hh * HEAD_DIM, (hh + 1) * HEAD_DIM)
        q_o[:, sl] = _rope128(h[:, C_AQ + hh * HEAD_DIM:C_AQ + (hh + 1) * HEAD_DIM], cos, sin) * (HEAD_DIM ** -0.5)
    for hh in range(N_KV):
        sl = slice(hh * HEAD_DIM, (hh + 1) * HEAD_DIM)
        k_o[:, sl] = _rope128(h[:, C_AK + hh * HEAD_DIM:C_AK + (hh + 1) * HEAD_DIM], cos, sin)
    cos = c64[...]
    sin = s64[...]
    for j in range(IDX_H * IDX_D // 128):
        sl = slice(j * 128, (j + 1) * 128)
        iq_o[:, sl] = _rope64x2(h[:, C_IQ + j * 128:C_IQ + (j + 1) * 128], cos, sin) * (IDX_D ** -0.5)
    small = h[:, C_SM:C_SM + 128]
    ik_o[...] = _rope64x2(small, cos, sin)[:, SM_IK:SM_IK + IDX_D]
    iw_o[...] = small[:, SM_IW:SM_IW + IDX_H] * (IDX_H ** -0.5)

    gq = h[:, C_GQKV:C_GQKV + GDN_CONV_DIM]
    y = gq * gw[GDN_CONV - 1:GDN_CONV, :]
    for j in range(GDN_CONV - 1):
        y = y + gst[j] * gw[j:j + 1, :]
    for j in range(GDN_CONV - 2):
        gst_o[j] = gst[j + 1]
    gst_o[GDN_CONV - 2] = gq
    qs, ks, v, gs, bs = _gdn_activations(y, small, alog[...], dtb[...])
    for b in range(nb):
        for hh in range(GDN_H):
            sl = slice(hh * GDN_D, (hh + 1) * GDN_D)
            kc = jnp.broadcast_to(ks[hh][b:b + 1, :], (GDN_D, GDN_D)).T
            qc = jnp.broadcast_to(qs[hh][b:b + 1, :], (GDN_D, GDN_D)).T
            s = s_in[b, hh] * jnp.exp(gs[hh][b:b + 1, :])
            ks_row = jnp.sum(kc * s, axis=0, keepdims=True)
            delta = (v[b:b + 1, sl] - ks_row) * bs[hh][b:b + 1, :]
            s = s + kc * delta
            s_o[b, hh] = s
            o_buf[b:b + 1, sl] = jnp.sum(qc * s, axis=0, keepdims=True)
    ng = norm_g[...]
    for hh in range(GDN_H):
        sl = slice(hh * GDN_D, (hh + 1) * GDN_D)
        z = h[:, C_GZ + hh * GDN_D:C_GZ + (hh + 1) * GDN_D]
        gdn_o[:, sl] = _gated_rmsnorm(o_buf[:, sl], z, ng).astype(BF)

    u = h[:, C_SC:C_SC + SC_W] * h[:, C_SX:C_SX + SC_W]
    y = u * sw[SC_CONV - 1:SC_CONV, :]
    for j in range(SC_CONV - 1):
        y = y + sst[j] * sw[j:j + 1, :]
    for j in range(SC_CONV - 2):
        sst_o[j] = sst[j + 1]
    sst_o[SC_CONV - 2] = u
    scm_o[...] = (h[:, C_SB:C_SB + SC_W] * y).astype(BF)


def _sample_prep(h, tabs, gst, sst, s_in, gw, sw, alog, dtb, norm_g):
    nb = h.shape[0]
    sds = jax.ShapeDtypeStruct
    return pl.pallas_call(
        functools.partial(_sample_prep_kernel, nb=nb),
        out_shape=[sds((nb, ATTN_W), F32), sds((nb, KV_W), F32), sds((nb, IDX_H * IDX_D), F32),
                   sds((nb, IDX_D), F32), sds((nb, IDX_H), F32),
                   sds(gst.shape, F32), sds(sst.shape, F32), sds(s_in.shape, F32),
                   sds((nb, GDN_W), BF), sds((nb, SC_W), BF)],
        scratch_shapes=[pltpu.VMEM((nb, GDN_W), F32)],
        compiler_params=pltpu.CompilerParams(vmem_limit_bytes=VMEM_LIMIT),
        name="sample_prep",
    )(h, *tabs, gst, sst, s_in, gw, sw, alog, dtb, norm_g)


def _sample_index_kernel(pt, *refs, pps, topk):
    pages = refs[:pps]
    iq, iw, ikn, bias_o, seln_o, keys = refs[pps:]
    s = pl.program_id(1)
    iqb = iq[...]
    w = jnp.broadcast_to(iw[...], (IDX_H, PAGE))
    for j in range(pps):
        sc = _dot_nt(iqb, pages[j][...].astype(BF))
        row = jnp.sum(jnp.maximum(sc, 0.0) * w, axis=0, keepdims=True)
        keys[pl.ds(s * pps + j, 1), :] = _sortable(row)

    @pl.when(s == pl.num_programs(1) - 1)
    def _():
        prod = iqb.astype(F32) * ikn[...].astype(BF).astype(F32)
        sn = jnp.sum(prod, axis=1, keepdims=True)
        sn = jnp.sum(jnp.maximum(sn, 0.0) * iw[...], axis=0, keepdims=True)
        key_new = jnp.broadcast_to(_sortable(sn), (1, PAGE))
        kall = keys[...]

        def count_ge(cand):
            c = jnp.where(kall >= cand, 1.0, 0.0).sum(axis=0, keepdims=True).sum(axis=1, keepdims=True)
            return jnp.broadcast_to(c, (1, PAGE)) + jnp.where(key_new >= cand, 1.0, 0.0)

        theta = _kth_largest(count_ge, (1, PAGE), float(topk))
        bias_o[...] = jnp.where(kall >= theta, 0.0, NEG_INF)
        seln_o[...] = jnp.where(key_new >= theta, 0.0, NEG_INF)


def _sample_index(page_table, cache_kidx, layer, iq3, iw3, ikn3, pps=16):
    nb, npages = page_table.shape
    topk = min(TOPK_MAX, (npages * PAGE + 1) // 4)
    page = lambda j: pl.BlockSpec((None, None, PAGE, IDX_D),
                                  lambda b, s, pt, j=j: (layer, pt[b, s * pps + j], 0, 0))
    per_b = lambda shp: pl.BlockSpec((None,) + shp, lambda b, s, pt: (b, 0, 0))
    grid_spec = pltpu.PrefetchScalarGridSpec(
        num_scalar_prefetch=1,
        grid=(nb, npages // pps),
        in_specs=[page(j) for j in range(pps)] + [per_b((IDX_H, IDX_D)), per_b((IDX_H, 1)), per_b((1, IDX_D))],
        out_specs=[per_b((npages, PAGE)), per_b((1, PAGE))],
        scratch_shapes=[pltpu.VMEM((npages, PAGE), I32)],
    )
    return pl.pallas_call(
        functools.partial(_sample_index_kernel, pps=pps, topk=topk),
        grid_spec=grid_spec,
        out_shape=[jax.ShapeDtypeStruct((nb, npages, PAGE), F32), jax.ShapeDtypeStruct((nb, 1, PAGE), F32)],
        compiler_params=_cp("parallel", "arbitrary"),
        name="sample_index",
    )(page_table, *([cache_kidx] * pps), iq3, iw3, ikn3)


def _sample_attn_kernel(pt, *refs, pps):
    kp = refs[:pps]
    vp = refs[pps:2 * pps]
    qbd, bias, seln, kn, vn, o_ref, m_ref, l_ref, acc_ref = refs[2 * pps:]
    s = pl.program_id(1)

    @pl.when(s == 0)
    def _():
        m_ref[...] = jnp.full_like(m_ref, NEG_INF)
        l_ref[...] = jnp.zeros_like(l_ref)
        acc_ref[...] = jnp.zeros_like(acc_ref)

    q = qbd[...]
    lg = [_dot_nt(q, kp[j][...].astype(BF)) + bias[pl.ds(s * pps + j, 1), :] for j in range(pps)]
    m_old = m_ref[...]
    m_new = m_old
    for x in lg:
        m_new = jnp.maximum(m_new, jnp.max(x, axis=1, keepdims=True))
    m_safe = jnp.where(m_new == NEG_INF, 0.0, m_new)
    alpha = jnp.exp(m_old - m_safe)
    l_new = l_ref[...] * alpha
    acc = acc_ref[...] * alpha
    for j in range(pps):
        p = jnp.exp(lg[j] - m_safe)
        l_new = l_new + jnp.sum(p, axis=1, keepdims=True)
        acc = acc + _dot(p.astype(BF), vp[j][...].astype(BF))
    m_ref[...] = m_new
    l_ref[...] = l_new
    acc_ref[...] = acc

    @pl.when(s == pl.num_programs(1) - 1)
    def _():
        knb = kn[...].astype(BF).astype(F32)
        vnb = vn[...].astype(BF).astype(F32)
        x = jnp.sum(q.astype(F32) * knb, axis=1, keepdims=True) + seln[:, 0:1]
        m_fin = jnp.maximum(m_new, x)
        a = jnp.exp(m_new - m_fin)
        p = jnp.exp(x - m_fin)
        l_fin = l_new * a + p
        acc_fin = acc * a + p.astype(BF).astype(F32) * vnb
        out = acc_fin / l_fin
        head = lax.broadcasted_iota(I32, (N_Q, HEAD_DIM), 0) // (N_Q // N_KV)
        res = jnp.zeros((N_Q, HEAD_DIM), F32)
        for g in range(N_KV):
            res = res + jnp.where(head == g, out[:, g * HEAD_DIM:(g + 1) * HEAD_DIM], 0.0)
        o_ref[...] = res.astype(BF)


def _sample_attn(page_table, cache_k, cache_v, layer, qbd, bias, seln, kn3, vn3, pps=8):
    nb, npages = page_table.shape
    page = lambda j: pl.BlockSpec((None, None, PAGE, KV_W),
                                  lambda b, s, pt, j=j: (layer, pt[b, s * pps + j], 0, 0))
    per_b = lambda shp: pl.BlockSpec((None,) + shp, lambda b, s, pt: (b, 0, 0))
    grid_spec = pltpu.PrefetchScalarGridSpec(
        num_scalar_prefetch=1,
        grid=(nb, npages // pps),
        in_specs=[page(j) for j in range(pps)] * 2
                 + [per_b((N_Q, KV_W)), per_b((npages, PAGE)), per_b((1, PAGE)), per_b((1, KV_W)), per_b((1, KV_W))],
        out_specs=per_b((N_Q, HEAD_DIM)),
        scratch_shapes=[pltpu.VMEM((N_Q, 1), F32), pltpu.VMEM((N_Q, 1), F32), pltpu.VMEM((N_Q, KV_W), F32)],
    )
    return pl.pallas_call(
        functools.partial(_sample_attn_kernel, pps=pps),
        grid_spec=grid_spec,
        out_shape=jax.ShapeDtypeStruct((nb, N_Q, HEAD_DIM), BF),
        compiler_params=_cp("parallel", "arbitrary"),
        name="sample_attn",
    )(page_table, *([cache_k] * pps), *([cache_v] * pps), qbd, bias, seln, kn3, vn3)


SAMPLE_ROWS = 16


def _sample_layer(xs, tabs, lw, layer, page_table, cache_k, cache_v, cache_kidx, gst, sst, s_in):
    nb = page_table.shape[0]
    h = _matmul(xs, lw["w_in"], SAMPLE_ROWS, 768)[:nb]
    q, k, iq, ik, iw, gst_n, sst_n, s_n, gdn, scm = _sample_prep(
        h, tabs, gst, sst, s_in, lw["gdn_conv_w"], lw["sc_conv_w"], lw["alog"], lw["dtb"], lw["norm_g"])
    v = h[:, C_AV:C_AV + KV_W]
    bias, seln = _sample_index(page_table, cache_kidx, layer, iq.reshape(nb, IDX_H, IDX_D).astype(BF),
                               iw.reshape(nb, IDX_H, 1), ik.reshape(nb, 1, IDX_D))
    q4 = q.reshape(nb, N_Q, 1, HEAD_DIM)
    onehot = (jnp.arange(N_Q)[:, None] // (N_Q // N_KV) == jnp.arange(N_KV)[None, :])
    qbd = jnp.where(onehot[None, :, :, None], q4, 0.0).reshape(nb, N_Q, KV_W).astype(BF)
    attn = _sample_attn(page_table, cache_k, cache_v, layer, qbd, bias, seln,
                        k.reshape(nb, 1, KV_W), v.reshape(nb, 1, KV_W)).reshape(nb, ATTN_W)
    padr = lambda a: jnp.pad(a, ((0, SAMPLE_ROWS - nb), (0, 0)))
    h1 = _out_proj(padr(attn), padr(gdn), padr(scm), xs, lw["w_out"], lw["ln1_g"], lw["ln1_b"], SAMPLE_ROWS)
    y = _ffn(h1, lw["w_gate"], lw["w_up"], lw["w_down"], lw["ln2_g"], lw["ln2_b"], SAMPLE_ROWS, 512)
    outs = dict(k=k.reshape(nb, 1, N_KV, HEAD_DIM), v=v.reshape(nb, 1, N_KV, HEAD_DIM), kidx=ik.reshape(nb, 1, IDX_D),
                s=s_n, gconv=jnp.swapaxes(gst_n, 0, 1), sconv=jnp.swapaxes(sst_n, 0, 1),
                attn=attn, gdn=gdn, scm=scm)
    return y, outs


def kernel(x_prompt, x_sample, cache_k, cache_v, cache_kidx, page_table, state_gdn, state_gdn_conv,
           state_sc_conv, w_in, gdn_conv_w, gdn_a_log, gdn_dt_bias, gdn_norm_g, sc_conv_w, w_out,
           ln1_g, ln1_b, w_gate, w_up, w_down, ln2_g, ln2_b):
    bp, tp, _ = x_prompt.shape
    nb, ts, _ = x_sample.shape
    assert ts == 1
    past = page_table.shape[1] * PAGE
    tabs_p = _rope_tables(jnp.arange(tp))
    tabs_s = _rope_tables(past + jnp.arange(ts))
    w_in_p = _pack_w_in(w_in)
    w_out_b, w_gate_b, w_up_b, w_down_b = (w.astype(BF) for w in (w_out, w_gate, w_up, w_down))
    ck = cache_k.reshape(cache_k.shape[:3] + (KV_W,))
    cv = cache_v.reshape(cache_v.shape[:3] + (KV_W,))

    xp = x_prompt
    xs = jnp.pad(x_sample.reshape(nb, D_MODEL), ((0, SAMPLE_ROWS - nb), (0, 0)))
    po, so = [], []
    for l in range(DEPTH):
        lw = dict(w_in=w_in_p[l], gdn_conv_w=gdn_conv_w[l], sc_conv_w=sc_conv_w[l],
                  alog=_head_rows(gdn_a_log[l]), dtb=_head_rows(gdn_dt_bias[l]), norm_g=gdn_norm_g[l][None, :],
                  w_out=w_out_b[l], ln1_g=ln1_g[l][None, :], ln1_b=ln1_b[l][None, :],
                  w_gate=w_gate_b[l], w_up=w_up_b[l], w_down=w_down_b[l],
                  ln2_g=ln2_g[l][None, :], ln2_b=ln2_b[l][None, :])
        xp, o = _prompt_layer(xp, tabs_p, lw)
        po.append(o)
        xs, o = _sample_layer(xs, tabs_s, lw, l, page_table, ck, cv, cache_kidx,
                              jnp.swapaxes(state_gdn_conv[l], 0, 1), jnp.swapaxes(state_sc_conv[l], 0, 1),
                              state_gdn[l])
        so.append(o)

    st = lambda outs, n: jnp.stack([o[n] for o in outs])
    return (xp, xs[:nb].reshape(nb, ts, D_MODEL),
            st(po, "k"), st(po, "v"), st(po, "kidx"), st(so, "k"), st(so, "v"), st(so, "kidx"),
            st(po, "s"), st(so, "s"), st(po, "gconv"), st(so, "gconv"), st(po, "sconv"), st(so, "sconv"))
```

```python
import functools
import math

import numpy as np
import jax
import jax.numpy as jnp
from jax import lax
from jax.experimental import pallas as pl
from jax.experimental.pallas import tpu as pltpu

F32 = jnp.float32
BF = jnp.bfloat16
I32 = jnp.int32

D_MODEL = 2048
DEPTH = 4
PAGE = 128
HEAD_DIM = 128
N_Q = 8
N_KV = 4
ATTN_W = N_Q * HEAD_DIM
KV_W = N_KV * HEAD_DIM
IDX_H = 16
IDX_D = 64
TOPK_MAX = 256
ROPE_THETA = 10000.0
GDN_H = 4
GDN_D = 128
GDN_W = GDN_H * GDN_D
GDN_CONV = 4
GDN_CONV_DIM = 3 * GDN_W
GDN_CHUNK = 64
SC_W = 512
SC_CONV = 3
FFN_DIM = 5632
DN_ALPHA = (2 * DEPTH) ** 0.25
LN_EPS = 1e-5
NORM_EPS = 1e-6
IN_SIZES = (ATTN_W, KV_W, KV_W, IDX_H * IDX_D, IDX_D, IDX_H, GDN_CONV_DIM, GDN_W, GDN_H, GDN_H,
            SC_W, SC_W, SC_W)

C_AQ, C_AK, C_AV, C_IQ, C_GQKV, C_GZ, C_SB, C_SC, C_SX, C_SM = (
    0, 1024, 1536, 2048, 3072, 4608, 5120, 5632, 6144, 6656)
SM_IK, SM_IW, SM_GB, SM_GA = 0, 64, 80, 84
IN_PACKED = 6912

NEG_INF = float("-inf")
MASKED = -1e30
Q_SCALE_LOG2 = HEAD_DIM ** -0.5 * math.log2(math.e)
INT_MIN = -2 ** 31

VMEM_LIMIT = 56 * 1024 * 1024


def _cp(*sem):
    return pltpu.CompilerParams(dimension_semantics=tuple(sem), vmem_limit_bytes=VMEM_LIMIT)


def _sigmoid(x):
    return 1.0 / (1.0 + jnp.exp(-x))


def _softplus(x):
    return jnp.maximum(x, 0.0) + jnp.log1p(jnp.exp(-jnp.abs(x)))


def _dot(a, b):
    return jnp.dot(a, b, preferred_element_type=F32)


def _dot_hi(a, b):
    return jnp.dot(a, b, preferred_element_type=F32, precision=lax.Precision.HIGHEST)


def _dot_nt(a, b):
    return lax.dot_general(a, b, (((1,), (1,)), ((), ())), preferred_element_type=F32)


def _pack_w_in(w_in):
    cuts = np.cumsum(IN_SIZES)[:-1].tolist()
    aq, ak, av, iq, ik, iw, gqkv, gz, gb, ga, sb, sc, sx = jnp.split(w_in, cuts, axis=-1)
    used = C_SM + IDX_D + IDX_H + 2 * GDN_H
    pad = jnp.zeros(w_in.shape[:-1] + (IN_PACKED - used,), w_in.dtype)
    packed = jnp.concatenate([aq, ak, av, iq, gqkv, gz, sb, sc, sx, ik, iw, gb, ga, pad], axis=-1)
    return packed.astype(BF)


def _mm_kernel(x_ref, w_ref, o_ref):
    o_ref[...] = _dot(x_ref[...].astype(BF), w_ref[...])


def _matmul(x, w, tm, tn):
    m, k = x.shape
    n = w.shape[1]
    return pl.pallas_call(
        _mm_kernel,
        grid=(m // tm, n // tn),
        in_specs=[pl.BlockSpec((tm, k), lambda i, j: (i, 0)),
                  pl.BlockSpec((k, tn), lambda i, j: (0, j))],
        out_specs=pl.BlockSpec((tm, tn), lambda i, j: (i, j)),
        out_shape=jax.ShapeDtypeStruct((m, n), F32),
        compiler_params=_cp("parallel", "arbitrary"),
        name="proj_in",
    )(x, w)


def _rope_tables(pos):
    def tab(half):
        inv = ROPE_THETA ** (-jnp.arange(half, dtype=F32) / half)
        ang = pos.astype(F32)[:, None] * inv
        return jnp.cos(ang), jnp.sin(ang)
    c, s = tab(HEAD_DIM // 2)
    c128 = jnp.concatenate([c, c], -1)
    s128 = jnp.concatenate([-s, s], -1)
    c, s = tab(IDX_D // 2)
    c64 = jnp.concatenate([c, c, c, c], -1)
    s64 = jnp.concatenate([-s, s, -s, s], -1)
    return c128, s128, c64, s64


def _rope128(x, cos, sin):
    return x * cos + pltpu.roll(x, HEAD_DIM // 2, 1) * sin


def _rope64x2(x, cos, sin):
    lane = lax.broadcasted_iota(I32, x.shape, 1)
    first = (lane % IDX_D) < (IDX_D // 2)
    partner = jnp.where(first, pltpu.roll(x, 128 - IDX_D // 2, 1), pltpu.roll(x, IDX_D // 2, 1))
    return x * cos + partner * sin


def _attn_prep_kernel(aq, ak, av, iq, sm, c128, s128, c64, s64,
                      kr_o, kb_o, vt_o, qt_o, iqt_o, ikr_o, ikb_o, iwt_o):
    cos = c128[...]
    sin = s128[...]
    for h in range(N_Q):
        sl = slice(h * HEAD_DIM, (h + 1) * HEAD_DIM)
        y = _rope128(aq[:, sl], cos, sin) * Q_SCALE_LOG2
        qt_o[sl, :] = y.T.astype(BF)
    for h in range(N_KV):
        sl = slice(h * HEAD_DIM, (h + 1) * HEAD_DIM)
        y = _rope128(ak[:, sl], cos, sin)
        kr_o[:, sl] = y
        kb_o[:, sl] = y.astype(BF)
        vt_o[sl, :] = av[:, sl].T.astype(BF)
    cos = c64[...]
    sin = s64[...]
    for j in range(IDX_H * IDX_D // 128):
        sl = slice(j * 128, (j + 1) * 128)
        y = _rope64x2(iq[:, sl], cos, sin) * (IDX_D ** -0.5)
        iqt_o[sl, :] = y.T.astype(BF)
    x = sm[...]
    y = _rope64x2(x, cos, sin)
    ikr_o[...] = y[:, SM_IK:SM_IK + IDX_D]
    ikb_o[...] = y[:, SM_IK:SM_IK + IDX_D].astype(BF)
    iwt_o[...] = x.T[SM_IW:SM_IW + IDX_H, :] * (IDX_H ** -0.5)


def _attn_prep(h3, tabs, tr):
    b, t, _ = h3.shape
    nr = t // tr
    col = lambda w, off: pl.BlockSpec((None, tr, w), lambda bi, r, o=off // w: (bi, r, o))
    tab = pl.BlockSpec((tr, 128), lambda bi, r: (r, 0))
    return pl.pallas_call(
        _attn_prep_kernel,
        grid=(b, nr),
        in_specs=[col(ATTN_W, C_AQ), col(KV_W, C_AK), col(KV_W, C_AV), col(ATTN_W, C_IQ), col(128, C_SM),
                  tab, tab, tab, tab],
        out_specs=[
            pl.BlockSpec((None, tr, KV_W), lambda bi, r: (bi, r, 0)),
            pl.BlockSpec((None, tr, KV_W), lambda bi, r: (bi, r, 0)),
            pl.BlockSpec((None, None, KV_W, tr), lambda bi, r: (bi, r, 0, 0)),
            pl.BlockSpec((None, ATTN_W, tr), lambda bi, r: (bi, 0, r)),
            pl.BlockSpec((None, ATTN_W, tr), lambda bi, r: (bi, 0, r)),
            pl.BlockSpec((None, tr, IDX_D), lambda bi, r: (bi, r, 0)),
            pl.BlockSpec((None, tr, IDX_D), lambda bi, r: (bi, r, 0)),
            pl.BlockSpec((None, IDX_H, tr), lambda bi, r: (bi, 0, r)),
        ],
        out_shape=[
            jax.ShapeDtypeStruct((b, t, KV_W), F32),
            jax.ShapeDtypeStruct((b, t, KV_W), BF),
            jax.ShapeDtypeStruct((b, nr, KV_W, tr), BF),
            jax.ShapeDtypeStruct((b, ATTN_W, t), BF),
            jax.ShapeDtypeStruct((b, ATTN_W, t), BF),
            jax.ShapeDtypeStruct((b, t, IDX_D), F32),
            jax.ShapeDtypeStruct((b, t, IDX_D), BF),
            jax.ShapeDtypeStruct((b, IDX_H, t), F32),
        ],
        compiler_params=_cp("parallel", "parallel"),
        name="attn_prep",
    )(h3, h3, h3, h3, h3, *tabs)


def _sortable(x):
    bits = pltpu.bitcast(x, I32)
    return jnp.where(bits >= 0, bits, bits ^ jnp.int32(0x7FFFFFFF))


def _kth_largest(count_ge, shape, k):
    lo = jnp.where(count_ge(jnp.zeros(shape, I32)) >= k, 0, INT_MIN).astype(I32)

    def body(it, lo):
        cand = lo + (jnp.int32(1) << (30 - it))
        return jnp.where(count_ge(cand) >= k, cand, lo)

    return lax.fori_loop(0, 31, body, lo)


def _dsa_prompt_kernel(qt, iqt, iwt, kb, vt, ikb, o_ref, keys, bias, m_ref, l_ref, acc_ref, s_ref, *, tq, topk):
    i = pl.program_id(1)
    nch = i + 1
    row = lax.broadcasted_iota(I32, (tq, tq), 0)
    lane = lax.broadcasted_iota(I32, (tq, tq), 1)

    def off_of(c):
        return pl.multiple_of(c * tq, tq)

    def causal_of(c):
        return (c * tq + row) <= (i * tq + lane)

    def fold8(x):
        return x.reshape(tq // 8, 8, tq)

    def indexer(c, carry):
        off = off_of(c)
        ikc = ikb[pl.ds(off, tq), :]
        acc = jnp.zeros((tq, tq), F32)
        for h in range(IDX_H):
            s = _dot(ikc, iqt[h * IDX_D:(h + 1) * IDX_D, :])
            acc = acc + jnp.maximum(s, 0.0) * iwt[h:h + 1, :]
        acc = jnp.where(causal_of(c), acc, NEG_INF)
        keys[pl.ds(off, tq), :] = _sortable(acc)
        return carry

    lax.fori_loop(0, nch, indexer, 0)

    def count_ge(cand):
        def body(c, cnt):
            kc = keys[pl.ds(off_of(c), tq), :]
            return cnt + fold8(jnp.where(kc >= cand, 1, 0)).sum(axis=0)
        cnt = lax.fori_loop(0, nch, body, jnp.zeros((8, tq), I32))
        return cnt.sum(axis=0, keepdims=True)

    theta = _kth_largest(count_ge, (1, tq), topk)

    def make_bias(c, carry):
        off = off_of(c)
        sel = keys[pl.ds(off, tq), :] >= theta
        bias[pl.ds(off, tq), :] = jnp.where(causal_of(c), jnp.where(sel, 0.0, MASKED), MASKED)
        return carry

    lax.fori_loop(0, nch, make_bias, 0)

    m_ref[...] = jnp.full_like(m_ref, NEG_INF)
    l_ref[...] = jnp.zeros_like(l_ref)
    acc_ref[...] = jnp.zeros_like(acc_ref)

    def attend(c, carry):
        off = off_of(c)
        bc = bias[pl.ds(off, tq), :]
        m_new = []
        for h in range(N_Q):
            gsl = slice(h // (N_Q // N_KV) * HEAD_DIM, (h // (N_Q // N_KV) + 1) * HEAD_DIM)
            s = _dot(kb[pl.ds(off, tq), gsl], qt[h * HEAD_DIM:(h + 1) * HEAD_DIM, :]) + bc
            s_ref[h] = s
            m_new.append(jnp.maximum(m_ref[h], fold8(s).max(axis=0).max(axis=0, keepdims=True)))
        for h in range(N_Q):
            gsl = slice(h // (N_Q // N_KV) * HEAD_DIM, (h // (N_Q // N_KV) + 1) * HEAD_DIM)
            p = jnp.exp2(s_ref[h] - m_new[h])
            alpha = jnp.exp2(m_ref[h] - m_new[h])
            l_ref[h] = l_ref[h] * alpha + fold8(p).sum(axis=0).sum(axis=0, keepdims=True)
            acc_ref[h] = acc_ref[h] * alpha + _dot(vt[c, gsl, :], p.astype(BF))
            m_ref[h] = m_new[h]
        return carry

    lax.fori_loop(0, nch, attend, 0)

    for h in range(N_Q):
        o_ref[:, h * HEAD_DIM:(h + 1) * HEAD_DIM] = (acc_ref[h] * (1.0 / l_ref[h])).T.astype(BF)


def _dsa_prompt(qt, iqt, iwt, kb, vt, ikb, tq):
    b, _, t = qt.shape
    topk = min(TOPK_MAX, t // 4)
    kern = functools.partial(_dsa_prompt_kernel, tq=tq, topk=topk)
    return pl.pallas_call(
        kern,
        grid=(b, t // tq),
        in_specs=[
            pl.BlockSpec((None, ATTN_W, tq), lambda bi, i: (bi, 0, i)),
            pl.BlockSpec((None, ATTN_W, tq), lambda bi, i: (bi, 0, i)),
            pl.BlockSpec((None, IDX_H, tq), lambda bi, i: (bi, 0, i)),
            pl.BlockSpec((None, t, KV_W), lambda bi, i: (bi, 0, 0)),
            pl.BlockSpec((None, t // tq, KV_W, tq), lambda bi, i: (bi, 0, 0, 0)),
            pl.BlockSpec((None, t, IDX_D), lambda bi, i: (bi, 0, 0)),
        ],
        out_specs=pl.BlockSpec((None, tq, ATTN_W), lambda bi, i: (bi, i, 0)),
        out_shape=jax.ShapeDtypeStruct((b, t, ATTN_W), BF),
        scratch_shapes=[pltpu.VMEM((t, tq), I32), pltpu.VMEM((t, tq), F32),
                        pltpu.VMEM((N_Q, 1, tq), F32), pltpu.VMEM((N_Q, 1, tq), F32),
                        pltpu.VMEM((N_Q, HEAD_DIM, tq), F32), pltpu.VMEM((N_Q, tq, tq), F32)],
        compiler_params=_cp("parallel", "arbitrary"),
        name="dsa_prompt",
    )(qt, iqt, iwt, kb, vt, ikb)


def _l2norm(x):
    return x * lax.rsqrt(jnp.sum(x * x, -1, keepdims=True) + NORM_EPS)


def _gdn_activations(y, small, alog, dtb):
    y = y * _sigmoid(y)
    qs, ks, gs, bs = [], [], [], []
    r = y.shape[0]
    for h in range(GDN_H):
        sl = slice(h * GDN_D, (h + 1) * GDN_D)
        qs.append(_l2norm(y[:, sl]) * (GDN_D ** -0.5))
        ks.append(_l2norm(y[:, GDN_W + h * GDN_D:GDN_W + (h + 1) * GDN_D]))
        b_raw = jnp.broadcast_to(small[:, SM_GB + h:SM_GB + h + 1], (r, GDN_D))
        a_raw = jnp.broadcast_to(small[:, SM_GA + h:SM_GA + h + 1], (r, GDN_D))
        bs.append(_sigmoid(b_raw))
        gs.append(-jnp.exp(alog[:, sl]) * _softplus(a_raw + dtb[:, sl]))
    return qs, ks, y[:, 2 * GDN_W:], gs, bs


def _mix_prep_kernel(gq, gq_prev, sm, sb, sc, sx, sc_prev, sx_prev, gw, sw, alog, dtb,
                     qn_o, kn_o, vv_o, g_o, beta_o, scm_o, utail_o, xbuf, ubuf, *, tr):
    r = pl.program_id(1)
    first = r == 0
    xbuf[0:8, :] = jnp.where(first, 0.0, gq_prev[...])
    xbuf[8:, :] = gq[...]
    y = xbuf[5:5 + tr, :] * gw[0:1, :]
    for j in range(1, GDN_CONV):
        y = y + xbuf[5 + j:5 + j + tr, :] * gw[j:j + 1, :]
    qs, ks, v, gs, bs = _gdn_activations(y, sm[...], alog[...], dtb[...])
    for h in range(GDN_H):
        sl = slice(h * GDN_D, (h + 1) * GDN_D)
        qn_o[:, sl] = qs[h]
        kn_o[:, sl] = ks[h]
        g_o[:, sl] = gs[h]
        beta_o[:, sl] = bs[h]
    vv_o[...] = v

    u = sc[...] * sx[...]
    ubuf[0:8, :] = jnp.where(first, 0.0, sc_prev[...] * sx_prev[...])
    ubuf[8:, :] = u
    y = ubuf[6:6 + tr, :] * sw[0:1, :]
    for j in range(1, SC_CONV):
        y = y + ubuf[6 + j:6 + j + tr, :] * sw[j:j + 1, :]
    scm_o[...] = (sb[...] * y).astype(BF)

    @pl.when(r == pl.num_programs(1) - 1)
    def _():
        utail_o[...] = u[tr - 8:, :]


def _mix_prep(h3, gw, sw, alog, dtb, tr):
    b, t, _ = h3.shape
    nr = t // tr
    col = lambda w, off: pl.BlockSpec((None, tr, w), lambda bi, r, o=off // w: (bi, r, o))
    prev = lambda w, off: pl.BlockSpec(
        (None, 8, w), lambda bi, r, o=off // w: (bi, jnp.maximum(r * (tr // 8) - 1, 0), o))
    full = lambda a: pl.BlockSpec(a.shape, lambda bi, r: (0, 0))
    out = pl.BlockSpec((None, tr, GDN_W), lambda bi, r: (bi, r, 0))
    act = jax.ShapeDtypeStruct((b, t, GDN_W), F32)
    return pl.pallas_call(
        functools.partial(_mix_prep_kernel, tr=tr),
        grid=(b, nr),
        in_specs=[col(GDN_CONV_DIM, C_GQKV), prev(GDN_CONV_DIM, C_GQKV), col(128, C_SM),
                  col(SC_W, C_SB), col(SC_W, C_SC), col(SC_W, C_SX), prev(SC_W, C_SC), prev(SC_W, C_SX),
                  full(gw), full(sw), full(alog), full(dtb)],
        out_specs=[out, out, out, out, out, out,
                   pl.BlockSpec((None, 8, SC_W), lambda bi, r: (bi, 0, 0))],
        out_shape=[act, act, act, act, act,
                   jax.ShapeDtypeStruct((b, t, SC_W), BF),
                   jax.ShapeDtypeStruct((b, 8, SC_W), F32)],
        scratch_shapes=[pltpu.VMEM((tr + 8, GDN_CONV_DIM), F32), pltpu.VMEM((tr + 8, SC_W), F32)],
        compiler_params=_cp("parallel", "arbitrary"),
        name="mix_prep",
    )(h3, h3, h3, h3, h3, h3, h3, h3, gw, sw, alog, dtb)


def _cumsum_rows(x):
    row = lax.broadcasted_iota(I32, x.shape, 0)
    s = 1
    while s < x.shape[0]:
        x = x + jnp.where(row >= s, pltpu.roll(x, s, 0), 0.0)
        s *= 2
    return x


def _split_bf(x):
    hi = x.astype(BF)
    return hi, (x - hi.astype(F32)).astype(BF)


def _dot3(a, b):
    m = a.shape[0]
    ah, al = _split_bf(a)
    bh, bl = _split_bf(b)
    r = _dot(jnp.concatenate([ah, al], axis=0), bh)
    return r[:m] + r[m:] + _dot(ah, bl)


def _gated_rmsnorm(o, z, norm_g):
    on = o * lax.rsqrt(jnp.mean(o * o, -1, keepdims=True) + NORM_EPS) * norm_g
    return on * (z * _sigmoid(z))


def _gdn_chunk_kernel(qn, kn, vv, gb, betab, z, norm_g, o_ref, s_out, s_ref, *, nb):
    c = pl.program_id(0)
    C = GDN_CHUNK
    W = GDN_H * C

    @pl.when(c == 0)
    def _():
        s_ref[...] = jnp.zeros_like(s_ref)

    ii = lax.broadcasted_iota(I32, (C, W), 0)
    jj = lax.broadcasted_iota(I32, (C, W), 1) % C
    lower = ii >= jj
    strict = ii > jj
    eye = jnp.where(ii == jj, 1.0, 0.0).astype(F32)
    blockmask = (lax.broadcasted_iota(I32, (W, W), 0) // C) == (lax.broadcasted_iota(I32, (W, W), 1) // C)
    zpad = jnp.zeros((GDN_D - C, GDN_D), F32)
    ng = norm_g[...]
    heads = [slice(h * GDN_D, (h + 1) * GDN_D) for h in range(GDN_H)]
    lanes = lambda parts: jnp.concatenate(parts, axis=1)

    def block_diag(m):
        return jnp.where(blockmask, jnp.concatenate([m] * GDN_H, axis=0), 0.0)

    st = []
    for b in range(nb):
        q = [qn[b, :, sl] for sl in heads]
        k = [kn[b, :, sl] for sl in heads]
        beta = [betab[b, :, sl] for sl in heads]
        gc = [_cumsum_rows(gb[b, :, sl]) for sl in heads]
        kbeta = [k[h] * beta[h] for h in range(GDN_H)]
        gcol = lanes([g[:, 0:C] for g in gc])
        grow = lanes([jnp.concatenate([g, g], axis=0).T[0:C, 0:C] for g in gc])
        decay = jnp.where(lower, jnp.exp(jnp.where(lower, gcol - grow, 0.0)), 0.0)
        kk, qk = [], []
        for h in range(GDN_H):
            r = _dot_nt(jnp.concatenate([kbeta[h], q[h]], axis=0).astype(BF), k[h].astype(BF))
            kk.append(r[:C])
            qk.append(r[C:])
        a = jnp.where(strict, lanes(kk) * decay, 0.0)
        intra = lanes(qk) * decay
        rhs = jnp.concatenate(
            [lanes([vv[b, :, heads[h]] * beta[h], kbeta[h] * jnp.exp(gc[h])]) for h in range(GDN_H)], axis=0)
        st.append(dict(q=q, k=k, gc=gc, intra=intra, rhs=rhs, nm=-a))

    for d in st:
        d["x"] = eye + d["nm"]
        d["p"] = _dot3(d["nm"], block_diag(d["nm"]))
    step = 4
    while step < C:
        for d in st:
            r = _dot3(jnp.concatenate([d["x"], d["p"]], axis=0), block_diag(d["p"]))
            d["x"] = d["x"] + r[:C]
            d["p"] = r[C:]
        step *= 2
    for d in st:
        tinv = d["x"] + _dot3(d["x"], block_diag(d["p"]))
        d["uw"] = _dot3(block_diag(tinv), d["rhs"])

    for b in range(nb):
        d = st[b]
        for h in range(GDN_H):
            u = d["uw"][h * C:(h + 1) * C, 0:GDN_D]
            w = d["uw"][h * C:(h + 1) * C, GDN_D:]
            gc = d["gc"][h]
            s = s_ref[b, h]
            r = _dot(jnp.concatenate([w, d["q"][h] * jnp.exp(gc)], axis=0).astype(BF), s.astype(BF))
            v_new = u - r[:C]
            o = r[C:] + _dot(d["intra"][:, h * C:(h + 1) * C].astype(BF), v_new.astype(BF))
            g_last = gc[C - 1:C, :]
            kd = d["k"][h] * jnp.exp(g_last - gc)
            kdt = jnp.concatenate([kd, zpad], axis=0).T.astype(BF)
            vnp = jnp.concatenate([v_new, zpad], axis=0).astype(BF)
            s_ref[b, h] = s * jnp.exp(g_last) + _dot(kdt, vnp)
            o_ref[b, :, heads[h]] = _gated_rmsnorm(o, z[b, :, heads[h]], ng).astype(BF)

    @pl.when(c == pl.num_programs(0) - 1)
    def _():
        s_out[...] = s_ref[...]


def _gdn_chunked(qn, kn, vv, gb, betab, h3, norm_g):
    b, t, _ = qn.shape
    C = GDN_CHUNK
    blk = pl.BlockSpec((b, C, GDN_W), lambda c: (0, c, 0))
    return pl.pallas_call(
        functools.partial(_gdn_chunk_kernel, nb=b),
        grid=(t // C,),
        in_specs=[blk, blk, blk, blk, blk,
                  pl.BlockSpec((b, C, GDN_W), lambda c: (0, c, C_GZ // GDN_W)),
                  pl.BlockSpec((1, GDN_D), lambda c: (0, 0))],
        out_specs=[pl.BlockSpec((b, C, GDN_W), lambda c: (0, c, 0)),
                   pl.BlockSpec((b, GDN_H, GDN_D, GDN_D), lambda c: (0, 0, 0, 0))],
        out_shape=[jax.ShapeDtypeStruct((b, t, GDN_W), BF),
                   jax.ShapeDtypeStruct((b, GDN_H, GDN_D, GDN_D), F32)],
        scratch_shapes=[pltpu.VMEM((b, GDN_H, GDN_D, GDN_D), F32)],
        compiler_params=_cp("arbitrary"),
        name="gdn_chunked",
    )(qn, kn, vv, gb, betab, h3, norm_g)


def _layer_norm(y, g, b):
    mu = jnp.mean(y, -1, keepdims=True)
    yc = y - mu
    var = jnp.mean(yc * yc, -1, keepdims=True)
    return yc * lax.rsqrt(var + LN_EPS) * g + b


def _out_proj_kernel(attn, gdn, scm, x, wo, g, b, o_ref):
    y = _dot(attn[...], wo[0:ATTN_W, :])
    y = y + _dot(gdn[...], wo[ATTN_W:ATTN_W + GDN_W, :])
    y = y + _dot(scm[...], wo[ATTN_W + GDN_W:, :])
    o_ref[...] = _layer_norm(DN_ALPHA * x[...] + y, g[...], b[...])


def _out_proj(attn, gdn, scm, x, wo, g, b, tm):
    m = x.shape[0]
    row = lambda w: pl.BlockSpec((tm, w), lambda i: (i, 0))
    full = lambda a: pl.BlockSpec(a.shape, lambda i: (0, 0))
    return pl.pallas_call(
        _out_proj_kernel,
        grid=(m // tm,),
        in_specs=[row(ATTN_W), row(GDN_W), row(SC_W), row(D_MODEL), full(wo), full(g), full(b)],
        out_specs=row(D_MODEL),
        out_shape=jax.ShapeDtypeStruct((m, D_MODEL), F32),
        compiler_params=_cp("parallel"),
        name="out_proj_ln",
    )(attn, gdn, scm, x, wo, g, b)


def _ffn_kernel(h, wg, wu, wd, g, b, o_ref, hb, acc):
    f = pl.program_id(1)

    @pl.when(f == 0)
    def _():
        hb[...] = h[...].astype(BF)
        acc[...] = jnp.zeros_like(acc)

    x = hb[...]
    gate = _dot(x, wg[...])
    hid = gate * _sigmoid(gate) * _dot(x, wu[...])
    acc[...] += _dot(hid.astype(BF), wd[...])

    @pl.when(f == pl.num_programs(1) - 1)
    def _():
        o_ref[...] = _layer_norm(DN_ALPHA * h[...] + acc[...], g[...], b[...])


def _ffn(h, wg, wu, wd, g, b, tm, tf):
    m = h.shape[0]
    full = lambda a: pl.BlockSpec(a.shape, lambda i, f: (0, 0))
    return pl.pallas_call(
        _ffn_kernel,
        grid=(m // tm, FFN_DIM // tf),
        in_specs=[pl.BlockSpec((tm, D_MODEL), lambda i, f: (i, 0)),
                  pl.BlockSpec((D_MODEL, tf), lambda i, f: (0, f)),
                  pl.BlockSpec((D_MODEL, tf), lambda i, f: (0, f)),
                  pl.BlockSpec((tf, D_MODEL), lambda i, f: (f, 0)),
                  full(g), full(b)],
        out_specs=pl.BlockSpec((tm, D_MODEL), lambda i, f: (i, 0)),
        out_shape=jax.ShapeDtypeStruct((m, D_MODEL), F32),
        scratch_shapes=[pltpu.VMEM((tm, D_MODEL), BF), pltpu.VMEM((tm, D_MODEL), F32)],
        compiler_params=_cp("parallel", "arbitrary"),
        name="ffn_ln",
    )(h, wg, wu, wd, g, b)


def _head_rows(v):
    return jnp.repeat(v.astype(F32), GDN_D)[None, :]


def _prompt_layer(xp, tabs, lw, tm_in=1024, tn_in=768, tr=256, tm_out=512, tm_ffn=512, tf=512):
    b, t, _ = xp.shape
    x2 = xp.reshape(b * t, D_MODEL)
    h = _matmul(x2, lw["w_in"], min(tm_in, b * t), tn_in)
    h3 = h.reshape(b, t, IN_PACKED)
    kr, kb, vt, qt, iqt, ikr, ikb, iwt = _attn_prep(h3, tabs, tr)
    attn = _dsa_prompt(qt, iqt, iwt, kb, vt, ikb, tr)
    qn, kn, vv, gb, betab, scm, utail = _mix_prep(h3, lw["gdn_conv_w"], lw["sc_conv_w"],
                                                 lw["alog"], lw["dtb"], tr)
    gdn, s_new = _gdn_chunked(qn, kn, vv, gb, betab, h3, lw["norm_g"])
    h1 = _out_proj(attn.reshape(b * t, ATTN_W), gdn.reshape(b * t, GDN_W), scm.reshape(b * t, SC_W),
                   x2, lw["w_out"], lw["ln1_g"], lw["ln1_b"], min(tm_out, b * t))
    y = _ffn(h1, lw["w_gate"], lw["w_up"], lw["w_down"], lw["ln2_g"], lw["ln2_b"], min(tm_ffn, b * t), tf)
    outs = dict(
        k=kr.reshape(b, t, N_KV, HEAD_DIM),
        v=h3[:, :, C_AV:C_AV + KV_W].reshape(b, t, N_KV, HEAD_DIM),
        kidx=ikr,
        s=s_new,
        gconv=h3[:, t - (GDN_CONV - 1):, C_GQKV:C_GQKV + GDN_CONV_DIM],
        sconv=utail[:, 8 - (SC_CONV - 1):, :],
        attn=attn, gdn=gdn, scm=scm,
    )
    return y.reshape(b, t, D_MODEL), outs


def _sample_prep_kernel(h, c128, s128, c64, s64, gst, sst, s_in, gw, sw, alog, dtb, norm_g,
                        q_o, k_o, iq_o, ik_o, iw_o, gst_o, sst_o, s_o, gdn_o, scm_o, o_buf, *, nb):
    cos = c128[...]
    sin = s128[...]
    for hh in range(N_Q):
        sl = slice(hh * HEAD_DIM, (hh + 1) * HEAD_DIM)
        q_o[:, sl] = _rope128(h[:, C_AQ + hh * HEAD_DIM:C_AQ + (hh + 1) * HEAD_DIM], cos, sin) * (HEAD_DIM ** -0.5)
    for hh in range(N_KV):
        sl = slice(hh * HEAD_DIM, (hh + 1) * HEAD_DIM)
        k_o[:, sl] = _rope128(h[:, C_AK + hh * HEAD_DIM:C_AK + (hh + 1) * HEAD_DIM], cos, sin)
    cos = c64[...]
    sin = s64[...]
    for j in range(IDX_H * IDX_D // 128):
        sl = slice(j * 128, (j + 1) * 128)
        iq_o[:, sl] = _rope64x2(h[:, C_IQ + j * 128:C_IQ + (j + 1) * 128], cos, sin) * (IDX_D ** -0.5)
    small = h[:, C_SM:C_SM + 128]
    ik_o[...] = _rope64x2(small, cos, sin)[:, SM_IK:SM_IK + IDX_D]
    iw_o[...] = small[:, SM_IW:SM_IW + IDX_H] * (IDX_H ** -0.5)

    gq = h[:, C_GQKV:C_GQKV + GDN_CONV_DIM]
    y = gq * gw[GDN_CONV - 1:GDN_CONV, :]
    for j in range(GDN_CONV - 1):
        y = y + gst[j] * gw[j:j + 1, :]
    for j in range(GDN_CONV - 2):
        gst_o[j] = gst[j + 1]
    gst_o[GDN_CONV - 2] = gq
    qs, ks, v, gs, bs = _gdn_activations(y, small, alog[...], dtb[...])
    for b in range(nb):
        for hh in range(GDN_H):
            sl = slice(hh * GDN_D, (hh + 1) * GDN_D)
            kc = jnp.broadcast_to(ks[hh][b:b + 1, :], (GDN_D, GDN_D)).T
            qc = jnp.broadcast_to(qs[hh][b:b + 1, :], (GDN_D, GDN_D)).T
            s = s_in[b, hh] * jnp.exp(gs[hh][b:b + 1, :])
            ks_row = jnp.sum(kc * s, axis=0, keepdims=True)
            delta = (v[b:b + 1, sl] - ks_row) * bs[hh][b:b + 1, :]
            s = s + kc * delta
            s_o[b, hh] = s
            o_buf[b:b + 1, sl] = jnp.sum(qc * s, axis=0, keepdims=True)
    ng = norm_g[...]
    for hh in range(GDN_H):
        sl = slice(hh * GDN_D, (hh + 1) * GDN_D)
        z = h[:, C_GZ + hh * GDN_D:C_GZ + (hh + 1) * GDN_D]
        gdn_o[:, sl] = _gated_rmsnorm(o_buf[:, sl], z, ng).astype(BF)

    u = h[:, C_SC:C_SC + SC_W] * h[:, C_SX:C_SX + SC_W]
    y = u * sw[SC_CONV - 1:SC_CONV, :]
    for j in range(SC_CONV - 1):
        y = y + sst[j] * sw[j:j + 1, :]
    for j in range(SC_CONV - 2):
        sst_o[j] = sst[j + 1]
    sst_o[SC_CONV - 2] = u
    scm_o[...] = (h[:, C_SB:C_SB + SC_W] * y).astype(BF)


def _sample_prep(h, tabs, gst, sst, s_in, gw, sw, alog, dtb, norm_g):
    nb = h.shape[0]
    sds = jax.ShapeDtypeStruct
    return pl.pallas_call(
        functools.partial(_sample_prep_kernel, nb=nb),
        out_shape=[sds((nb, ATTN_W), F32), sds((nb, KV_W), F32), sds((nb, IDX_H * IDX_D), F32),
                   sds((nb, IDX_D), F32), sds((nb, IDX_H), F32),
                   sds(gst.shape, F32), sds(sst.shape, F32), sds(s_in.shape, F32),
                   sds((nb, GDN_W), BF), sds((nb, SC_W), BF)],
        scratch_shapes=[pltpu.VMEM((nb, GDN_W), F32)],
        compiler_params=pltpu.CompilerParams(vmem_limit_bytes=VMEM_LIMIT),
        name="sample_prep",
    )(h, *tabs, gst, sst, s_in, gw, sw, alog, dtb, norm_g)


def _sample_index_kernel(pt, *refs, pps, topk):
    pages = refs[:pps]
    iq, iw, ikn, bias_o, seln_o, keys = refs[pps:]
    s = pl.program_id(1)
    iqb = iq[...]
    w = jnp.broadcast_to(iw[...], (IDX_H, PAGE))
    for j in range(pps):
        sc = _dot_nt(iqb, pages[j][...].astype(BF))
        row = jnp.sum(jnp.maximum(sc, 0.0) * w, axis=0, keepdims=True)
        keys[pl.ds(s * pps + j, 1), :] = _sortable(row)

    @pl.when(s == pl.num_programs(1) - 1)
    def _():
        prod = iqb.astype(F32) * ikn[...].astype(BF).astype(F32)
        sn = jnp.sum(prod, axis=1, keepdims=True)
        sn = jnp.sum(jnp.maximum(sn, 0.0) * iw[...], axis=0, keepdims=True)
        key_new = jnp.broadcast_to(_sortable(sn), (1, PAGE))
        kall = keys[...]

        def count_ge(cand):
            c = jnp.where(kall >= cand, 1.0, 0.0).sum(axis=0, keepdims=True).sum(axis=1, keepdims=True)
            return jnp.broadcast_to(c, (1, PAGE)) + jnp.where(key_new >= cand, 1.0, 0.0)

        theta = _kth_largest(count_ge, (1, PAGE), float(topk))
        bias_o[...] = jnp.where(kall >= theta, 0.0, NEG_INF)
        seln_o[...] = jnp.where(key_new >= theta, 0.0, NEG_INF)


def _sample_index(page_table, cache_kidx, layer, iq3, iw3, ikn3, pps=16):
    nb, npages = page_table.shape
    topk = min(TOPK_MAX, (npages * PAGE + 1) // 4)
    page = lambda j: pl.BlockSpec((None, None, PAGE, IDX_D),
                                  lambda b, s, pt, j=j: (layer, pt[b, s * pps + j], 0, 0))
    per_b = lambda shp: pl.BlockSpec((None,) + shp, lambda b, s, pt: (b, 0, 0))
    grid_spec = pltpu.PrefetchScalarGridSpec(
        num_scalar_prefetch=1,
        grid=(nb, npages // pps),
        in_specs=[page(j) for j in range(pps)] + [per_b((IDX_H, IDX_D)), per_b((IDX_H, 1)), per_b((1, IDX_D))],
        out_specs=[per_b((npages, PAGE)), per_b((1, PAGE))],
        scratch_shapes=[pltpu.VMEM((npages, PAGE), I32)],
    )
    return pl.pallas_call(
        functools.partial(_sample_index_kernel, pps=pps, topk=topk),
        grid_spec=grid_spec,
        out_shape=[jax.ShapeDtypeStruct((nb, npages, PAGE), F32), jax.ShapeDtypeStruct((nb, 1, PAGE), F32)],
        compiler_params=_cp("parallel", "arbitrary"),
        name="sample_index",
    )(page_table, *([cache_kidx] * pps), iq3, iw3, ikn3)


def _sample_attn_kernel(pt, *refs, pps):
    kp = refs[:pps]
    vp = refs[pps:2 * pps]
    qbd, bias, seln, kn, vn, o_ref, m_ref, l_ref, acc_ref = refs[2 * pps:]
    s = pl.program_id(1)

    @pl.when(s == 0)
    def _():
        m_ref[...] = jnp.full_like(m_ref, NEG_INF)
        l_ref[...] = jnp.zeros_like(l_ref)
        acc_ref[...] = jnp.zeros_like(acc_ref)

    q = qbd[...]
    gsls = [slice(g * HEAD_DIM, (g + 1) * HEAD_DIM) for g in range(N_KV)]
    lg = []
    for j in range(pps):
        x = bias[pl.ds(s * pps + j, 1), :]
        for g in range(N_KV):
            x = x + _dot_nt(q[:, gsls[g]], kp[j][:, g, :].astype(BF))
        lg.append(x)
    m_old = m_ref[...]
    m_new = m_old
    for x in lg:
        m_new = jnp.maximum(m_new, jnp.max(x, axis=1, keepdims=True))
    m_safe = jnp.where(m_new == NEG_INF, 0.0, m_new)
    alpha = jnp.exp(m_old - m_safe)
    l_new = l_ref[...] * alpha
    accs = [acc_ref[:, gsls[g]] * alpha for g in range(N_KV)]
    for j in range(pps):
        p = jnp.exp(lg[j] - m_safe)
        l_new = l_new + jnp.sum(p, axis=1, keepdims=True)
        pb = p.astype(BF)
        for g in range(N_KV):
            accs[g] = accs[g] + _dot(pb, vp[j][:, g, :].astype(BF))
    acc = jnp.concatenate(accs, axis=1)
    m_ref[...] = m_new
    l_ref[...] = l_new
    acc_ref[...] = acc

    @pl.when(s == pl.num_programs(1) - 1)
    def _():
        knb = kn[...].astype(BF).astype(F32)
        vnb = vn[...].astype(BF).astype(F32)
        x = jnp.sum(q.astype(F32) * knb, axis=1, keepdims=True) + seln[:, 0:1]
        m_fin = jnp.maximum(m_new, x)
        a = jnp.exp(m_new - m_fin)
        p = jnp.exp(x - m_fin)
        l_fin = l_new * a + p
        acc_fin = acc * a + p.astype(BF).astype(F32) * vnb
        out = acc_fin / l_fin
        head = lax.broadcasted_iota(I32, (N_Q, HEAD_DIM), 0) // (N_Q // N_KV)
        res = jnp.zeros((N_Q, HEAD_DIM), F32)
        for g in range(N_KV):
            res = res + jnp.where(head == g, out[:, g * HEAD_DIM:(g + 1) * HEAD_DIM], 0.0)
        o_ref[...] = res.astype(BF)


def _sample_attn(page_table, cache_k, cache_v, layer, qbd, bias, seln, kn3, vn3, pps=8):
    nb, npages = page_table.shape
    page = lambda j: pl.BlockSpec((None, None, PAGE, N_KV, HEAD_DIM),
                                  lambda b, s, pt, j=j: (layer, pt[b, s * pps + j], 0, 0, 0))
    per_b = lambda shp: pl.BlockSpec((None,) + shp, lambda b, s, pt: (b, 0, 0))
    grid_spec = pltpu.PrefetchScalarGridSpec(
        num_scalar_prefetch=1,
        grid=(nb, npages // pps),
        in_specs=[page(j) for j in range(pps)] * 2
                 + [per_b((N_Q, KV_W)), per_b((npages, PAGE)), per_b((1, PAGE)), per_b((1, KV_W)), per_b((1, KV_W))],
        out_specs=per_b((N_Q, HEAD_DIM)),
        scratch_shapes=[pltpu.VMEM((N_Q, 1), F32), pltpu.VMEM((N_Q, 1), F32), pltpu.VMEM((N_Q, KV_W), F32)],
    )
    return pl.pallas_call(
        functools.partial(_sample_attn_kernel, pps=pps),
        grid_spec=grid_spec,
        out_shape=jax.ShapeDtypeStruct((nb, N_Q, HEAD_DIM), BF),
        compiler_params=_cp("parallel", "arbitrary"),
        name="sample_attn",
    )(page_table, *([cache_k] * pps), *([cache_v] * pps), qbd, bias, seln, kn3, vn3)


SAMPLE_ROWS = 16


def _sample_layer(xs, tabs, lw, layer, page_table, cache_k, cache_v, cache_kidx, gst, sst, s_in):
    nb = page_table.shape[0]
    h = _matmul(xs, lw["w_in"], SAMPLE_ROWS, 768)[:nb]
    q, k, iq, ik, iw, gst_n, sst_n, s_n, gdn, scm = _sample_prep(
        h, tabs, gst, sst, s_in, lw["gdn_conv_w"], lw["sc_conv_w"], lw["alog"], lw["dtb"], lw["norm_g"])
    v = h[:, C_AV:C_AV + KV_W]
    bias, seln = _sample_index(page_table, cache_kidx, layer, iq.reshape(nb, IDX_H, IDX_D).astype(BF),
                               iw.reshape(nb, IDX_H, 1), ik.reshape(nb, 1, IDX_D))
    q4 = q.reshape(nb, N_Q, 1, HEAD_DIM)
    onehot = (jnp.arange(N_Q)[:, None] // (N_Q // N_KV) == jnp.arange(N_KV)[None, :])
    qbd = jnp.where(onehot[None, :, :, None], q4, 0.0).reshape(nb, N_Q, KV_W).astype(BF)
    attn = _sample_attn(page_table, cache_k, cache_v, layer, qbd, bias, seln,
                        k.reshape(nb, 1, KV_W), v.reshape(nb, 1, KV_W)).reshape(nb, ATTN_W)
    padr = lambda a: jnp.pad(a, ((0, SAMPLE_ROWS - nb), (0, 0)))
    h1 = _out_proj(padr(attn), padr(gdn), padr(scm), xs, lw["w_out"], lw["ln1_g"], lw["ln1_b"], SAMPLE_ROWS)
    y = _ffn(h1, lw["w_gate"], lw["w_up"], lw["w_down"], lw["ln2_g"], lw["ln2_b"], SAMPLE_ROWS, 512)
    outs = dict(k=k.reshape(nb, 1, N_KV, HEAD_DIM), v=v.reshape(nb, 1, N_KV, HEAD_DIM), kidx=ik.reshape(nb, 1, IDX_D),
                s=s_n, gconv=jnp.swapaxes(gst_n, 0, 1), sconv=jnp.swapaxes(sst_n, 0, 1),
                attn=attn, gdn=gdn, scm=scm)
    return y, outs


def kernel(x_prompt, x_sample, cache_k, cache_v, cache_kidx, page_table, state_gdn, state_gdn_conv,
           state_sc_conv, w_in, gdn_conv_w, gdn_a_log, gdn_dt_bias, gdn_norm_g, sc_conv_w, w_out,
           ln1_g, ln1_b, w_gate, w_up, w_down, ln2_g, ln2_b):
    bp, tp, _ = x_prompt.shape
    nb, ts, _ = x_sample.shape
    assert ts == 1
    past = page_table.shape[1] * PAGE
    tabs_p = _rope_tables(jnp.arange(tp))
    tabs_s = _rope_tables(past + jnp.arange(ts))
    w_in_p = _pack_w_in(w_in)
    w_out_b, w_gate_b, w_up_b, w_down_b = (w.astype(BF) for w in (w_out, w_gate, w_up, w_down))

    xp = x_prompt
    xs = jnp.pad(x_sample.reshape(nb, D_MODEL), ((0, SAMPLE_ROWS - nb), (0, 0)))
    po, so = [], []
    for l in range(DEPTH):
        lw = dict(w_in=w_in_p[l], gdn_conv_w=gdn_conv_w[l], sc_conv_w=sc_conv_w[l],
                  alog=_head_rows(gdn_a_log[l]), dtb=_head_rows(gdn_dt_bias[l]), norm_g=gdn_norm_g[l][None, :],
                  w_out=w_out_b[l], ln1_g=ln1_g[l][None, :], ln1_b=ln1_b[l][None, :],
                  w_gate=w_gate_b[l], w_up=w_up_b[l], w_down=w_down_b[l],
                  ln2_g=ln2_g[l][None, :], ln2_b=ln2_b[l][None, :])
        xp, o = _prompt_layer(xp, tabs_p, lw)
        po.append(o)
        xs, o = _sample_layer(xs, tabs_s, lw, l, page_table, cache_k, cache_v, cache_kidx,
                              jnp.swapaxes(state_gdn_conv[l], 0, 1), jnp.swapaxes(state_sc_conv[l], 0, 1),
                              state_gdn[l])
        so.append(o)

    st = lambda outs, n: jnp.stack([o[n] for o in outs])
    return (xp, xs[:nb].reshape(nb, ts, D_MODEL),
            st(po, "k"), st(po, "v"), st(po, "kidx"), st(so, "k"), st(so, "v"), st(so, "kidx"),
            st(po, "s"), st(so, "s"), st(po, "gconv"), st(so, "gconv"), st(po, "sconv"), st(so, "sconv"))
```

```python
import functools
import math

import numpy as np
import jax
import jax.numpy as jnp
from jax import lax
from jax.experimental import pallas as pl
from jax.experimental.pallas import tpu as pltpu

F32 = jnp.float32
BF = jnp.bfloat16
I32 = jnp.int32

D_MODEL = 2048
DEPTH = 4
PAGE = 128
HEAD_DIM = 128
N_Q = 8
N_KV = 4
ATTN_W = N_Q * HEAD_DIM
KV_W = N_KV * HEAD_DIM
IDX_H = 16
IDX_D = 64
TOPK_MAX = 256
ROPE_THETA = 10000.0
GDN_H = 4
GDN_D = 128
GDN_W = GDN_H * GDN_D
GDN_CONV = 4
GDN_CONV_DIM = 3 * GDN_W
GDN_CHUNK = 64
SC_W = 512
SC_CONV = 3
FFN_DIM = 5632
DN_ALPHA = (2 * DEPTH) ** 0.25
LN_EPS = 1e-5
NORM_EPS = 1e-6
IN_SIZES = (ATTN_W, KV_W, KV_W, IDX_H * IDX_D, IDX_D, IDX_H, GDN_CONV_DIM, GDN_W, GDN_H, GDN_H,
            SC_W, SC_W, SC_W)

C_AQ, C_AK, C_AV, C_IQ, C_GQKV, C_GZ, C_SB, C_SC, C_SX, C_SM = (
    0, 1024, 1536, 2048, 3072, 4608, 5120, 5632, 6144, 6656)
SM_IK, SM_IW, SM_GB, SM_GA = 0, 64, 80, 84
IN_PACKED = 6912

NEG_INF = float("-inf")
MASKED = -1e30
Q_SCALE_LOG2 = HEAD_DIM ** -0.5 * math.log2(math.e)
INT_MIN = -2 ** 31

VMEM_LIMIT = 56 * 1024 * 1024


def _cp(*sem):
    return pltpu.CompilerParams(dimension_semantics=tuple(sem), vmem_limit_bytes=VMEM_LIMIT)


def _sigmoid(x):
    return 1.0 / (1.0 + jnp.exp(-x))


def _softplus(x):
    return jnp.maximum(x, 0.0) + jnp.log1p(jnp.exp(-jnp.abs(x)))


def _dot(a, b):
    return jnp.dot(a, b, preferred_element_type=F32)


def _dot_hi(a, b):
    return jnp.dot(a, b, preferred_element_type=F32, precision=lax.Precision.HIGHEST)


def _dot_nt(a, b):
    return lax.dot_general(a, b, (((1,), (1,)), ((), ())), preferred_element_type=F32)


def _pack_w_in(w_in):
    cuts = np.cumsum(IN_SIZES)[:-1].tolist()
    aq, ak, av, iq, ik, iw, gqkv, gz, gb, ga, sb, sc, sx = jnp.split(w_in.astype(BF), cuts, axis=-1)
    used = C_SM + IDX_D + IDX_H + 2 * GDN_H
    pad = jnp.zeros(w_in.shape[:-1] + (IN_PACKED - used,), BF)
    return jnp.concatenate([aq, ak, av, iq, gqkv, gz, sb, sc, sx, ik, iw, gb, ga, pad], axis=-1)


def _mm_kernel(x_ref, w_ref, o_ref):
    o_ref[...] = _dot(x_ref[...].astype(BF), w_ref[...])


def _matmul(x, w, layer, tm, tn):
    m, k = x.shape
    n = w.shape[2]
    return pl.pallas_call(
        _mm_kernel,
        grid=(m // tm, n // tn),
        in_specs=[pl.BlockSpec((tm, k), lambda i, j: (i, 0)),
                  pl.BlockSpec((None, k, tn), lambda i, j: (layer, 0, j))],
        out_specs=pl.BlockSpec((tm, tn), lambda i, j: (i, j)),
        out_shape=jax.ShapeDtypeStruct((m, n), F32),
        compiler_params=_cp("parallel", "arbitrary"),
        name="proj_in",
    )(x, w)


def _rope_tables(pos):
    def tab(half):
        inv = ROPE_THETA ** (-jnp.arange(half, dtype=F32) / half)
        ang = pos.astype(F32)[:, None] * inv
        return jnp.cos(ang), jnp.sin(ang)
    c, s = tab(HEAD_DIM // 2)
    c128 = jnp.concatenate([c, c], -1)
    s128 = jnp.concatenate([-s, s], -1)
    c, s = tab(IDX_D // 2)
    c64 = jnp.concatenate([c, c, c, c], -1)
    s64 = jnp.concatenate([-s, s, -s, s], -1)
    return c128, s128, c64, s64


def _rope128(x, cos, sin):
    return x * cos + pltpu.roll(x, HEAD_DIM // 2, 1) * sin


def _rope64x2(x, cos, sin):
    lane = lax.broadcasted_iota(I32, x.shape, 1)
    first = (lane % IDX_D) < (IDX_D // 2)
    partner = jnp.where(first, pltpu.roll(x, 128 - IDX_D // 2, 1), pltpu.roll(x, IDX_D // 2, 1))
    return x * cos + partner * sin


def _attn_prep_kernel(aq, ak, av, iq, sm, c128, s128, c64, s64,
                      kr_o, kb_o, vt_o, qt_o, iqt_o, ikr_o, ikb_o, iwt_o):
    cos = c128[...]
    sin = s128[...]
    for h in range(N_Q):
        sl = slice(h * HEAD_DIM, (h + 1) * HEAD_DIM)
        y = _rope128(aq[:, sl], cos, sin) * Q_SCALE_LOG2
        qt_o[sl, :] = y.T.astype(BF)
    for h in range(N_KV):
        sl = slice(h * HEAD_DIM, (h + 1) * HEAD_DIM)
        y = _rope128(ak[:, sl], cos, sin)
        kr_o[:, sl] = y
        kb_o[:, sl] = y.astype(BF)
        vt_o[sl, :] = av[:, sl].T.astype(BF)
    cos = c64[...]
    sin = s64[...]
    for j in range(IDX_H * IDX_D // 128):
        sl = slice(j * 128, (j + 1) * 128)
        y = _rope64x2(iq[:, sl], cos, sin) * (IDX_D ** -0.5)
        iqt_o[sl, :] = y.T.astype(BF)
    x = sm[...]
    y = _rope64x2(x, cos, sin)
    ikr_o[...] = y[:, SM_IK:SM_IK + IDX_D]
    ikb_o[...] = y[:, SM_IK:SM_IK + IDX_D].astype(BF)
    iwt_o[...] = x.T[SM_IW:SM_IW + IDX_H, :] * (IDX_H ** -0.5)


def _attn_prep(h3, tabs, tr):
    b, t, _ = h3.shape
    nr = t // tr
    col = lambda w, off: pl.BlockSpec((None, tr, w), lambda bi, r, o=off // w: (bi, r, o))
    tab = pl.BlockSpec((tr, 128), lambda bi, r: (r, 0))
    return pl.pallas_call(
        _attn_prep_kernel,
        grid=(b, nr),
        in_specs=[col(ATTN_W, C_AQ), col(KV_W, C_AK), col(KV_W, C_AV), col(ATTN_W, C_IQ), col(128, C_SM),
                  tab, tab, tab, tab],
        out_specs=[
            pl.BlockSpec((None, tr, KV_W), lambda bi, r: (bi, r, 0)),
            pl.BlockSpec((None, tr, KV_W), lambda bi, r: (bi, r, 0)),
            pl.BlockSpec((None, None, KV_W, tr), lambda bi, r: (bi, r, 0, 0)),
            pl.BlockSpec((None, ATTN_W, tr), lambda bi, r: (bi, 0, r)),
            pl.BlockSpec((None, ATTN_W, tr), lambda bi, r: (bi, 0, r)),
            pl.BlockSpec((None, tr, IDX_D), lambda bi, r: (bi, r, 0)),
            pl.BlockSpec((None, tr, IDX_D), lambda bi, r: (bi, r, 0)),
            pl.BlockSpec((None, IDX_H, tr), lambda bi, r: (bi, 0, r)),
        ],
        out_shape=[
            jax.ShapeDtypeStruct((b, t, KV_W), F32),
            jax.ShapeDtypeStruct((b, t, KV_W), BF),
            jax.ShapeDtypeStruct((b, nr, KV_W, tr), BF),
            jax.ShapeDtypeStruct((b, ATTN_W, t), BF),
            jax.ShapeDtypeStruct((b, ATTN_W, t), BF),
            jax.ShapeDtypeStruct((b, t, IDX_D), F32),
            jax.ShapeDtypeStruct((b, t, IDX_D), BF),
            jax.ShapeDtypeStruct((b, IDX_H, t), F32),
        ],
        compiler_params=_cp("parallel", "parallel"),
        name="attn_prep",
    )(h3, h3, h3, h3, h3, *tabs)


def _sortable(x):
    bits = pltpu.bitcast(x, I32)
    return jnp.where(bits >= 0, bits, bits ^ jnp.int32(0x7FFFFFFF))


def _kth_largest(count_ge, shape, k):
    lo = jnp.where(count_ge(jnp.zeros(shape, I32)) >= k, 0, INT_MIN).astype(I32)

    def body(it, lo):
        cand = lo + (jnp.int32(1) << (30 - it))
        return jnp.where(count_ge(cand) >= k, cand, lo)

    return lax.fori_loop(0, 31, body, lo)


def _dsa_prompt_kernel(qt, iqt, iwt, kb, vt, ikb, o_ref, keys, bias, m_ref, l_ref, acc_ref, s_ref, *, tq, topk):
    i = pl.program_id(1)
    nch = i + 1
    row = lax.broadcasted_iota(I32, (tq, tq), 0)
    lane = lax.broadcasted_iota(I32, (tq, tq), 1)

    def off_of(c):
        return pl.multiple_of(c * tq, tq)

    def causal_of(c):
        return (c * tq + row) <= (i * tq + lane)

    def fold8(x):
        return x.reshape(tq // 8, 8, tq)

    def indexer(c, carry):
        off = off_of(c)
        ikc = ikb[pl.ds(off, tq), :]
        acc = jnp.zeros((tq, tq), F32)
        for h in range(IDX_H):
            s = _dot(ikc, iqt[h * IDX_D:(h + 1) * IDX_D, :])
            acc = acc + jnp.maximum(s, 0.0) * iwt[h:h + 1, :]
        acc = jnp.where(causal_of(c), acc, NEG_INF)
        keys[pl.ds(off, tq), :] = _sortable(acc)
        return carry

    lax.fori_loop(0, nch, indexer, 0)

    def count_ge(cand):
        def body(c, cnt):
            kc = keys[pl.ds(off_of(c), tq), :]
            return cnt + fold8(jnp.where(kc >= cand, 1, 0)).sum(axis=0)
        cnt = lax.fori_loop(0, nch, body, jnp.zeros((8, tq), I32))
        return cnt.sum(axis=0, keepdims=True)

    theta = _kth_largest(count_ge, (1, tq), topk)

    def make_bias(c, carry):
        off = off_of(c)
        sel = keys[pl.ds(off, tq), :] >= theta
        bias[pl.ds(off, tq), :] = jnp.where(causal_of(c), jnp.where(sel, 0.0, MASKED), MASKED)
        return carry

    lax.fori_loop(0, nch, make_bias, 0)

    m_ref[...] = jnp.full_like(m_ref, NEG_INF)
    l_ref[...] = jnp.zeros_like(l_ref)
    acc_ref[...] = jnp.zeros_like(acc_ref)

    def attend(c, carry):
        off = off_of(c)
        bc = bias[pl.ds(off, tq), :]
        m_new = []
        for h in range(N_Q):
            gsl = slice(h // (N_Q // N_KV) * HEAD_DIM, (h // (N_Q // N_KV) + 1) * HEAD_DIM)
            s = _dot(kb[pl.ds(off, tq), gsl], qt[h * HEAD_DIM:(h + 1) * HEAD_DIM, :]) + bc
            s_ref[h] = s
            m_new.append(jnp.maximum(m_ref[h], fold8(s).max(axis=0).max(axis=0, keepdims=True)))
        for h in range(N_Q):
            gsl = slice(h // (N_Q // N_KV) * HEAD_DIM, (h // (N_Q // N_KV) + 1) * HEAD_DIM)
            p = jnp.exp2(s_ref[h] - m_new[h])
            alpha = jnp.exp2(m_ref[h] - m_new[h])
            l_ref[h] = l_ref[h] * alpha + fold8(p).sum(axis=0).sum(axis=0, keepdims=True)
            acc_ref[h] = acc_ref[h] * alpha + _dot(vt[c, gsl, :], p.astype(BF))
            m_ref[h] = m_new[h]
        return carry

    lax.fori_loop(0, nch, attend, 0)

    for h in range(N_Q):
        o_ref[:, h * HEAD_DIM:(h + 1) * HEAD_DIM] = (acc_ref[h] * (1.0 / l_ref[h])).T.astype(BF)


def _dsa_prompt(qt, iqt, iwt, kb, vt, ikb, tq):
    b, _, t = qt.shape
    topk = min(TOPK_MAX, t // 4)
    kern = functools.partial(_dsa_prompt_kernel, tq=tq, topk=topk)
    return pl.pallas_call(
        kern,
        grid=(b, t // tq),
        in_specs=[
            pl.BlockSpec((None, ATTN_W, tq), lambda bi, i: (bi, 0, i)),
            pl.BlockSpec((None, ATTN_W, tq), lambda bi, i: (bi, 0, i)),
            pl.BlockSpec((None, IDX_H, tq), lambda bi, i: (bi, 0, i)),
            pl.BlockSpec((None, t, KV_W), lambda bi, i: (bi, 0, 0)),
            pl.BlockSpec((None, t // tq, KV_W, tq), lambda bi, i: (bi, 0, 0, 0)),
            pl.BlockSpec((None, t, IDX_D), lambda bi, i: (bi, 0, 0)),
        ],
        out_specs=pl.BlockSpec((None, tq, ATTN_W), lambda bi, i: (bi, i, 0)),
        out_shape=jax.ShapeDtypeStruct((b, t, ATTN_W), BF),
        scratch_shapes=[pltpu.VMEM((t, tq), I32), pltpu.VMEM((t, tq), F32),
                        pltpu.VMEM((N_Q, 1, tq), F32), pltpu.VMEM((N_Q, 1, tq), F32),
                        pltpu.VMEM((N_Q, HEAD_DIM, tq), F32), pltpu.VMEM((N_Q, tq, tq), F32)],
        compiler_params=_cp("parallel", "arbitrary"),
        name="dsa_prompt",
    )(qt, iqt, iwt, kb, vt, ikb)


def _l2norm(x):
    return x * lax.rsqrt(jnp.sum(x * x, -1, keepdims=True) + NORM_EPS)


def _gdn_activations(y, small, alog, dtb):
    y = y * _sigmoid(y)
    qs, ks, gs, bs = [], [], [], []
    r = y.shape[0]
    for h in range(GDN_H):
        sl = slice(h * GDN_D, (h + 1) * GDN_D)
        qs.append(_l2norm(y[:, sl]) * (GDN_D ** -0.5))
        ks.append(_l2norm(y[:, GDN_W + h * GDN_D:GDN_W + (h + 1) * GDN_D]))
        b_raw = jnp.broadcast_to(small[:, SM_GB + h:SM_GB + h + 1], (r, GDN_D))
        a_raw = jnp.broadcast_to(small[:, SM_GA + h:SM_GA + h + 1], (r, GDN_D))
        bs.append(_sigmoid(b_raw))
        gs.append(-jnp.exp(alog[:, sl]) * _softplus(a_raw + dtb[:, sl]))
    return qs, ks, y[:, 2 * GDN_W:], gs, bs


def _mix_prep_kernel(gq, gq_prev, sm, sb, sc, sx, sc_prev, sx_prev, gw, sw, alog, dtb,
                     qn_o, kn_o, vv_o, g_o, beta_o, scm_o, utail_o, xbuf, ubuf, *, tr):
    r = pl.program_id(1)
    first = r == 0
    xbuf[0:8, :] = jnp.where(first, 0.0, gq_prev[...])
    xbuf[8:, :] = gq[...]
    y = xbuf[5:5 + tr, :] * gw[0:1, :]
    for j in range(1, GDN_CONV):
        y = y + xbuf[5 + j:5 + j + tr, :] * gw[j:j + 1, :]
    qs, ks, v, gs, bs = _gdn_activations(y, sm[...], alog[...], dtb[...])
    for h in range(GDN_H):
        sl = slice(h * GDN_D, (h + 1) * GDN_D)
        qn_o[:, sl] = qs[h]
        kn_o[:, sl] = ks[h]
        g_o[:, sl] = gs[h]
        beta_o[:, sl] = bs[h]
    vv_o[...] = v

    u = sc[...] * sx[...]
    ubuf[0:8, :] = jnp.where(first, 0.0, sc_prev[...] * sx_prev[...])
    ubuf[8:, :] = u
    y = ubuf[6:6 + tr, :] * sw[0:1, :]
    for j in range(1, SC_CONV):
        y = y + ubuf[6 + j:6 + j + tr, :] * sw[j:j + 1, :]
    scm_o[...] = (sb[...] * y).astype(BF)

    @pl.when(r == pl.num_programs(1) - 1)
    def _():
        utail_o[...] = u[tr - 8:, :]


def _mix_prep(h3, gw, sw, alog, dtb, tr):
    b, t, _ = h3.shape
    nr = t // tr
    col = lambda w, off: pl.BlockSpec((None, tr, w), lambda bi, r, o=off // w: (bi, r, o))
    prev = lambda w, off: pl.BlockSpec(
        (None, 8, w), lambda bi, r, o=off // w: (bi, jnp.maximum(r * (tr // 8) - 1, 0), o))
    full = lambda a: pl.BlockSpec(a.shape, lambda bi, r: (0, 0))
    out = pl.BlockSpec((None, tr, GDN_W), lambda bi, r: (bi, r, 0))
    act = jax.ShapeDtypeStruct((b, t, GDN_W), F32)
    return pl.pallas_call(
        functools.partial(_mix_prep_kernel, tr=tr),
        grid=(b, nr),
        in_specs=[col(GDN_CONV_DIM, C_GQKV), prev(GDN_CONV_DIM, C_GQKV), col(128, C_SM),
                  col(SC_W, C_SB), col(SC_W, C_SC), col(SC_W, C_SX), prev(SC_W, C_SC), prev(SC_W, C_SX),
                  full(gw), full(sw), full(alog), full(dtb)],
        out_specs=[out, out, out, out, out, out,
                   pl.BlockSpec((None, 8, SC_W), lambda bi, r: (bi, 0, 0))],
        out_shape=[act, act, act, act, act,
                   jax.ShapeDtypeStruct((b, t, SC_W), BF),
                   jax.ShapeDtypeStruct((b, 8, SC_W), F32)],
        scratch_shapes=[pltpu.VMEM((tr + 8, GDN_CONV_DIM), F32), pltpu.VMEM((tr + 8, SC_W), F32)],
        compiler_params=_cp("parallel", "arbitrary"),
        name="mix_prep",
    )(h3, h3, h3, h3, h3, h3, h3, h3, gw, sw, alog, dtb)


def _cumsum_rows(x):
    row = lax.broadcasted_iota(I32, x.shape, 0)
    s = 1
    while s < x.shape[0]:
        x = x + jnp.where(row >= s, pltpu.roll(x, s, 0), 0.0)
        s *= 2
    return x


def _split_bf(x):
    hi = x.astype(BF)
    return hi, (x - hi.astype(F32)).astype(BF)


def _dot3(a, b):
    m = a.shape[0]
    ah, al = _split_bf(a)
    bh, bl = _split_bf(b)
    r = _dot(jnp.concatenate([ah, al], axis=0), bh)
    return r[:m] + r[m:] + _dot(ah, bl)


def _gated_rmsnorm(o, z, norm_g):
    on = o * lax.rsqrt(jnp.mean(o * o, -1, keepdims=True) + NORM_EPS) * norm_g
    return on * (z * _sigmoid(z))


def _gdn_chunk_kernel(qn, kn, vv, gb, betab, z, norm_g, o_ref, s_out, s_ref, *, nb):
    c = pl.program_id(0)
    C = GDN_CHUNK
    W = GDN_H * C

    @pl.when(c == 0)
    def _():
        s_ref[...] = jnp.zeros_like(s_ref)

    ii = lax.broadcasted_iota(I32, (C, W), 0)
    jj = lax.broadcasted_iota(I32, (C, W), 1) % C
    lower = ii >= jj
    strict = ii > jj
    eye = jnp.where(ii == jj, 1.0, 0.0).astype(F32)
    blockmask = (lax.broadcasted_iota(I32, (W, W), 0) // C) == (lax.broadcasted_iota(I32, (W, W), 1) // C)
    zpad = jnp.zeros((GDN_D - C, GDN_D), F32)
    ng = norm_g[...]
    heads = [slice(h * GDN_D, (h + 1) * GDN_D) for h in range(GDN_H)]
    lanes = lambda parts: jnp.concatenate(parts, axis=1)

    def block_diag(m):
        return jnp.where(blockmask, jnp.concatenate([m] * GDN_H, axis=0), 0.0)

    st = []
    for b in range(nb):
        q = [qn[b, :, sl] for sl in heads]
        k = [kn[b, :, sl] for sl in heads]
        beta = [betab[b, :, sl] for sl in heads]
        gc = [_cumsum_rows(gb[b, :, sl]) for sl in heads]
        kbeta = [k[h] * beta[h] for h in range(GDN_H)]
        gcol = lanes([g[:, 0:C] for g in gc])
        grow = lanes([jnp.concatenate([g, g], axis=0).T[0:C, 0:C] for g in gc])
        decay = jnp.where(lower, jnp.exp(jnp.where(lower, gcol - grow, 0.0)), 0.0)
        kk, qk = [], []
        for h in range(GDN_H):
            r = _dot_nt(jnp.concatenate([kbeta[h], q[h]], axis=0).astype(BF), k[h].astype(BF))
            kk.append(r[:C])
            qk.append(r[C:])
        a = jnp.where(strict, lanes(kk) * decay, 0.0)
        intra = lanes(qk) * decay
        rhs = jnp.concatenate(
            [lanes([vv[b, :, heads[h]] * beta[h], kbeta[h] * jnp.exp(gc[h])]) for h in range(GDN_H)], axis=0)
        st.append(dict(q=q, k=k, gc=gc, intra=intra, rhs=rhs, nm=-a))

    for d in st:
        d["x"] = eye + d["nm"]
        d["p"] = _dot3(d["nm"], block_diag(d["nm"]))
    step = 4
    while step < C:
        for d in st:
            r = _dot3(jnp.concatenate([d["x"], d["p"]], axis=0), block_diag(d["p"]))
            d["x"] = d["x"] + r[:C]
            d["p"] = r[C:]
        step *= 2
    for d in st:
        tinv = d["x"] + _dot3(d["x"], block_diag(d["p"]))
        d["uw"] = _dot3(block_diag(tinv), d["rhs"])

    for b in range(nb):
        d = st[b]
        for h in range(GDN_H):
            u = d["uw"][h * C:(h + 1) * C, 0:GDN_D]
            w = d["uw"][h * C:(h + 1) * C, GDN_D:]
            gc = d["gc"][h]
            s = s_ref[b, h]
            r = _dot(jnp.concatenate([w, d["q"][h] * jnp.exp(gc)], axis=0).astype(BF), s.astype(BF))
            v_new = u - r[:C]
            o = r[C:] + _dot(d["intra"][:, h * C:(h + 1) * C].astype(BF), v_new.astype(BF))
            g_last = gc[C - 1:C, :]
            kd = d["k"][h] * jnp.exp(g_last - gc)
            kdt = jnp.concatenate([kd, zpad], axis=0).T.astype(BF)
            vnp = jnp.concatenate([v_new, zpad], axis=0).astype(BF)
            s_ref[b, h] = s * jnp.exp(g_last) + _dot(kdt, vnp)
            o_ref[b, :, heads[h]] = _gated_rmsnorm(o, z[b, :, heads[h]], ng).astype(BF)

    @pl.when(c == pl.num_programs(0) - 1)
    def _():
        s_out[...] = s_ref[...]


def _gdn_chunked(qn, kn, vv, gb, betab, h3, norm_g):
    b, t, _ = qn.shape
    C = GDN_CHUNK
    blk = pl.BlockSpec((b, C, GDN_W), lambda c: (0, c, 0))
    return pl.pallas_call(
        functools.partial(_gdn_chunk_kernel, nb=b),
        grid=(t // C,),
        in_specs=[blk, blk, blk, blk, blk,
                  pl.BlockSpec((b, C, GDN_W), lambda c: (0, c, C_GZ // GDN_W)),
                  pl.BlockSpec((1, GDN_D), lambda c: (0, 0))],
        out_specs=[pl.BlockSpec((b, C, GDN_W), lambda c: (0, c, 0)),
                   pl.BlockSpec((b, GDN_H, GDN_D, GDN_D), lambda c: (0, 0, 0, 0))],
        out_shape=[jax.ShapeDtypeStruct((b, t, GDN_W), BF),
                   jax.ShapeDtypeStruct((b, GDN_H, GDN_D, GDN_D), F32)],
        scratch_shapes=[pltpu.VMEM((b, GDN_H, GDN_D, GDN_D), F32)],
        compiler_params=_cp("arbitrary"),
        name="gdn_chunked",
    )(qn, kn, vv, gb, betab, h3, norm_g)


def _layer_norm(y, g, b):
    mu = jnp.mean(y, -1, keepdims=True)
    yc = y - mu
    var = jnp.mean(yc * yc, -1, keepdims=True)
    return yc * lax.rsqrt(var + LN_EPS) * g + b


def _out_proj_kernel(attn, gdn, scm, x, wo, g, b, o_ref):
    y = _dot(attn[...], wo[0:ATTN_W, :])
    y = y + _dot(gdn[...], wo[ATTN_W:ATTN_W + GDN_W, :])
    y = y + _dot(scm[...], wo[ATTN_W + GDN_W:, :])
    o_ref[...] = _layer_norm(DN_ALPHA * x[...] + y, g[...], b[...])


def _out_proj(attn, gdn, scm, x, wo, layer, g, b, tm):
    m = x.shape[0]
    row = lambda w: pl.BlockSpec((tm, w), lambda i: (i, 0))
    full = lambda a: pl.BlockSpec((None,) + a.shape[1:], lambda i: (layer, 0, 0))
    return pl.pallas_call(
        _out_proj_kernel,
        grid=(m // tm,),
        in_specs=[row(ATTN_W), row(GDN_W), row(SC_W), row(D_MODEL), full(wo), full(g), full(b)],
        out_specs=row(D_MODEL),
        out_shape=jax.ShapeDtypeStruct((m, D_MODEL), F32),
        compiler_params=_cp("parallel"),
        name="out_proj_ln",
    )(attn, gdn, scm, x, wo, g, b)


def _ffn_kernel(h, wg, wu, wd, g, b, o_ref, hb, acc):
    f = pl.program_id(1)

    @pl.when(f == 0)
    def _():
        hb[...] = h[...].astype(BF)
        acc[...] = jnp.zeros_like(acc)

    x = hb[...]
    gate = _dot(x, wg[...])
    hid = gate * _sigmoid(gate) * _dot(x, wu[...])
    acc[...] += _dot(hid.astype(BF), wd[...])

    @pl.when(f == pl.num_programs(1) - 1)
    def _():
        o_ref[...] = _layer_norm(DN_ALPHA * h[...] + acc[...], g[...], b[...])


def _ffn(h, wg, wu, wd, layer, g, b, tm, tf):
    m = h.shape[0]
    full = lambda a: pl.BlockSpec((None,) + a.shape[1:], lambda i, f: (layer, 0, 0))
    return pl.pallas_call(
        _ffn_kernel,
        grid=(m // tm, FFN_DIM // tf),
        in_specs=[pl.BlockSpec((tm, D_MODEL), lambda i, f: (i, 0)),
                  pl.BlockSpec((None, D_MODEL, tf), lambda i, f: (layer, 0, f)),
                  pl.BlockSpec((None, D_MODEL, tf), lambda i, f: (layer, 0, f)),
                  pl.BlockSpec((None, tf, D_MODEL), lambda i, f: (layer, f, 0)),
                  full(g), full(b)],
        out_specs=pl.BlockSpec((tm, D_MODEL), lambda i, f: (i, 0)),
        out_shape=jax.ShapeDtypeStruct((m, D_MODEL), F32),
        scratch_shapes=[pltpu.VMEM((tm, D_MODEL), BF), pltpu.VMEM((tm, D_MODEL), F32)],
        compiler_params=_cp("parallel", "arbitrary"),
        name="ffn_ln",
    )(h, wg, wu, wd, g, b)


def _head_rows(v):
    return jnp.repeat(v.astype(F32), GDN_D)[None, :]


def _prompt_layer(xp, tabs, lw, tm_in=1024, tn_in=768, tr=256, tm_out=512, tm_ffn=512, tf=512):
    b, t, _ = xp.shape
    x2 = xp.reshape(b * t, D_MODEL)
    layer = lw["layer"]
    h = _matmul(x2, lw["w_in"], layer, min(tm_in, b * t), tn_in)
    h3 = h.reshape(b, t, IN_PACKED)
    kr, kb, vt, qt, iqt, ikr, ikb, iwt = _attn_prep(h3, tabs, tr)
    attn = _dsa_prompt(qt, iqt, iwt, kb, vt, ikb, tr)
    qn, kn, vv, gb, betab, scm, utail = _mix_prep(h3, lw["gdn_conv_w"], lw["sc_conv_w"],
                                                 lw["alog"], lw["dtb"], tr)
    gdn, s_new = _gdn_chunked(qn, kn, vv, gb, betab, h3, lw["norm_g"])
    h1 = _out_proj(attn.reshape(b * t, ATTN_W), gdn.reshape(b * t, GDN_W), scm.reshape(b * t, SC_W),
                   x2, lw["w_out"], layer, lw["ln1_g"], lw["ln1_b"], min(tm_out, b * t))
    y = _ffn(h1, lw["w_gate"], lw["w_up"], lw["w_down"], layer, lw["ln2_g"], lw["ln2_b"],
             min(tm_ffn, b * t), tf)
    outs = dict(
        k=kr.reshape(b, t, N_KV, HEAD_DIM),
        v=h3[:, :, C_AV:C_AV + KV_W].reshape(b, t, N_KV, HEAD_DIM),
        kidx=ikr,
        s=s_new,
        gconv=h3[:, t - (GDN_CONV - 1):, C_GQKV:C_GQKV + GDN_CONV_DIM],
        sconv=utail[:, 8 - (SC_CONV - 1):, :],
        attn=attn, gdn=gdn, scm=scm,
    )
    return y.reshape(b, t, D_MODEL), outs


def _sample_prep_kernel(h, c128, s128, c64, s64, gst, sst, s_in, gw, sw, alog, dtb, norm_g,
                        q_o, k_o, iq_o, ik_o, iw_o, gst_o, sst_o, s_o, gdn_o, scm_o, o_buf, *, nb):
    cos = c128[...]
    sin = s128[...]
    for hh in range(N_Q):
        sl = slice(hh * HEAD_DIM, (hh + 1) * HEAD_DIM)
        q_o[:, sl] = _rope128(h[:, C_AQ + hh * HEAD_DIM:C_AQ + (hh + 1) * HEAD_DIM], cos, sin) * (HEAD_DIM ** -0.5)
    for hh in range(N_KV):
        sl = slice(hh * HEAD_DIM, (hh + 1) * HEAD_DIM)
        k_o[:, sl] = _rope128(h[:, C_AK + hh * HEAD_DIM:C_AK + (hh + 1) * HEAD_DIM], cos, sin)
    cos = c64[...]
    sin = s64[...]
    for j in range(IDX_H * IDX_D // 128):
        sl = slice(j * 128, (j + 1) * 128)
        iq_o[:, sl] = _rope64x2(h[:, C_IQ + j * 128:C_IQ + (j + 1) * 128], cos, sin) * (IDX_D ** -0.5)
    small = h[:, C_SM:C_SM + 128]
    ik_o[...] = _rope64x2(small, cos, sin)[:, SM_IK:SM_IK + IDX_D]
    iw_o[...] = small[:, SM_IW:SM_IW + IDX_H] * (IDX_H ** -0.5)

    gq = h[:, C_GQKV:C_GQKV + GDN_CONV_DIM]
    y = gq * gw[GDN_CONV - 1:GDN_CONV, :]
    for j in range(GDN_CONV - 1):
        y = y + gst[j] * gw[j:j + 1, :]
    for j in range(GDN_CONV - 2):
        gst_o[j] = gst[j + 1]
    gst_o[GDN_CONV - 2] = gq
    qs, ks, v, gs, bs = _gdn_activations(y, small, alog[...], dtb[...])
    for b in range(nb):
        for hh in range(GDN_H):
            sl = slice(hh * GDN_D, (hh + 1) * GDN_D)
            kc = jnp.broadcast_to(ks[hh][b:b + 1, :], (GDN_D, GDN_D)).T
            qc = jnp.broadcast_to(qs[hh][b:b + 1, :], (GDN_D, GDN_D)).T
            s = s_in[b, hh] * jnp.exp(gs[hh][b:b + 1, :])
            ks_row = jnp.sum(kc * s, axis=0, keepdims=True)
            delta = (v[b:b + 1, sl] - ks_row) * bs[hh][b:b + 1, :]
            s = s + kc * delta
            s_o[b, hh] = s
            o_buf[b:b + 1, sl] = jnp.sum(qc * s, axis=0, keepdims=True)
    ng = norm_g[...]
    for hh in range(GDN_H):
        sl = slice(hh * GDN_D, (hh + 1) * GDN_D)
        z = h[:, C_GZ + hh * GDN_D:C_GZ + (hh + 1) * GDN_D]
        gdn_o[:, sl] = _gated_rmsnorm(o_buf[:, sl], z, ng).astype(BF)

    u = h[:, C_SC:C_SC + SC_W] * h[:, C_SX:C_SX + SC_W]
    y = u * sw[SC_CONV - 1:SC_CONV, :]
    for j in range(SC_CONV - 1):
        y = y + sst[j] * sw[j:j + 1, :]
    for j in range(SC_CONV - 2):
        sst_o[j] = sst[j + 1]
    sst_o[SC_CONV - 2] = u
    scm_o[...] = (h[:, C_SB:C_SB + SC_W] * y).astype(BF)


def _sample_prep(h, tabs, gst, sst, s_in, gw, sw, alog, dtb, norm_g):
    nb = h.shape[0]
    sds = jax.ShapeDtypeStruct
    return pl.pallas_call(
        functools.partial(_sample_prep_kernel, nb=nb),
        out_shape=[sds((nb, ATTN_W), F32), sds((nb, KV_W), F32), sds((nb, IDX_H * IDX_D), F32),
                   sds((nb, IDX_D), F32), sds((nb, IDX_H), F32),
                   sds(gst.shape, F32), sds(sst.shape, F32), sds(s_in.shape, F32),
                   sds((nb, GDN_W), BF), sds((nb, SC_W), BF)],
        scratch_shapes=[pltpu.VMEM((nb, GDN_W), F32)],
        compiler_params=pltpu.CompilerParams(vmem_limit_bytes=VMEM_LIMIT),
        name="sample_prep",
    )(h, *tabs, gst, sst, s_in, gw, sw, alog, dtb, norm_g)


def _sample_index_kernel(pt, *refs, pps, topk):
    pages = refs[:pps]
    iq2, iw, ikn, expand, bias_o, seln_o, keys = refs[pps:]
    s = pl.program_id(1)
    half = PAGE // 2
    iqb = iq2[...]
    w = jnp.broadcast_to(iw[...], (IDX_H, half))
    for j in range(pps):
        sc = jnp.maximum(_dot_nt(iqb, pages[j][...].astype(BF)), 0.0)
        even = jnp.sum(sc[0:IDX_H] * w, axis=0, keepdims=True)
        odd = jnp.sum(sc[IDX_H:] * w, axis=0, keepdims=True)
        keys[pl.ds(s * pps + j, 1), :] = _sortable(jnp.concatenate([even, odd], axis=1))

    @pl.when(s == pl.num_programs(1) - 1)
    def _():
        prod = iqb[0:IDX_H, 0:IDX_D].astype(F32) * ikn[...].astype(BF).astype(F32)
        sn = jnp.sum(prod, axis=1, keepdims=True)
        sn = jnp.sum(jnp.maximum(sn, 0.0) * iw[...], axis=0, keepdims=True)
        key_new = jnp.broadcast_to(_sortable(sn), (1, PAGE))
        kall = keys[...]

        def count_ge(cand):
            c = jnp.where(kall >= cand, 1.0, 0.0).sum(axis=0, keepdims=True).sum(axis=1, keepdims=True)
            return jnp.broadcast_to(c, (1, PAGE)) + jnp.where(key_new >= cand, 1.0, 0.0)

        theta = _kth_largest(count_ge, (1, PAGE), float(topk))
        sel = _dot(jnp.where(kall >= theta, 1.0, 0.0).astype(BF), expand[...])
        bias_o[...] = jnp.where(sel > 0.5, 0.0, MASKED)
        seln_o[...] = jnp.where(key_new >= theta, 0.0, MASKED)


def _expand_matrix():
    slot = np.arange(PAGE)
    key = np.where(slot < PAGE // 2, 2 * slot, 2 * (slot - PAGE // 2) + 1)
    return jnp.asarray(key[:, None] == (np.arange(PAGE * N_KV)[None, :] // N_KV), BF)


def _sample_index(page_table, kidx2, layer, iq2, iw3, ikn3, pps=16):
    nb, npages = page_table.shape
    topk = min(TOPK_MAX, (npages * PAGE + 1) // 4)
    page = lambda j: pl.BlockSpec((None, None, PAGE // 2, 2 * IDX_D),
                                  lambda b, s, pt, j=j: (layer, pt[b, s * pps + j], 0, 0))
    per_b = lambda shp: pl.BlockSpec((None,) + shp, lambda b, s, pt: (b, 0, 0))
    grid_spec = pltpu.PrefetchScalarGridSpec(
        num_scalar_prefetch=1,
        grid=(nb, npages // pps),
        in_specs=[page(j) for j in range(pps)]
                 + [per_b((2 * IDX_H, 2 * IDX_D)), per_b((IDX_H, 1)), per_b((1, IDX_D)),
                    pl.BlockSpec((PAGE, PAGE * N_KV), lambda b, s, pt: (0, 0))],
        out_specs=[per_b((npages, PAGE * N_KV)), per_b((1, PAGE))],
        scratch_shapes=[pltpu.VMEM((npages, PAGE), I32)],
    )
    return pl.pallas_call(
        functools.partial(_sample_index_kernel, pps=pps, topk=topk),
        grid_spec=grid_spec,
        out_shape=[jax.ShapeDtypeStruct((nb, npages, PAGE * N_KV), F32), jax.ShapeDtypeStruct((nb, 1, PAGE), F32)],
        compiler_params=_cp("parallel", "arbitrary"),
        name="sample_index",
    )(page_table, *([kidx2] * pps), iq2, iw3, ikn3, _expand_matrix())


def _sample_attn_kernel(pt, *refs, pps):
    kp = refs[:pps]
    vp = refs[pps:2 * pps]
    q8, bias, seln, kn8, vn8, o_ref, m_ref, l_ref, acc_ref = refs[2 * pps:]
    s = pl.program_id(1)
    rows = PAGE * N_KV

    @pl.when(s == 0)
    def _():
        m_ref[...] = jnp.full_like(m_ref, NEG_INF)
        l_ref[...] = jnp.zeros_like(l_ref)
        acc_ref[...] = jnp.zeros_like(acc_ref)

    q = q8[...]
    lane_head = lax.broadcasted_iota(I32, (N_Q, rows), 1) % N_KV
    own = lane_head == lax.broadcasted_iota(I32, (N_Q, rows), 0) // (N_Q // N_KV)
    head_mask = jnp.where(own, 0.0, MASKED)
    lg = [_dot_nt(q, kp[j][...].astype(BF)) + (bias[pl.ds(s * pps + j, 1), :] + head_mask) for j in range(pps)]
    m_old = m_ref[...]
    m_new = m_old
    for x in lg:
        m_new = jnp.maximum(m_new, jnp.max(x, axis=1, keepdims=True))
    alpha = jnp.exp(m_old - m_new)
    l_new = l_ref[...] * alpha
    acc = acc_ref[...] * alpha
    for j in range(pps):
        p = jnp.exp(lg[j] - m_new)
        l_new = l_new + jnp.sum(p, axis=1, keepdims=True)
        acc = acc + _dot(p.astype(BF), vp[j][...].astype(BF))
    m_ref[...] = m_new
    l_ref[...] = l_new
    acc_ref[...] = acc

    @pl.when(s == pl.num_programs(1) - 1)
    def _():
        knb = kn8[...].astype(BF).astype(F32)
        vnb = vn8[...].astype(BF).astype(F32)
        x = jnp.sum(q.astype(F32) * knb, axis=1, keepdims=True) + seln[:, 0:1]
        m_fin = jnp.maximum(m_new, x)
        a = jnp.exp(m_new - m_fin)
        p = jnp.exp(x - m_fin)
        l_fin = l_new * a + p
        o_ref[...] = ((acc * a + p.astype(BF).astype(F32) * vnb) / l_fin).astype(BF)


def _sample_attn(page_table, ck2, cv2, layer, q8, bias, seln, kn8, vn8, pps=8):
    nb, npages = page_table.shape
    rows = PAGE * N_KV
    page = lambda j: pl.BlockSpec((None, None, rows, HEAD_DIM),
                                  lambda b, s, pt, j=j: (layer, pt[b, s * pps + j], 0, 0))
    per_b = lambda shp: pl.BlockSpec((None,) + shp, lambda b, s, pt: (b, 0, 0))
    grid_spec = pltpu.PrefetchScalarGridSpec(
        num_scalar_prefetch=1,
        grid=(nb, npages // pps),
        in_specs=[page(j) for j in range(pps)] * 2
                 + [per_b((N_Q, HEAD_DIM)), per_b((npages, rows)), per_b((1, PAGE)),
                    per_b((N_Q, HEAD_DIM)), per_b((N_Q, HEAD_DIM))],
        out_specs=per_b((N_Q, HEAD_DIM)),
        scratch_shapes=[pltpu.VMEM((N_Q, 1), F32), pltpu.VMEM((N_Q, 1), F32), pltpu.VMEM((N_Q, HEAD_DIM), F32)],
    )
    return pl.pallas_call(
        functools.partial(_sample_attn_kernel, pps=pps),
        grid_spec=grid_spec,
        out_shape=jax.ShapeDtypeStruct((nb, N_Q, HEAD_DIM), BF),
        compiler_params=_cp("parallel", "arbitrary"),
        name="sample_attn",
    )(page_table, *([ck2] * pps), *([cv2] * pps), q8, bias, seln, kn8, vn8)


SAMPLE_ROWS = 16


def _sample_layer(xs, tabs, lw, layer, page_table, cache_k, cache_v, cache_kidx, gst, sst, s_in):
    nb = page_table.shape[0]
    h = _matmul(xs, lw["w_in"], layer, SAMPLE_ROWS, 768)[:nb]
    q, k, iq, ik, iw, gst_n, sst_n, s_n, gdn, scm = _sample_prep(
        h, tabs, gst, sst, s_in, lw["gdn_conv_w"], lw["sc_conv_w"], lw["alog"], lw["dtb"], lw["norm_g"])
    v = h[:, C_AV:C_AV + KV_W]
    iq3 = iq.reshape(nb, IDX_H, IDX_D).astype(BF)
    zero = jnp.zeros_like(iq3)
    iq2 = jnp.concatenate([jnp.concatenate([iq3, zero], -1), jnp.concatenate([zero, iq3], -1)], 1)
    bias, seln = _sample_index(page_table, cache_kidx, layer, iq2, iw.reshape(nb, IDX_H, 1), ik.reshape(nb, 1, IDX_D))
    own_kv = lambda a: jnp.repeat(a.reshape(nb, N_KV, HEAD_DIM), N_Q // N_KV, axis=1)
    attn = _sample_attn(page_table, cache_k, cache_v, layer, q.reshape(nb, N_Q, HEAD_DIM).astype(BF),
                        bias, seln, own_kv(k), own_kv(v)).reshape(nb, ATTN_W)
    padr = lambda a: jnp.pad(a, ((0, SAMPLE_ROWS - nb), (0, 0)))
    h1 = _out_proj(padr(attn), padr(gdn), padr(scm), xs, lw["w_out"], layer, lw["ln1_g"], lw["ln1_b"],
                   SAMPLE_ROWS)
    y = _ffn(h1, lw["w_gate"], lw["w_up"], lw["w_down"], layer, lw["ln2_g"], lw["ln2_b"], SAMPLE_ROWS, 512)
    outs = dict(k=k.reshape(nb, 1, N_KV, HEAD_DIM), v=v.reshape(nb, 1, N_KV, HEAD_DIM), kidx=ik.reshape(nb, 1, IDX_D),
                s=s_n, gconv=jnp.swapaxes(gst_n, 0, 1), sconv=jnp.swapaxes(sst_n, 0, 1),
                attn=attn, gdn=gdn, scm=scm)
    return y, outs


def kernel(x_prompt, x_sample, cache_k, cache_v, cache_kidx, page_table, state_gdn, state_gdn_conv,
           state_sc_conv, w_in, gdn_conv_w, gdn_a_log, gdn_dt_bias, gdn_norm_g, sc_conv_w, w_out,
           ln1_g, ln1_b, w_gate, w_up, w_down, ln2_g, ln2_b):
    bp, tp, _ = x_prompt.shape
    nb, ts, _ = x_sample.shape
    assert ts == 1
    past = page_table.shape[1] * PAGE
    tabs_p = _rope_tables(jnp.arange(tp))
    tabs_s = _rope_tables(past + jnp.arange(ts))
    w_in_p = _pack_w_in(w_in)
    w_out_b, w_gate_b, w_up_b, w_down_b = (w.astype(BF) for w in (w_out, w_gate, w_up, w_down))
    depth, pool = cache_k.shape[:2]
    cache_k = cache_k.reshape(depth, pool, PAGE * N_KV, HEAD_DIM)
    cache_v = cache_v.reshape(depth, pool, PAGE * N_KV, HEAD_DIM)
    cache_kidx = cache_kidx.reshape(depth, pool, PAGE // 2, 2 * IDX_D)

    xp = x_prompt
    xs = jnp.pad(x_sample.reshape(nb, D_MODEL), ((0, SAMPLE_ROWS - nb), (0, 0)))
    po, so = [], []
    for l in range(DEPTH):
        lw = dict(layer=l, w_in=w_in_p, gdn_conv_w=gdn_conv_w[l], sc_conv_w=sc_conv_w[l],
                  alog=_head_rows(gdn_a_log[l]), dtb=_head_rows(gdn_dt_bias[l]), norm_g=gdn_norm_g[l][None, :],
                  w_out=w_out_b, ln1_g=ln1_g[:, None, :], ln1_b=ln1_b[:, None, :],
                  w_gate=w_gate_b, w_up=w_up_b, w_down=w_down_b,
                  ln2_g=ln2_g[:, None, :], ln2_b=ln2_b[:, None, :])
        xp, o = _prompt_layer(xp, tabs_p, lw)
        po.append(o)
        xs, o = _sample_layer(xs, tabs_s, lw, l, page_table, cache_k, cache_v, cache_kidx,
                              jnp.swapaxes(state_gdn_conv[l], 0, 1), jnp.swapaxes(state_sc_conv[l], 0, 1),
                              state_gdn[l])
        so.append(o)

    st = lambda outs, n: jnp.stack([o[n] for o in outs])
    return (xp, xs[:nb].reshape(nb, ts, D_MODEL),
            st(po, "k"), st(po, "v"), st(po, "kidx"), st(so, "k"), st(so, "v"), st(so, "kidx"),
            st(po, "s"), st(so, "s"), st(po, "gconv"), st(so, "gconv"), st(po, "sconv"), st(so, "sconv"))
```

```python
import functools
import math

import numpy as np
import jax
import jax.numpy as jnp
from jax import lax
from jax.experimental import pallas as pl
from jax.experimental.pallas import tpu as pltpu

F32 = jnp.float32
BF = jnp.bfloat16
I32 = jnp.int32

D_MODEL = 2048
DEPTH = 4
PAGE = 128
HEAD_DIM = 128
N_Q = 8
N_KV = 4
ATTN_W = N_Q * HEAD_DIM
KV_W = N_KV * HEAD_DIM
IDX_H = 16
IDX_D = 64
TOPK_MAX = 256
ROPE_THETA = 10000.0
GDN_H = 4
GDN_D = 128
GDN_W = GDN_H * GDN_D
GDN_CONV = 4
GDN_CONV_DIM = 3 * GDN_W
GDN_CHUNK = 64
SC_W = 512
SC_CONV = 3
FFN_DIM = 5632
DN_ALPHA = (2 * DEPTH) ** 0.25
LN_EPS = 1e-5
NORM_EPS = 1e-6
IN_SIZES = (ATTN_W, KV_W, KV_W, IDX_H * IDX_D, IDX_D, IDX_H, GDN_CONV_DIM, GDN_W, GDN_H, GDN_H,
            SC_W, SC_W, SC_W)

C_AQ, C_AK, C_AV, C_IQ, C_GQKV, C_GZ, C_SB, C_SC, C_SX, C_SM = (
    0, 1024, 1536, 2048, 3072, 4608, 5120, 5632, 6144, 6656)
SM_IK, SM_IW, SM_GB, SM_GA = 0, 64, 80, 84
IN_PACKED = 6912

NEG_INF = float("-inf")
MASKED = -1e30
Q_SCALE_LOG2 = HEAD_DIM ** -0.5 * math.log2(math.e)
INT_MIN = -2 ** 31

VMEM_LIMIT = 56 * 1024 * 1024


def _cp(*sem):
    return pltpu.CompilerParams(dimension_semantics=tuple(sem), vmem_limit_bytes=VMEM_LIMIT)


def _sigmoid(x):
    return 1.0 / (1.0 + jnp.exp(-x))


def _softplus(x):
    return jnp.maximum(x, 0.0) + jnp.log1p(jnp.exp(-jnp.abs(x)))


def _dot(a, b):
    return jnp.dot(a, b, preferred_element_type=F32)


def _dot_hi(a, b):
    return jnp.dot(a, b, preferred_element_type=F32, precision=lax.Precision.HIGHEST)


def _dot_nt(a, b):
    return lax.dot_general(a, b, (((1,), (1,)), ((), ())), preferred_element_type=F32)


_IN_OFF = np.concatenate([[0], np.cumsum(IN_SIZES)]).tolist()
_O_IK, _O_GQKV, _O_GB, _O_SB, _O_END = _IN_OFF[4], _IN_OFF[6], _IN_OFF[8], _IN_OFF[10], _IN_OFF[13]


def _pack_kernel(x, o):
    rows = x.shape[0]
    o[:, 0:C_GQKV] = x[:, 0:_O_IK].astype(BF)
    o[:, C_GQKV:C_SB] = x[:, _O_GQKV:_O_GB].astype(BF)
    o[:, C_SB:C_SM] = x[:, _O_SB:_O_END].astype(BF)
    used = (_O_GQKV - _O_IK) + (_O_SB - _O_GB)
    small = jnp.concatenate([x[:, _O_IK:_O_GQKV], x[:, _O_GB:_O_SB], jnp.zeros((rows, 128 - used), F32)], axis=1)
    o[:, C_SM:C_SM + 128] = small.astype(BF)
    o[:, C_SM + 128:] = jnp.zeros((rows, IN_PACKED - C_SM - 128), BF)


def _pack_w_in(w_in, tr=256):
    depth, d, n = w_in.shape
    return pl.pallas_call(
        _pack_kernel,
        grid=(depth, d // tr),
        in_specs=[pl.BlockSpec((None, tr, n), lambda l, r: (l, r, 0))],
        out_specs=pl.BlockSpec((None, tr, IN_PACKED), lambda l, r: (l, r, 0)),
        out_shape=jax.ShapeDtypeStruct((depth, d, IN_PACKED), BF),
        compiler_params=_cp("parallel", "parallel"),
        name="pack_w_in",
    )(w_in)


def _mm_kernel(x_ref, w_ref, o_ref):
    o_ref[...] = _dot(x_ref[...].astype(BF), w_ref[...])


def _matmul(x, w, layer, tm, tn):
    m, k = x.shape
    n = w.shape[2]
    return pl.pallas_call(
        _mm_kernel,
        grid=(m // tm, n // tn),
        in_specs=[pl.BlockSpec((tm, k), lambda i, j: (i, 0)),
                  pl.BlockSpec((None, k, tn), lambda i, j: (layer, 0, j))],
        out_specs=pl.BlockSpec((tm, tn), lambda i, j: (i, j)),
        out_shape=jax.ShapeDtypeStruct((m, n), F32),
        compiler_params=_cp("parallel", "arbitrary"),
        name="proj_in",
    )(x, w)


def _rope_tables(pos):
    def tab(half):
        inv = ROPE_THETA ** (-jnp.arange(half, dtype=F32) / half)
        ang = pos.astype(F32)[:, None] * inv
        return jnp.cos(ang), jnp.sin(ang)
    c, s = tab(HEAD_DIM // 2)
    c128 = jnp.concatenate([c, c], -1)
    s128 = jnp.concatenate([-s, s], -1)
    c, s = tab(IDX_D // 2)
    c64 = jnp.concatenate([c, c, c, c], -1)
    s64 = jnp.concatenate([-s, s, -s, s], -1)
    return c128, s128, c64, s64


def _rope128(x, cos, sin):
    return x * cos + pltpu.roll(x, HEAD_DIM // 2, 1) * sin


def _rope64x2(x, cos, sin):
    lane = lax.broadcasted_iota(I32, x.shape, 1)
    first = (lane % IDX_D) < (IDX_D // 2)
    partner = jnp.where(first, pltpu.roll(x, 128 - IDX_D // 2, 1), pltpu.roll(x, IDX_D // 2, 1))
    return x * cos + partner * sin


def _attn_prep_kernel(aq, ak, av, iq, sm, c128, s128, c64, s64, k_st, v_st, ik_st,
                      ks_o, vs_o, iks_o, kb_o, vt_o, qt_o, iqt_o, ikb_o, iwt_o, *, tr):
    del k_st, v_st, ik_st
    cos = c128[...]
    sin = s128[...]
    for h in range(N_Q):
        sl = slice(h * HEAD_DIM, (h + 1) * HEAD_DIM)
        y = _rope128(aq[:, sl], cos, sin) * Q_SCALE_LOG2
        qt_o[sl, :] = y.T.astype(BF)
    for h in range(N_KV):
        sl = slice(h * HEAD_DIM, (h + 1) * HEAD_DIM)
        y = _rope128(ak[:, sl], cos, sin)
        v = av[:, sl]
        ks_o[pl.ds(h, tr, stride=N_KV), :] = y
        vs_o[pl.ds(h, tr, stride=N_KV), :] = v
        kb_o[:, sl] = y.astype(BF)
        vt_o[sl, :] = v.T.astype(BF)
    cos = c64[...]
    sin = s64[...]
    for j in range(IDX_H * IDX_D // 128):
        sl = slice(j * 128, (j + 1) * 128)
        y = _rope64x2(iq[:, sl], cos, sin) * (IDX_D ** -0.5)
        iqt_o[sl, :] = y.T.astype(BF)
    x = sm[...]
    y = _rope64x2(x, cos, sin)
    iks_o[...] = y.T[SM_IK:SM_IK + IDX_D, :]
    ikb_o[...] = y[:, SM_IK:SM_IK + IDX_D].astype(BF)
    iwt_o[...] = x.T[SM_IW:SM_IW + IDX_H, :] * (IDX_H ** -0.5)


def _attn_prep(h3, tabs, tr, layer, k_st, v_st, ik_st):
    b, t, _ = h3.shape
    nr = t // tr
    col = lambda w, off: pl.BlockSpec((None, tr, w), lambda bi, r, o=off // w: (bi, r, o))
    tab = pl.BlockSpec((tr, 128), lambda bi, r: (r, 0))
    anyspec = pl.BlockSpec(memory_space=pl.ANY)
    sds = jax.ShapeDtypeStruct
    return pl.pallas_call(
        functools.partial(_attn_prep_kernel, tr=tr),
        grid=(b, nr),
        in_specs=[col(ATTN_W, C_AQ), col(KV_W, C_AK), col(KV_W, C_AV), col(ATTN_W, C_IQ), col(128, C_SM),
                  tab, tab, tab, tab, anyspec, anyspec, anyspec],
        out_specs=[
            pl.BlockSpec((None, None, tr * N_KV, HEAD_DIM), lambda bi, r: (layer, bi, r, 0)),
            pl.BlockSpec((None, None, tr * N_KV, HEAD_DIM), lambda bi, r: (layer, bi, r, 0)),
            pl.BlockSpec((None, None, IDX_D, tr), lambda bi, r: (layer, bi, 0, r)),
            pl.BlockSpec((None, tr, KV_W), lambda bi, r: (bi, r, 0)),
            pl.BlockSpec((None, None, KV_W, tr), lambda bi, r: (bi, r, 0, 0)),
            pl.BlockSpec((None, ATTN_W, tr), lambda bi, r: (bi, 0, r)),
            pl.BlockSpec((None, ATTN_W, tr), lambda bi, r: (bi, 0, r)),
            pl.BlockSpec((None, tr, IDX_D), lambda bi, r: (bi, r, 0)),
            pl.BlockSpec((None, IDX_H, tr), lambda bi, r: (bi, 0, r)),
        ],
        out_shape=[
            sds(k_st.shape, F32), sds(v_st.shape, F32), sds(ik_st.shape, F32),
            sds((b, t, KV_W), BF),
            sds((b, nr, KV_W, tr), BF),
            sds((b, ATTN_W, t), BF),
            sds((b, ATTN_W, t), BF),
            sds((b, t, IDX_D), BF),
            sds((b, IDX_H, t), F32),
        ],
        input_output_aliases={9: 0, 10: 1, 11: 2},
        compiler_params=_cp("parallel", "parallel"),
        name="attn_prep",
    )(h3, h3, h3, h3, h3, *tabs, k_st, v_st, ik_st)


def _sortable(x):
    bits = pltpu.bitcast(x, I32)
    return jnp.where(bits >= 0, bits, bits ^ jnp.int32(0x7FFFFFFF))


def _kth_largest(count_ge, shape, k):
    lo = jnp.where(count_ge(jnp.zeros(shape, I32)) >= k, 0, INT_MIN).astype(I32)

    def body(it, lo):
        cand = lo + (jnp.int32(1) << (30 - it))
        return jnp.where(count_ge(cand) >= k, cand, lo)

    return lax.fori_loop(0, 31, body, lo)


def _dsa_prompt_kernel(qt, iqt, iwt, kb, vt, ikb, o_ref, keys, bias, m_ref, l_ref, acc_ref, s_ref, *, tq, topk):
    i = pl.program_id(1)
    nch = i + 1
    row = lax.broadcasted_iota(I32, (tq, tq), 0)
    lane = lax.broadcasted_iota(I32, (tq, tq), 1)

    def off_of(c):
        return pl.multiple_of(c * tq, tq)

    def causal_of(c):
        return (c * tq + row) <= (i * tq + lane)

    def fold8(x):
        return x.reshape(tq // 8, 8, tq)

    def indexer(c, carry):
        off = off_of(c)
        ikc = ikb[pl.ds(off, tq), :]
        acc = jnp.zeros((tq, tq), F32)
        for h in range(IDX_H):
            s = _dot(ikc, iqt[h * IDX_D:(h + 1) * IDX_D, :])
            acc = acc + jnp.maximum(s, 0.0) * iwt[h:h + 1, :]
        acc = jnp.where(causal_of(c), acc, NEG_INF)
        keys[pl.ds(off, tq), :] = _sortable(acc)
        return carry

    lax.fori_loop(0, nch, indexer, 0)

    def count_ge(cand):
        def body(c, cnt):
            kc = keys[pl.ds(off_of(c), tq), :]
            return cnt + fold8(jnp.where(kc >= cand, 1, 0)).sum(axis=0)
        cnt = lax.fori_loop(0, nch, body, jnp.zeros((8, tq), I32))
        return cnt.sum(axis=0, keepdims=True)

    theta = _kth_largest(count_ge, (1, tq), topk)

    def make_bias(c, carry):
        off = off_of(c)
        sel = keys[pl.ds(off, tq), :] >= theta
        bias[pl.ds(off, tq), :] = jnp.where(causal_of(c), jnp.where(sel, 0.0, MASKED), MASKED)
        return carry

    lax.fori_loop(0, nch, make_bias, 0)

    m_ref[...] = jnp.full_like(m_ref, NEG_INF)
    l_ref[...] = jnp.zeros_like(l_ref)
    acc_ref[...] = jnp.zeros_like(acc_ref)

    def attend(c, carry):
        off = off_of(c)
        bc = bias[pl.ds(off, tq), :]
        m_new = []
        for h in range(N_Q):
            gsl = slice(h // (N_Q // N_KV) * HEAD_DIM, (h // (N_Q // N_KV) + 1) * HEAD_DIM)
            s = _dot(kb[pl.ds(off, tq), gsl], qt[h * HEAD_DIM:(h + 1) * HEAD_DIM, :]) + bc
            s_ref[h] = s
            m_new.append(jnp.maximum(m_ref[h], fold8(s).max(axis=0).max(axis=0, keepdims=True)))
        for h in range(N_Q):
            gsl = slice(h // (N_Q // N_KV) * HEAD_DIM, (h // (N_Q // N_KV) + 1) * HEAD_DIM)
            p = jnp.exp2(s_ref[h] - m_new[h])
            alpha = jnp.exp2(m_ref[h] - m_new[h])
            l_ref[h] = l_ref[h] * alpha + fold8(p).sum(axis=0).sum(axis=0, keepdims=True)
            acc_ref[h] = acc_ref[h] * alpha + _dot(vt[c, gsl, :], p.astype(BF))
            m_ref[h] = m_new[h]
        return carry

    lax.fori_loop(0, nch, attend, 0)

    for h in range(N_Q):
        o_ref[:, h * HEAD_DIM:(h + 1) * HEAD_DIM] = (acc_ref[h] * (1.0 / l_ref[h])).T.astype(BF)


def _dsa_prompt(qt, iqt, iwt, kb, vt, ikb, tq):
    b, _, t = qt.shape
    topk = min(TOPK_MAX, t // 4)
    kern = functools.partial(_dsa_prompt_kernel, tq=tq, topk=topk)
    return pl.pallas_call(
        kern,
        grid=(b, t // tq),
        in_specs=[
            pl.BlockSpec((None, ATTN_W, tq), lambda bi, i: (bi, 0, i)),
            pl.BlockSpec((None, ATTN_W, tq), lambda bi, i: (bi, 0, i)),
            pl.BlockSpec((None, IDX_H, tq), lambda bi, i: (bi, 0, i)),
            pl.BlockSpec((None, t, KV_W), lambda bi, i: (bi, 0, 0)),
            pl.BlockSpec((None, t // tq, KV_W, tq), lambda bi, i: (bi, 0, 0, 0)),
            pl.BlockSpec((None, t, IDX_D), lambda bi, i: (bi, 0, 0)),
        ],
        out_specs=pl.BlockSpec((None, tq, ATTN_W), lambda bi, i: (bi, i, 0)),
        out_shape=jax.ShapeDtypeStruct((b, t, ATTN_W), BF),
        scratch_shapes=[pltpu.VMEM((t, tq), I32), pltpu.VMEM((t, tq), F32),
                        pltpu.VMEM((N_Q, 1, tq), F32), pltpu.VMEM((N_Q, 1, tq), F32),
                        pltpu.VMEM((N_Q, HEAD_DIM, tq), F32), pltpu.VMEM((N_Q, tq, tq), F32)],
        compiler_params=_cp("parallel", "arbitrary"),
        name="dsa_prompt",
    )(qt, iqt, iwt, kb, vt, ikb)


def _l2norm(x):
    return x * lax.rsqrt(jnp.sum(x * x, -1, keepdims=True) + NORM_EPS)


def _gdn_activations(y, small, alog, dtb):
    y = y * _sigmoid(y)
    qs, ks, gs, bs = [], [], [], []
    r = y.shape[0]
    for h in range(GDN_H):
        sl = slice(h * GDN_D, (h + 1) * GDN_D)
        qs.append(_l2norm(y[:, sl]) * (GDN_D ** -0.5))
        ks.append(_l2norm(y[:, GDN_W + h * GDN_D:GDN_W + (h + 1) * GDN_D]))
        b_raw = jnp.broadcast_to(small[:, SM_GB + h:SM_GB + h + 1], (r, GDN_D))
        a_raw = jnp.broadcast_to(small[:, SM_GA + h:SM_GA + h + 1], (r, GDN_D))
        bs.append(_sigmoid(b_raw))
        gs.append(-jnp.exp(alog[:, sl]) * _softplus(a_raw + dtb[:, sl]))
    return qs, ks, y[:, 2 * GDN_W:], gs, bs


def _mix_prep_kernel(gq, gq_prev, sm, sb, sc, sx, sc_prev, sx_prev, gw, sw, alog, dtb,
                     qn_o, kn_o, vv_o, g_o, beta_o, scm_o, utail_o, xbuf, ubuf, *, tr):
    r = pl.program_id(1)
    first = r == 0
    xbuf[0:8, :] = jnp.where(first, 0.0, gq_prev[...])
    xbuf[8:, :] = gq[...]
    y = xbuf[5:5 + tr, :] * gw[0:1, :]
    for j in range(1, GDN_CONV):
        y = y + xbuf[5 + j:5 + j + tr, :] * gw[j:j + 1, :]
    qs, ks, v, gs, bs = _gdn_activations(y, sm[...], alog[...], dtb[...])
    for h in range(GDN_H):
        sl = slice(h * GDN_D, (h + 1) * GDN_D)
        qn_o[:, sl] = qs[h]
        kn_o[:, sl] = ks[h]
        g_o[:, sl] = gs[h]
        beta_o[:, sl] = bs[h]
    vv_o[...] = v

    u = sc[...] * sx[...]
    ubuf[0:8, :] = jnp.where(first, 0.0, sc_prev[...] * sx_prev[...])
    ubuf[8:, :] = u
    y = ubuf[6:6 + tr, :] * sw[0:1, :]
    for j in range(1, SC_CONV):
        y = y + ubuf[6 + j:6 + j + tr, :] * sw[j:j + 1, :]
    scm_o[...] = (sb[...] * y).astype(BF)

    @pl.when(r == pl.num_programs(1) - 1)
    def _():
        utail_o[...] = u[tr - 8:, :]


def _mix_prep(h3, gw, sw, alog, dtb, tr):
    b, t, _ = h3.shape
    nr = t // tr
    col = lambda w, off: pl.BlockSpec((None, tr, w), lambda bi, r, o=off // w: (bi, r, o))
    prev = lambda w, off: pl.BlockSpec(
        (None, 8, w), lambda bi, r, o=off // w: (bi, jnp.maximum(r * (tr // 8) - 1, 0), o))
    full = lambda a: pl.BlockSpec(a.shape, lambda bi, r: (0, 0))
    out = pl.BlockSpec((None, tr, GDN_W), lambda bi, r: (bi, r, 0))
    act = jax.ShapeDtypeStruct((b, t, GDN_W), F32)
    return pl.pallas_call(
        functools.partial(_mix_prep_kernel, tr=tr),
        grid=(b, nr),
        in_specs=[col(GDN_CONV_DIM, C_GQKV), prev(GDN_CONV_DIM, C_GQKV), col(128, C_SM),
                  col(SC_W, C_SB), col(SC_W, C_SC), col(SC_W, C_SX), prev(SC_W, C_SC), prev(SC_W, C_SX),
                  full(gw), full(sw), full(alog), full(dtb)],
        out_specs=[out, out, out, out, out, out,
                   pl.BlockSpec((None, 8, SC_W), lambda bi, r: (bi, 0, 0))],
        out_shape=[act, act, act, act, act,
                   jax.ShapeDtypeStruct((b, t, SC_W), BF),
                   jax.ShapeDtypeStruct((b, 8, SC_W), F32)],
        scratch_shapes=[pltpu.VMEM((tr + 8, GDN_CONV_DIM), F32), pltpu.VMEM((tr + 8, SC_W), F32)],
        compiler_params=_cp("parallel", "arbitrary"),
        name="mix_prep",
    )(h3, h3, h3, h3, h3, h3, h3, h3, gw, sw, alog, dtb)


def _cumsum_rows(x):
    row = lax.broadcasted_iota(I32, x.shape, 0)
    s = 1
    while s < x.shape[0]:
        x = x + jnp.where(row >= s, pltpu.roll(x, s, 0), 0.0)
        s *= 2
    return x


def _split_bf(x):
    hi = x.astype(BF)
    return hi, (x - hi.astype(F32)).astype(BF)


def _dot3(a, b):
    m = a.shape[0]
    ah, al = _split_bf(a)
    bh, bl = _split_bf(b)
    r = _dot(jnp.concatenate([ah, al], axis=0), bh)
    return r[:m] + r[m:] + _dot(ah, bl)


def _gated_rmsnorm(o, z, norm_g):
    on = o * lax.rsqrt(jnp.mean(o * o, -1, keepdims=True) + NORM_EPS) * norm_g
    return on * (z * _sigmoid(z))


def _gdn_chunk_kernel(qn, kn, vv, gb, betab, z, norm_g, o_ref, s_out, s_ref, *, nb):
    c = pl.program_id(0)
    C = GDN_CHUNK
    W = GDN_H * C

    @pl.when(c == 0)
    def _():
        s_ref[...] = jnp.zeros_like(s_ref)

    ii = lax.broadcasted_iota(I32, (C, W), 0)
    jj = lax.broadcasted_iota(I32, (C, W), 1) % C
    lower = ii >= jj
    strict = ii > jj
    eye = jnp.where(ii == jj, 1.0, 0.0).astype(F32)
    blockmask = (lax.broadcasted_iota(I32, (W, W), 0) // C) == (lax.broadcasted_iota(I32, (W, W), 1) // C)
    zpad = jnp.zeros((GDN_D - C, GDN_D), F32)
    ng = norm_g[...]
    heads = [slice(h * GDN_D, (h + 1) * GDN_D) for h in range(GDN_H)]
    lanes = lambda parts: jnp.concatenate(parts, axis=1)

    def block_diag(m):
        return jnp.where(blockmask, jnp.concatenate([m] * GDN_H, axis=0), 0.0)

    st = []
    for b in range(nb):
        q = [qn[b, :, sl] for sl in heads]
        k = [kn[b, :, sl] for sl in heads]
        beta = [betab[b, :, sl] for sl in heads]
        gc = [_cumsum_rows(gb[b, :, sl]) for sl in heads]
        kbeta = [k[h] * beta[h] for h in range(GDN_H)]
        gcol = lanes([g[:, 0:C] for g in gc])
        grow = lanes([jnp.concatenate([g, g], axis=0).T[0:C, 0:C] for g in gc])
        decay = jnp.where(lower, jnp.exp(jnp.where(lower, gcol - grow, 0.0)), 0.0)
        kk, qk = [], []
        for h in range(GDN_H):
            r = _dot_nt(jnp.concatenate([kbeta[h], q[h]], axis=0).astype(BF), k[h].astype(BF))
            kk.append(r[:C])
            qk.append(r[C:])
        a = jnp.where(strict, lanes(kk) * decay, 0.0)
        intra = lanes(qk) * decay
        rhs = jnp.concatenate(
            [lanes([vv[b, :, heads[h]] * beta[h], kbeta[h] * jnp.exp(gc[h])]) for h in range(GDN_H)], axis=0)
        st.append(dict(q=q, k=k, gc=gc, intra=intra, rhs=rhs, nm=-a))

    for d in st:
        d["x"] = eye + d["nm"]
        d["p"] = _dot3(d["nm"], block_diag(d["nm"]))
    step = 4
    while step < C:
        for d in st:
            r = _dot3(jnp.concatenate([d["x"], d["p"]], axis=0), block_diag(d["p"]))
            d["x"] = d["x"] + r[:C]
            d["p"] = r[C:]
        step *= 2
    for d in st:
        tinv = d["x"] + _dot3(d["x"], block_diag(d["p"]))
        d["uw"] = _dot3(block_diag(tinv), d["rhs"])

    for b in range(nb):
        d = st[b]
        for h in range(GDN_H):
            u = d["uw"][h * C:(h + 1) * C, 0:GDN_D]
            w = d["uw"][h * C:(h + 1) * C, GDN_D:]
            gc = d["gc"][h]
            s = s_ref[b, h]
            r = _dot(jnp.concatenate([w, d["q"][h] * jnp.exp(gc)], axis=0).astype(BF), s.astype(BF))
            v_new = u - r[:C]
            o = r[C:] + _dot(d["intra"][:, h * C:(h + 1) * C].astype(BF), v_new.astype(BF))
            g_last = gc[C - 1:C, :]
            kd = d["k"][h] * jnp.exp(g_last - gc)
            kdt = jnp.concatenate([kd, zpad], axis=0).T.astype(BF)
            vnp = jnp.concatenate([v_new, zpad], axis=0).astype(BF)
            s_ref[b, h] = s * jnp.exp(g_last) + _dot(kdt, vnp)
            o_ref[b, :, heads[h]] = _gated_rmsnorm(o, z[b, :, heads[h]], ng).astype(BF)

    @pl.when(c == pl.num_programs(0) - 1)
    def _():
        s_out[...] = s_ref[...]


def _gdn_chunked(qn, kn, vv, gb, betab, h3, norm_g):
    b, t, _ = qn.shape
    C = GDN_CHUNK
    blk = pl.BlockSpec((b, C, GDN_W), lambda c: (0, c, 0))
    return pl.pallas_call(
        functools.partial(_gdn_chunk_kernel, nb=b),
        grid=(t // C,),
        in_specs=[blk, blk, blk, blk, blk,
                  pl.BlockSpec((b, C, GDN_W), lambda c: (0, c, C_GZ // GDN_W)),
                  pl.BlockSpec((1, GDN_D), lambda c: (0, 0))],
        out_specs=[pl.BlockSpec((b, C, GDN_W), lambda c: (0, c, 0)),
                   pl.BlockSpec((b, GDN_H, GDN_D, GDN_D), lambda c: (0, 0, 0, 0))],
        out_shape=[jax.ShapeDtypeStruct((b, t, GDN_W), BF),
                   jax.ShapeDtypeStruct((b, GDN_H, GDN_D, GDN_D), F32)],
        scratch_shapes=[pltpu.VMEM((b, GDN_H, GDN_D, GDN_D), F32)],
        compiler_params=_cp("arbitrary"),
        name="gdn_chunked",
    )(qn, kn, vv, gb, betab, h3, norm_g)


def _layer_norm(y, g, b):
    mu = jnp.mean(y, -1, keepdims=True)
    yc = y - mu
    var = jnp.mean(yc * yc, -1, keepdims=True)
    return yc * lax.rsqrt(var + LN_EPS) * g + b


def _out_proj_kernel(attn, gdn, scm, x, wo, g, b, o_ref):
    y = _dot(attn[...], wo[0:ATTN_W, :])
    y = y + _dot(gdn[...], wo[ATTN_W:ATTN_W + GDN_W, :])
    y = y + _dot(scm[...], wo[ATTN_W + GDN_W:, :])
    o_ref[...] = _layer_norm(DN_ALPHA * x[...] + y, g[...], b[...])


def _out_proj(attn, gdn, scm, x, wo, layer, g, b, tm):
    m = x.shape[0]
    row = lambda w: pl.BlockSpec((tm, w), lambda i: (i, 0))
    full = lambda a: pl.BlockSpec((None,) + a.shape[1:], lambda i: (layer, 0, 0))
    return pl.pallas_call(
        _out_proj_kernel,
        grid=(m // tm,),
        in_specs=[row(ATTN_W), row(GDN_W), row(SC_W), row(D_MODEL), full(wo), full(g), full(b)],
        out_specs=row(D_MODEL),
        out_shape=jax.ShapeDtypeStruct((m, D_MODEL), F32),
        compiler_params=_cp("parallel"),
        name="out_proj_ln",
    )(attn, gdn, scm, x, wo, g, b)


def _ffn_kernel(h, wg, wu, wd, g, b, o_ref, hb, acc):
    f = pl.program_id(1)

    @pl.when(f == 0)
    def _():
        hb[...] = h[...].astype(BF)
        acc[...] = jnp.zeros_like(acc)

    x = hb[...]
    gate = _dot(x, wg[...])
    hid = gate * _sigmoid(gate) * _dot(x, wu[...])
    acc[...] += _dot(hid.astype(BF), wd[...])

    @pl.when(f == pl.num_programs(1) - 1)
    def _():
        o_ref[...] = _layer_norm(DN_ALPHA * h[...] + acc[...], g[...], b[...])


def _ffn(h, wg, wu, wd, layer, g, b, tm, tf):
    m = h.shape[0]
    full = lambda a: pl.BlockSpec((None,) + a.shape[1:], lambda i, f: (layer, 0, 0))
    return pl.pallas_call(
        _ffn_kernel,
        grid=(m // tm, FFN_DIM // tf),
        in_specs=[pl.BlockSpec((tm, D_MODEL), lambda i, f: (i, 0)),
                  pl.BlockSpec((None, D_MODEL, tf), lambda i, f: (layer, 0, f)),
                  pl.BlockSpec((None, D_MODEL, tf), lambda i, f: (layer, 0, f)),
                  pl.BlockSpec((None, tf, D_MODEL), lambda i, f: (layer, f, 0)),
                  full(g), full(b)],
        out_specs=pl.BlockSpec((tm, D_MODEL), lambda i, f: (i, 0)),
        out_shape=jax.ShapeDtypeStruct((m, D_MODEL), F32),
        scratch_shapes=[pltpu.VMEM((tm, D_MODEL), BF), pltpu.VMEM((tm, D_MODEL), F32)],
        compiler_params=_cp("parallel", "arbitrary"),
        name="ffn_ln",
    )(h, wg, wu, wd, g, b)


def _head_rows(v):
    return jnp.repeat(v.astype(F32), GDN_D)[None, :]


def _prompt_layer(xp, tabs, lw, stacks, tm_in=1024, tn_in=768, tr=256, tm_out=512, tm_ffn=512, tf=512):
    b, t, _ = xp.shape
    x2 = xp.reshape(b * t, D_MODEL)
    layer = lw["layer"]
    h = _matmul(x2, lw["w_in"], layer, min(tm_in, b * t), tn_in)
    h3 = h.reshape(b, t, IN_PACKED)
    k_st, v_st, ik_st, kb, vt, qt, iqt, ikb, iwt = _attn_prep(h3, tabs, tr, layer, *stacks)
    attn = _dsa_prompt(qt, iqt, iwt, kb, vt, ikb, tr)
    qn, kn, vv, gb, betab, scm, utail = _mix_prep(h3, lw["gdn_conv_w"], lw["sc_conv_w"],
                                                 lw["alog"], lw["dtb"], tr)
    gdn, s_new = _gdn_chunked(qn, kn, vv, gb, betab, h3, lw["norm_g"])
    h1 = _out_proj(attn.reshape(b * t, ATTN_W), gdn.reshape(b * t, GDN_W), scm.reshape(b * t, SC_W),
                   x2, lw["w_out"], layer, lw["ln1_g"], lw["ln1_b"], min(tm_out, b * t))
    y = _ffn(h1, lw["w_gate"], lw["w_up"], lw["w_down"], layer, lw["ln2_g"], lw["ln2_b"],
             min(tm_ffn, b * t), tf)
    outs = dict(
        stacks=(k_st, v_st, ik_st),
        s=s_new,
        gconv=h3[:, t - (GDN_CONV - 1):, C_GQKV:C_GQKV + GDN_CONV_DIM],
        sconv=utail[:, 8 - (SC_CONV - 1):, :],
        attn=attn, gdn=gdn, scm=scm,
    )
    return y.reshape(b, t, D_MODEL), outs


def _sample_prep_kernel(h, c128, s128, c64, s64, gst, sst, s_in, gw, sw, alog, dtb, norm_g,
                        q_o, k_o, iq_o, ik_o, iw_o, gst_o, sst_o, s_o, gdn_o, scm_o, o_buf, *, nb):
    cos = c128[...]
    sin = s128[...]
    for hh in range(N_Q):
        sl = slice(hh * HEAD_DIM, (hh + 1) * HEAD_DIM)
        q_o[:, sl] = _rope128(h[:, C_AQ + hh * HEAD_DIM:C_AQ + (hh + 1) * HEAD_DIM], cos, sin) * (HEAD_DIM ** -0.5)
    for hh in range(N_KV):
        sl = slice(hh * HEAD_DIM, (hh + 1) * HEAD_DIM)
        k_o[:, sl] = _rope128(h[:, C_AK + hh * HEAD_DIM:C_AK + (hh + 1) * HEAD_DIM], cos, sin)
    cos = c64[...]
    sin = s64[...]
    for j in range(IDX_H * IDX_D // 128):
        sl = slice(j * 128, (j + 1) * 128)
        iq_o[:, sl] = _rope64x2(h[:, C_IQ + j * 128:C_IQ + (j + 1) * 128], cos, sin) * (IDX_D ** -0.5)
    small = h[:, C_SM:C_SM + 128]
    ik_o[...] = _rope64x2(small, cos, sin)[:, SM_IK:SM_IK + IDX_D]
    iw_o[...] = small[:, SM_IW:SM_IW + IDX_H] * (IDX_H ** -0.5)

    gq = h[:, C_GQKV:C_GQKV + GDN_CONV_DIM]
    y = gq * gw[GDN_CONV - 1:GDN_CONV, :]
    for j in range(GDN_CONV - 1):
        y = y + gst[j] * gw[j:j + 1, :]
    for j in range(GDN_CONV - 2):
        gst_o[j] = gst[j + 1]
    gst_o[GDN_CONV - 2] = gq
    qs, ks, v, gs, bs = _gdn_activations(y, small, alog[...], dtb[...])
    for b in range(nb):
        for hh in range(GDN_H):
            sl = slice(hh * GDN_D, (hh + 1) * GDN_D)
            kc = jnp.broadcast_to(ks[hh][b:b + 1, :], (GDN_D, GDN_D)).T
            qc = jnp.broadcast_to(qs[hh][b:b + 1, :], (GDN_D, GDN_D)).T
            s = s_in[b, hh] * jnp.exp(gs[hh][b:b + 1, :])
            ks_row = jnp.sum(kc * s, axis=0, keepdims=True)
            delta = (v[b:b + 1, sl] - ks_row) * bs[hh][b:b + 1, :]
            s = s + kc * delta
            s_o[b, hh] = s
            o_buf[b:b + 1, sl] = jnp.sum(qc * s, axis=0, keepdims=True)
    ng = norm_g[...]
    for hh in range(GDN_H):
        sl = slice(hh * GDN_D, (hh + 1) * GDN_D)
        z = h[:, C_GZ + hh * GDN_D:C_GZ + (hh + 1) * GDN_D]
        gdn_o[:, sl] = _gated_rmsnorm(o_buf[:, sl], z, ng).astype(BF)

    u = h[:, C_SC:C_SC + SC_W] * h[:, C_SX:C_SX + SC_W]
    y = u * sw[SC_CONV - 1:SC_CONV, :]
    for j in range(SC_CONV - 1):
        y = y + sst[j] * sw[j:j + 1, :]
    for j in range(SC_CONV - 2):
        sst_o[j] = sst[j + 1]
    sst_o[SC_CONV - 2] = u
    scm_o[...] = (h[:, C_SB:C_SB + SC_W] * y).astype(BF)


def _sample_prep(h, tabs, gst, sst, s_in, gw, sw, alog, dtb, norm_g):
    nb = h.shape[0]
    sds = jax.ShapeDtypeStruct
    return pl.pallas_call(
        functools.partial(_sample_prep_kernel, nb=nb),
        out_shape=[sds((nb, ATTN_W), F32), sds((nb, KV_W), F32), sds((nb, IDX_H * IDX_D), F32),
                   sds((nb, IDX_D), F32), sds((nb, IDX_H), F32),
                   sds(gst.shape, F32), sds(sst.shape, F32), sds(s_in.shape, F32),
                   sds((nb, GDN_W), BF), sds((nb, SC_W), BF)],
        scratch_shapes=[pltpu.VMEM((nb, GDN_W), F32)],
        compiler_params=pltpu.CompilerParams(vmem_limit_bytes=VMEM_LIMIT),
        name="sample_prep",
    )(h, *tabs, gst, sst, s_in, gw, sw, alog, dtb, norm_g)


def _sample_index_kernel(pt, *refs, pps, topk):
    pages = refs[:pps]
    iq, iw, ikn, expand, bias_o, seln_o, keys = refs[pps:]
    s = pl.program_id(1)
    iqb = iq[...]
    w = jnp.broadcast_to(iw[...], (IDX_H, PAGE))
    for j in range(pps):
        sc = jnp.maximum(_dot(iqb, pages[j][...].astype(BF)), 0.0)
        keys[pl.ds(s * pps + j, 1), :] = _sortable(jnp.sum(sc * w, axis=0, keepdims=True))

    @pl.when(s == pl.num_programs(1) - 1)
    def _():
        prod = iqb.astype(F32) * ikn[...].astype(BF).astype(F32)
        sn = jnp.sum(prod, axis=1, keepdims=True)
        sn = jnp.sum(jnp.maximum(sn, 0.0) * iw[...], axis=0, keepdims=True)
        key_new = jnp.broadcast_to(_sortable(sn), (1, PAGE))
        kall = keys[...]

        def count_ge(cand):
            c = jnp.where(kall >= cand, 1.0, 0.0).sum(axis=0, keepdims=True).sum(axis=1, keepdims=True)
            return jnp.broadcast_to(c, (1, PAGE)) + jnp.where(key_new >= cand, 1.0, 0.0)

        theta = _kth_largest(count_ge, (1, PAGE), float(topk))
        sel = _dot(jnp.where(kall >= theta, 1.0, 0.0).astype(BF), expand[...])
        bias_o[...] = jnp.where(sel > 0.5, 0.0, MASKED)
        seln_o[...] = jnp.where(key_new >= theta, 0.0, MASKED)


def _expand_matrix():
    return jnp.asarray(np.arange(PAGE)[:, None] == (np.arange(PAGE * N_KV)[None, :] // N_KV), BF)


def _sample_index(page_table, kidx_t, layer, iq3, iw3, ikn3, pps=16):
    nb, npages = page_table.shape
    topk = min(TOPK_MAX, (npages * PAGE + 1) // 4)
    page = lambda j: pl.BlockSpec((None, None, IDX_D, PAGE),
                                  lambda b, s, pt, j=j: (layer, pt[b, s * pps + j], 0, 0))
    per_b = lambda shp: pl.BlockSpec((None,) + shp, lambda b, s, pt: (b, 0, 0))
    grid_spec = pltpu.PrefetchScalarGridSpec(
        num_scalar_prefetch=1,
        grid=(nb, npages // pps),
        in_specs=[page(j) for j in range(pps)]
                 + [per_b((IDX_H, IDX_D)), per_b((IDX_H, 1)), per_b((1, IDX_D)),
                    pl.BlockSpec((PAGE, PAGE * N_KV), lambda b, s, pt: (0, 0))],
        out_specs=[per_b((npages, PAGE * N_KV)), per_b((1, PAGE))],
        scratch_shapes=[pltpu.VMEM((npages, PAGE), I32)],
    )
    return pl.pallas_call(
        functools.partial(_sample_index_kernel, pps=pps, topk=topk),
        grid_spec=grid_spec,
        out_shape=[jax.ShapeDtypeStruct((nb, npages, PAGE * N_KV), F32), jax.ShapeDtypeStruct((nb, 1, PAGE), F32)],
        compiler_params=_cp("parallel", "arbitrary"),
        name="sample_index",
    )(page_table, *([kidx_t] * pps), iq3, iw3, ikn3, _expand_matrix())


def _sample_attn_kernel(pt, *refs, pps):
    kp = refs[:pps]
    vp = refs[pps:2 * pps]
    q8, bias, seln, kn8, vn8, o_ref, m_ref, l_ref, acc_ref = refs[2 * pps:]
    s = pl.program_id(1)
    rows = PAGE * N_KV

    @pl.when(s == 0)
    def _():
        m_ref[...] = jnp.full_like(m_ref, NEG_INF)
        l_ref[...] = jnp.zeros_like(l_ref)
        acc_ref[...] = jnp.zeros_like(acc_ref)

    q = q8[...]
    lane_head = lax.broadcasted_iota(I32, (N_Q, rows), 1) % N_KV
    own = lane_head == lax.broadcasted_iota(I32, (N_Q, rows), 0) // (N_Q // N_KV)
    head_mask = jnp.where(own, 0.0, MASKED)
    lg = [_dot_nt(q, kp[j][...].astype(BF)) + (bias[pl.ds(s * pps + j, 1), :] + head_mask) for j in range(pps)]
    m_old = m_ref[...]
    m_new = m_old
    for x in lg:
        m_new = jnp.maximum(m_new, jnp.max(x, axis=1, keepdims=True))
    alpha = jnp.exp(m_old - m_new)
    l_new = l_ref[...] * alpha
    acc = acc_ref[...] * alpha
    for j in range(pps):
        p = jnp.exp(lg[j] - m_new)
        l_new = l_new + jnp.sum(p, axis=1, keepdims=True)
        acc = acc + _dot(p.astype(BF), vp[j][...].astype(BF))
    m_ref[...] = m_new
    l_ref[...] = l_new
    acc_ref[...] = acc

    @pl.when(s == pl.num_programs(1) - 1)
    def _():
        knb = kn8[...].astype(BF).astype(F32)
        vnb = vn8[...].astype(BF).astype(F32)
        x = jnp.sum(q.astype(F32) * knb, axis=1, keepdims=True) + seln[:, 0:1]
        m_fin = jnp.maximum(m_new, x)
        a = jnp.exp(m_new - m_fin)
        p = jnp.exp(x - m_fin)
        l_fin = l_new * a + p
        o_ref[...] = ((acc * a + p.astype(BF).astype(F32) * vnb) / l_fin).astype(BF)


def _sample_attn(page_table, ck2, cv2, layer, q8, bias, seln, kn8, vn8, pps=8):
    nb, npages = page_table.shape
    rows = PAGE * N_KV
    page = lambda j: pl.BlockSpec((None, None, rows, HEAD_DIM),
                                  lambda b, s, pt, j=j: (layer, pt[b, s * pps + j], 0, 0))
    per_b = lambda shp: pl.BlockSpec((None,) + shp, lambda b, s, pt: (b, 0, 0))
    grid_spec = pltpu.PrefetchScalarGridSpec(
        num_scalar_prefetch=1,
        grid=(nb, npages // pps),
        in_specs=[page(j) for j in range(pps)] * 2
                 + [per_b((N_Q, HEAD_DIM)), per_b((npages, rows)), per_b((1, PAGE)),
                    per_b((N_Q, HEAD_DIM)), per_b((N_Q, HEAD_DIM))],
        out_specs=per_b((N_Q, HEAD_DIM)),
        scratch_shapes=[pltpu.VMEM((N_Q, 1), F32), pltpu.VMEM((N_Q, 1), F32), pltpu.VMEM((N_Q, HEAD_DIM), F32)],
    )
    return pl.pallas_call(
        functools.partial(_sample_attn_kernel, pps=pps),
        grid_spec=grid_spec,
        out_shape=jax.ShapeDtypeStruct((nb, N_Q, HEAD_DIM), BF),
        compiler_params=_cp("parallel", "arbitrary"),
        name="sample_attn",
    )(page_table, *([ck2] * pps), *([cv2] * pps), q8, bias, seln, kn8, vn8)


SAMPLE_ROWS = 16


def _sample_layer(xs, tabs, lw, layer, page_table, cache_k, cache_v, cache_kidx, gst, sst, s_in):
    nb = page_table.shape[0]
    h = _matmul(xs, lw["w_in"], layer, SAMPLE_ROWS, 768)[:nb]
    q, k, iq, ik, iw, gst_n, sst_n, s_n, gdn, scm = _sample_prep(
        h, tabs, gst, sst, s_in, lw["gdn_conv_w"], lw["sc_conv_w"], lw["alog"], lw["dtb"], lw["norm_g"])
    v = h[:, C_AV:C_AV + KV_W]
    bias, seln = _sample_index(page_table, cache_kidx, layer, iq.reshape(nb, IDX_H, IDX_D).astype(BF),
                               iw.reshape(nb, IDX_H, 1), ik.reshape(nb, 1, IDX_D))
    own_kv = lambda a: jnp.repeat(a.reshape(nb, N_KV, HEAD_DIM), N_Q // N_KV, axis=1)
    attn = _sample_attn(page_table, cache_k, cache_v, layer, q.reshape(nb, N_Q, HEAD_DIM).astype(BF),
                        bias, seln, own_kv(k), own_kv(v)).reshape(nb, ATTN_W)
    padr = lambda a: jnp.pad(a, ((0, SAMPLE_ROWS - nb), (0, 0)))
    h1 = _out_proj(padr(attn), padr(gdn), padr(scm), xs, lw["w_out"], layer, lw["ln1_g"], lw["ln1_b"],
                   SAMPLE_ROWS)
    y = _ffn(h1, lw["w_gate"], lw["w_up"], lw["w_down"], layer, lw["ln2_g"], lw["ln2_b"], SAMPLE_ROWS, 512)
    outs = dict(k=k.reshape(nb, 1, N_KV, HEAD_DIM), v=v.reshape(nb, 1, N_KV, HEAD_DIM), kidx=ik.reshape(nb, 1, IDX_D),
                s=s_n, gconv=jnp.swapaxes(gst_n, 0, 1), sconv=jnp.swapaxes(sst_n, 0, 1),
                attn=attn, gdn=gdn, scm=scm)
    return y, outs


def kernel(x_prompt, x_sample, cache_k, cache_v, cache_kidx, page_table, state_gdn, state_gdn_conv,
           state_sc_conv, w_in, gdn_conv_w, gdn_a_log, gdn_dt_bias, gdn_norm_g, sc_conv_w, w_out,
           ln1_g, ln1_b, w_gate, w_up, w_down, ln2_g, ln2_b):
    bp, tp, _ = x_prompt.shape
    nb, ts, _ = x_sample.shape
    assert ts == 1
    past = page_table.shape[1] * PAGE
    tabs_p = _rope_tables(jnp.arange(tp))
    tabs_s = _rope_tables(past + jnp.arange(ts))
    w_in_p = _pack_w_in(w_in)
    w_out_b, w_gate_b, w_up_b, w_down_b = (w.astype(BF) for w in (w_out, w_gate, w_up, w_down))
    depth, pool = cache_k.shape[:2]
    cache_k = cache_k.reshape(depth, pool, PAGE * N_KV, HEAD_DIM)
    cache_v = cache_v.reshape(depth, pool, PAGE * N_KV, HEAD_DIM)
    cache_kidx = jnp.swapaxes(cache_kidx, 2, 3)

    xp = x_prompt
    xs = jnp.pad(x_sample.reshape(nb, D_MODEL), ((0, SAMPLE_ROWS - nb), (0, 0)))
    stacks = (jnp.zeros((DEPTH, bp, tp * N_KV, HEAD_DIM), F32), jnp.zeros((DEPTH, bp, tp * N_KV, HEAD_DIM), F32),
              jnp.zeros((DEPTH, bp, IDX_D, tp), F32))
    po, so = [], []
    for l in range(DEPTH):
        lw = dict(layer=l, w_in=w_in_p, gdn_conv_w=gdn_conv_w[l], sc_conv_w=sc_conv_w[l],
                  alog=_head_rows(gdn_a_log[l]), dtb=_head_rows(gdn_dt_bias[l]), norm_g=gdn_norm_g[l][None, :],
                  w_out=w_out_b, ln1_g=ln1_g[:, None, :], ln1_b=ln1_b[:, None, :],
                  w_gate=w_gate_b, w_up=w_up_b, w_down=w_down_b,
                  ln2_g=ln2_g[:, None, :], ln2_b=ln2_b[:, None, :])
        xp, o = _prompt_layer(xp, tabs_p, lw, stacks)
        stacks = o["stacks"]
        po.append(o)
        xs, o = _sample_layer(xs, tabs_s, lw, l, page_table, cache_k, cache_v, cache_kidx,
                              jnp.swapaxes(state_gdn_conv[l], 0, 1), jnp.swapaxes(state_sc_conv[l], 0, 1),
                              state_gdn[l])
        so.append(o)

    st = lambda outs, n: jnp.stack([o[n] for o in outs])
    k_st, v_st, ik_st = stacks
    return (xp, xs[:nb].reshape(nb, ts, D_MODEL),
            k_st.reshape(DEPTH, bp, tp, N_KV, HEAD_DIM), v_st.reshape(DEPTH, bp, tp, N_KV, HEAD_DIM),
            jnp.swapaxes(ik_st, 2, 3), st(so, "k"), st(so, "v"), st(so, "kidx"),
            st(po, "s"), st(so, "s"), st(po, "gconv"), st(so, "gconv"), st(po, "sconv"), st(so, "sconv"))
```

```python
import functools
import math

import numpy as np
import jax
import jax.numpy as jnp
from jax import lax
from jax.experimental import pallas as pl
from jax.experimental.pallas import tpu as pltpu

F32 = jnp.float32
BF = jnp.bfloat16
I32 = jnp.int32

D_MODEL = 2048
DEPTH = 4
PAGE = 128
HEAD_DIM = 128
N_Q = 8
N_KV = 4
ATTN_W = N_Q * HEAD_DIM
KV_W = N_KV * HEAD_DIM
IDX_H = 16
IDX_D = 64
TOPK_MAX = 256
ROPE_THETA = 10000.0
GDN_H = 4
GDN_D = 128
GDN_W = GDN_H * GDN_D
GDN_CONV = 4
GDN_CONV_DIM = 3 * GDN_W
GDN_CHUNK = 64
SC_W = 512
SC_CONV = 3
FFN_DIM = 5632
DN_ALPHA = (2 * DEPTH) ** 0.25
LN_EPS = 1e-5
NORM_EPS = 1e-6
IN_SIZES = (ATTN_W, KV_W, KV_W, IDX_H * IDX_D, IDX_D, IDX_H, GDN_CONV_DIM, GDN_W, GDN_H, GDN_H,
            SC_W, SC_W, SC_W)

C_AQ, C_AK, C_AV, C_IQ, C_GQKV, C_GZ, C_SB, C_SC, C_SX, C_SM = (
    0, 1024, 1536, 2048, 3072, 4608, 5120, 5632, 6144, 6656)
SM_IK, SM_IW, SM_GB, SM_GA = 0, 64, 80, 84
IN_PACKED = 6912

NEG_INF = float("-inf")
MASKED = -1e30
Q_SCALE_LOG2 = HEAD_DIM ** -0.5 * math.log2(math.e)
VT_ROWS = HEAD_DIM + 16
SUB = 64
INT_MIN = -2 ** 31

VMEM_LIMIT = 56 * 1024 * 1024


def _cp(*sem):
    return pltpu.CompilerParams(dimension_semantics=tuple(sem), vmem_limit_bytes=VMEM_LIMIT)


def _sigmoid(x):
    return 1.0 / (1.0 + jnp.exp(-x))


def _softplus(x):
    return jnp.maximum(x, 0.0) + jnp.log1p(jnp.exp(-jnp.abs(x)))


def _dot(a, b):
    return jnp.dot(a, b, preferred_element_type=F32)


def _dot_hi(a, b):
    return jnp.dot(a, b, preferred_element_type=F32, precision=lax.Precision.HIGHEST)


def _dot_nt(a, b):
    return lax.dot_general(a, b, (((1,), (1,)), ((), ())), preferred_element_type=F32)


_IN_OFF = np.concatenate([[0], np.cumsum(IN_SIZES)]).tolist()
_O_IK, _O_GQKV, _O_GB, _O_SB, _O_END = _IN_OFF[4], _IN_OFF[6], _IN_OFF[8], _IN_OFF[10], _IN_OFF[13]


PACK_COLS = 256
_PB_G, _PB_S, _PB_SM = C_GQKV // PACK_COLS, C_SB // PACK_COLS, C_SM // PACK_COLS
_SH_G, _SH_S = _O_GQKV - C_GQKV, _O_SB - C_SB


def _pack_kernel(a, b, o):
    j = pl.program_id(1)

    def emit(src):
        o[...] = src.T.astype(BF)

    @pl.when(j < _PB_G)
    def _():
        emit(a[...])

    @pl.when((j >= _PB_G) & (j < _PB_S))
    def _():
        emit(jnp.concatenate([a[_SH_G:, :], b[:_SH_G, :]], axis=0))

    @pl.when((j >= _PB_S) & (j < _PB_SM))
    def _():
        emit(jnp.concatenate([a[_SH_S:, :], b[:_SH_S, :]], axis=0))

    @pl.when(j == _PB_SM)
    def _():
        n_idx = _O_GQKV - _O_IK
        n_gdn = _O_SB - _O_GB
        lo = _O_GB - (_O_GB // 128) * 128
        pad = jnp.zeros((PACK_COLS - n_idx - n_gdn, a.shape[1]), F32)
        emit(jnp.concatenate([a[:n_idx, :], b[lo:lo + n_gdn, :], pad], axis=0))


def _pack_w_in(w_in):
    wt = jnp.swapaxes(w_in, 1, 2)
    depth, n, d = wt.shape
    nblk = IN_PACKED // PACK_COLS
    last_b = (n - 1) // 128

    def a_map(l, j):
        return l, jnp.where(j == _PB_SM, _O_IK // PACK_COLS, j), 0

    def b_map(l, j):
        return l, jnp.where(j == _PB_SM, _O_GB // 128, jnp.minimum(2 * j + 2, last_b)), 0

    return pl.pallas_call(
        _pack_kernel,
        grid=(depth, nblk),
        in_specs=[pl.BlockSpec((None, PACK_COLS, d), a_map), pl.BlockSpec((None, 128, d), b_map)],
        out_specs=pl.BlockSpec((None, d, PACK_COLS), lambda l, j: (l, 0, j)),
        out_shape=jax.ShapeDtypeStruct((depth, d, IN_PACKED), BF),
        compiler_params=_cp("parallel", "parallel"),
        name="pack_w_in",
    )(wt, wt)


def _mm_kernel(x_ref, w_ref, o_ref):
    o_ref[...] = _dot(x_ref[...].astype(BF), w_ref[...])


def _matmul(x, w, layer, tm, tn):
    m, k = x.shape
    n = w.shape[2]
    return pl.pallas_call(
        _mm_kernel,
        grid=(m // tm, n // tn),
        in_specs=[pl.BlockSpec((tm, k), lambda i, j: (i, 0)),
                  pl.BlockSpec((None, k, tn), lambda i, j: (layer, 0, j))],
        out_specs=pl.BlockSpec((tm, tn), lambda i, j: (i, j)),
        out_shape=jax.ShapeDtypeStruct((m, n), F32),
        compiler_params=_cp("parallel", "arbitrary"),
        name="proj_in",
    )(x, w)


def _rope_tables(pos):
    def tab(half):
        inv = ROPE_THETA ** (-jnp.arange(half, dtype=F32) / half)
        ang = pos.astype(F32)[:, None] * inv
        return jnp.cos(ang), jnp.sin(ang)
    c, s = tab(HEAD_DIM // 2)
    c128 = jnp.concatenate([c, c], -1)
    s128 = jnp.concatenate([-s, s], -1)
    c, s = tab(IDX_D // 2)
    c64 = jnp.concatenate([c, c, c, c], -1)
    s64 = jnp.concatenate([-s, s, -s, s], -1)
    return c128, s128, c64, s64


def _rope128(x, cos, sin):
    return x * cos + pltpu.roll(x, HEAD_DIM // 2, 1) * sin


def _rope64x2(x, cos, sin):
    lane = lax.broadcasted_iota(I32, x.shape, 1)
    first = (lane % IDX_D) < (IDX_D // 2)
    partner = jnp.where(first, pltpu.roll(x, 128 - IDX_D // 2, 1), pltpu.roll(x, IDX_D // 2, 1))
    return x * cos + partner * sin


def _attn_prep_kernel(aq, ak, av, iq, sm, c128, s128, c64, s64, k_st, v_st, ik_st,
                      ks_o, vs_o, iks_o, kb_o, vt_o, qt_o, iqt_o, ikb_o, iwt_o, *, tr):
    del k_st, v_st, ik_st
    cos = c128[...]
    sin = s128[...]
    for h in range(N_Q):
        sl = slice(h * HEAD_DIM, (h + 1) * HEAD_DIM)
        y = _rope128(aq[:, sl], cos, sin) * Q_SCALE_LOG2
        qt_o[sl, :] = y.T.astype(BF)
    for h in range(N_KV):
        sl = slice(h * HEAD_DIM, (h + 1) * HEAD_DIM)
        y = _rope128(ak[:, sl], cos, sin)
        v = av[:, sl]
        ks_o[pl.ds(h, tr, stride=N_KV), :] = y
        vs_o[pl.ds(h, tr, stride=N_KV), :] = v
        kb_o[:, sl] = y.astype(BF)
        vt_o[h * VT_ROWS:h * VT_ROWS + HEAD_DIM, :] = v.T.astype(BF)
        vt_o[h * VT_ROWS + HEAD_DIM:(h + 1) * VT_ROWS, :] = jnp.ones((VT_ROWS - HEAD_DIM, tr), BF)
    cos = c64[...]
    sin = s64[...]
    for j in range(IDX_H * IDX_D // 128):
        sl = slice(j * 128, (j + 1) * 128)
        y = _rope64x2(iq[:, sl], cos, sin) * (IDX_D ** -0.5)
        iqt_o[sl, :] = y.T.astype(BF)
    x = sm[...]
    y = _rope64x2(x, cos, sin)
    iks_o[...] = y.T[SM_IK:SM_IK + IDX_D, :]
    ikb_o[...] = y[:, SM_IK:SM_IK + IDX_D].astype(BF)
    iwt_o[...] = x.T[SM_IW:SM_IW + IDX_H, :] * (IDX_H ** -0.5)


def _attn_prep(h3, tabs, tr, layer, k_st, v_st, ik_st):
    b, t, _ = h3.shape
    nr = t // tr
    col = lambda w, off: pl.BlockSpec((None, tr, w), lambda bi, r, o=off // w: (bi, r, o))
    tab = pl.BlockSpec((tr, 128), lambda bi, r: (r, 0))
    anyspec = pl.BlockSpec(memory_space=pl.ANY)
    sds = jax.ShapeDtypeStruct
    return pl.pallas_call(
        functools.partial(_attn_prep_kernel, tr=tr),
        grid=(b, nr),
        in_specs=[col(ATTN_W, C_AQ), col(KV_W, C_AK), col(KV_W, C_AV), col(ATTN_W, C_IQ), col(128, C_SM),
                  tab, tab, tab, tab, anyspec, anyspec, anyspec],
        out_specs=[
            pl.BlockSpec((None, None, tr * N_KV, HEAD_DIM), lambda bi, r: (layer, bi, r, 0)),
            pl.BlockSpec((None, None, tr * N_KV, HEAD_DIM), lambda bi, r: (layer, bi, r, 0)),
            pl.BlockSpec((None, None, IDX_D, tr), lambda bi, r: (layer, bi, 0, r)),
            pl.BlockSpec((None, tr, KV_W), lambda bi, r: (bi, r, 0)),
            pl.BlockSpec((None, None, N_KV * VT_ROWS, tr), lambda bi, r: (bi, r, 0, 0)),
            pl.BlockSpec((None, ATTN_W, tr), lambda bi, r: (bi, 0, r)),
            pl.BlockSpec((None, ATTN_W, tr), lambda bi, r: (bi, 0, r)),
            pl.BlockSpec((None, tr, IDX_D), lambda bi, r: (bi, r, 0)),
            pl.BlockSpec((None, IDX_H, tr), lambda bi, r: (bi, 0, r)),
        ],
        out_shape=[
            sds(k_st.shape, F32), sds(v_st.shape, F32), sds(ik_st.shape, F32),
            sds((b, t, KV_W), BF),
            sds((b, nr, N_KV * VT_ROWS, tr), BF),
            sds((b, ATTN_W, t), BF),
            sds((b, ATTN_W, t), BF),
            sds((b, t, IDX_D), BF),
            sds((b, IDX_H, t), F32),
        ],
        input_output_aliases={9: 0, 10: 1, 11: 2},
        compiler_params=_cp("parallel", "parallel"),
        name="attn_prep",
    )(h3, h3, h3, h3, h3, *tabs, k_st, v_st, ik_st)


def _sortable(x):
    bits = pltpu.bitcast(x, I32)
    return jnp.where(bits >= 0, bits, bits ^ jnp.int32(0x7FFFFFFF))


def _kth_largest(count_ge, shape, k):
    lo = jnp.where(count_ge(jnp.zeros(shape, I32)) >= k, 0, INT_MIN).astype(I32)

    def body(it, lo):
        cand = lo + (jnp.int32(1) << (30 - it))
        return jnp.where(count_ge(cand) >= k, cand, lo)

    return lax.fori_loop(0, 31, body, lo)


def _dsa_prompt_kernel(qt, iqt, iwt, kb, vt, ikb, o_ref, keys, bias, m_ref, l_ref, acc_ref, s_ref, *, tq, topk):
    i = pl.program_id(1)
    nch = i + 1
    row = lax.broadcasted_iota(I32, (tq, tq), 0)
    lane = lax.broadcasted_iota(I32, (tq, tq), 1)

    def off_of(c):
        return pl.multiple_of(c * tq, tq)

    def causal_of(c):
        return (c * tq + row) <= (i * tq + lane)

    def fold8(x):
        return x.reshape(tq // 8, 8, tq)

    def indexer(c, carry):
        off = off_of(c)
        ikc = ikb[pl.ds(off, tq), :]
        acc = jnp.zeros((tq, tq), F32)
        for h in range(IDX_H):
            s = _dot(ikc, iqt[h * IDX_D:(h + 1) * IDX_D, :])
            acc = acc + jnp.maximum(s, 0.0) * iwt[h:h + 1, :]
        acc = jnp.where(causal_of(c), acc, NEG_INF)
        keys[pl.ds(off, tq), :] = _sortable(acc)
        return carry

    lax.fori_loop(0, nch, indexer, 0)

    def count_ge(cand):
        def hits(c):
            hit = jnp.where(keys[pl.ds(off_of(c), tq), :] >= cand, 1, 0)
            parts = [hit[r * 8:(r + 1) * 8] for r in range(tq // 8)]
            while len(parts) > 1:
                parts = [parts[j] + parts[j + 1] for j in range(0, len(parts), 2)]
            return parts[0]

        cnt = lax.fori_loop(0, nch // 2, lambda j, cnt: cnt + (hits(2 * j) + hits(2 * j + 1)),
                            jnp.zeros((8, tq), I32))
        cnt = cnt + lax.cond(nch % 2 == 1, lambda: hits(nch - 1), lambda: jnp.zeros((8, tq), I32))
        return cnt.sum(axis=0, keepdims=True)

    theta = _kth_largest(count_ge, (1, tq), topk)

    def make_bias(c, carry):
        off = off_of(c)
        sel = keys[pl.ds(off, tq), :] >= theta
        bias[pl.ds(off, tq), :] = jnp.where(causal_of(c), jnp.where(sel, 0.0, MASKED), MASKED)
        return carry

    lax.fori_loop(0, nch, make_bias, 0)

    m_ref[...] = jnp.full_like(m_ref, NEG_INF)
    l_ref[...] = jnp.zeros_like(l_ref)
    acc_ref[...] = jnp.zeros_like(acc_ref)

    def attend(c, carry):
        off = off_of(c)
        m_new = []
        for h in range(N_Q):
            gsl = slice(h // (N_Q // N_KV) * HEAD_DIM, (h // (N_Q // N_KV) + 1) * HEAD_DIM)
            qh = qt[h * HEAD_DIM:(h + 1) * HEAD_DIM, :]
            m8 = jnp.full((8, tq), NEG_INF, F32)
            for sb in range(tq // SUB):
                rows = pl.ds(pl.multiple_of(off + sb * SUB, SUB), SUB)
                s = _dot(kb[rows, gsl], qh) + bias[rows, :]
                s_ref[h, sb * SUB:(sb + 1) * SUB, :] = s
                m8 = jnp.maximum(m8, s.reshape(SUB // 8, 8, tq).max(axis=0))
            m_new.append(jnp.maximum(m_ref[h], m8.max(axis=0, keepdims=True)))
        for h in range(N_Q):
            g = h // (N_Q // N_KV)
            p = jnp.exp2(s_ref[h] - m_new[h]).astype(BF)
            alpha = jnp.exp2(m_ref[h] - m_new[h])
            pv = _dot(vt[c, g * VT_ROWS:(g + 1) * VT_ROWS, :], p)
            acc_ref[h] = acc_ref[h] * alpha + pv[0:HEAD_DIM]
            l_ref[h] = l_ref[h] * alpha + pv[HEAD_DIM:HEAD_DIM + 1]
            m_ref[h] = m_new[h]
        return carry

    lax.fori_loop(0, nch, attend, 0)

    for h in range(N_Q):
        o_ref[:, h * HEAD_DIM:(h + 1) * HEAD_DIM] = (acc_ref[h] * (1.0 / l_ref[h])).T.astype(BF)


def _dsa_prompt(qt, iqt, iwt, kb, vt, ikb, tq):
    b, _, t = qt.shape
    topk = min(TOPK_MAX, t // 4)
    kern = functools.partial(_dsa_prompt_kernel, tq=tq, topk=topk)
    return pl.pallas_call(
        kern,
        grid=(b, t // tq),
        in_specs=[
            pl.BlockSpec((None, ATTN_W, tq), lambda bi, i: (bi, 0, i)),
            pl.BlockSpec((None, ATTN_W, tq), lambda bi, i: (bi, 0, i)),
            pl.BlockSpec((None, IDX_H, tq), lambda bi, i: (bi, 0, i)),
            pl.BlockSpec((None, t, KV_W), lambda bi, i: (bi, 0, 0)),
            pl.BlockSpec((None, t // tq, N_KV * VT_ROWS, tq), lambda bi, i: (bi, 0, 0, 0)),
            pl.BlockSpec((None, t, IDX_D), lambda bi, i: (bi, 0, 0)),
        ],
        out_specs=pl.BlockSpec((None, tq, ATTN_W), lambda bi, i: (bi, i, 0)),
        out_shape=jax.ShapeDtypeStruct((b, t, ATTN_W), BF),
        scratch_shapes=[pltpu.VMEM((t, tq), I32), pltpu.VMEM((t, tq), F32),
                        pltpu.VMEM((N_Q, 1, tq), F32), pltpu.VMEM((N_Q, 1, tq), F32),
                        pltpu.VMEM((N_Q, HEAD_DIM, tq), F32), pltpu.VMEM((N_Q, tq, tq), F32)],
        compiler_params=_cp("parallel", "arbitrary"),
        name="dsa_prompt",
    )(qt, iqt, iwt, kb, vt, ikb)


def _l2norm(x):
    return x * lax.rsqrt(jnp.sum(x * x, -1, keepdims=True) + NORM_EPS)


def _gdn_activations(y, small, alog, dtb):
    y = y * _sigmoid(y)
    qs, ks, gs, bs = [], [], [], []
    r = y.shape[0]
    for h in range(GDN_H):
        sl = slice(h * GDN_D, (h + 1) * GDN_D)
        qs.append(_l2norm(y[:, sl]) * (GDN_D ** -0.5))
        ks.append(_l2norm(y[:, GDN_W + h * GDN_D:GDN_W + (h + 1) * GDN_D]))
        b_raw = jnp.broadcast_to(small[:, SM_GB + h:SM_GB + h + 1], (r, GDN_D))
        a_raw = jnp.broadcast_to(small[:, SM_GA + h:SM_GA + h + 1], (r, GDN_D))
        bs.append(_sigmoid(b_raw))
        gs.append(-jnp.exp(alog[:, sl]) * _softplus(a_raw + dtb[:, sl]))
    return qs, ks, y[:, 2 * GDN_W:], gs, bs


def _mix_prep_kernel(gq, gq_prev, sm, sb, sc, sx, sc_prev, sx_prev, gw, sw, alog, dtb,
                     qn_o, kn_o, vv_o, g_o, beta_o, scm_o, utail_o, xbuf, ubuf, *, tr):
    r = pl.program_id(1)
    first = r == 0
    xbuf[0:8, :] = jnp.where(first, 0.0, gq_prev[...])
    xbuf[8:, :] = gq[...]
    y = xbuf[5:5 + tr, :] * gw[0:1, :]
    for j in range(1, GDN_CONV):
        y = y + xbuf[5 + j:5 + j + tr, :] * gw[j:j + 1, :]
    qs, ks, v, gs, bs = _gdn_activations(y, sm[...], alog[...], dtb[...])
    for h in range(GDN_H):
        sl = slice(h * GDN_D, (h + 1) * GDN_D)
        qn_o[:, sl] = qs[h]
        kn_o[:, sl] = ks[h]
        g_o[:, sl] = gs[h]
        beta_o[:, sl] = bs[h]
    vv_o[...] = v

    u = sc[...] * sx[...]
    ubuf[0:8, :] = jnp.where(first, 0.0, sc_prev[...] * sx_prev[...])
    ubuf[8:, :] = u
    y = ubuf[6:6 + tr, :] * sw[0:1, :]
    for j in range(1, SC_CONV):
        y = y + ubuf[6 + j:6 + j + tr, :] * sw[j:j + 1, :]
    scm_o[...] = (sb[...] * y).astype(BF)

    @pl.when(r == pl.num_programs(1) - 1)
    def _():
        utail_o[...] = u[tr - 8:, :]


def _mix_prep(h3, gw, sw, alog, dtb, tr):
    b, t, _ = h3.shape
    nr = t // tr
    col = lambda w, off: pl.BlockSpec((None, tr, w), lambda bi, r, o=off // w: (bi, r, o))
    prev = lambda w, off: pl.BlockSpec(
        (None, 8, w), lambda bi, r, o=off // w: (bi, jnp.maximum(r * (tr // 8) - 1, 0), o))
    full = lambda a: pl.BlockSpec(a.shape, lambda bi, r: (0, 0))
    out = pl.BlockSpec((None, tr, GDN_W), lambda bi, r: (bi, r, 0))
    act = jax.ShapeDtypeStruct((b, t, GDN_W), F32)
    return pl.pallas_call(
        functools.partial(_mix_prep_kernel, tr=tr),
        grid=(b, nr),
        in_specs=[col(GDN_CONV_DIM, C_GQKV), prev(GDN_CONV_DIM, C_GQKV), col(128, C_SM),
                  col(SC_W, C_SB), col(SC_W, C_SC), col(SC_W, C_SX), prev(SC_W, C_SC), prev(SC_W, C_SX),
                  full(gw), full(sw), full(alog), full(dtb)],
        out_specs=[out, out, out, out, out, out,
                   pl.BlockSpec((None, 8, SC_W), lambda bi, r: (bi, 0, 0))],
        out_shape=[act, act, act, act, act,
                   jax.ShapeDtypeStruct((b, t, SC_W), BF),
                   jax.ShapeDtypeStruct((b, 8, SC_W), F32)],
        scratch_shapes=[pltpu.VMEM((tr + 8, GDN_CONV_DIM), F32), pltpu.VMEM((tr + 8, SC_W), F32)],
        compiler_params=_cp("parallel", "arbitrary"),
        name="mix_prep",
    )(h3, h3, h3, h3, h3, h3, h3, h3, gw, sw, alog, dtb)


def _cumsum_rows(x):
    row = lax.broadcasted_iota(I32, x.shape, 0)
    s = 1
    while s < x.shape[0]:
        x = x + jnp.where(row >= s, pltpu.roll(x, s, 0), 0.0)
        s *= 2
    return x


def _split_bf(x):
    hi = x.astype(BF)
    return hi, (x - hi.astype(F32)).astype(BF)


def _dot3(a, b):
    m = a.shape[0]
    ah, al = _split_bf(a)
    bh, bl = _split_bf(b)
    r = _dot(jnp.concatenate([ah, al], axis=0), bh)
    return r[:m] + r[m:] + _dot(ah, bl)


def _gated_rmsnorm(o, z, norm_g):
    on = o * lax.rsqrt(jnp.mean(o * o, -1, keepdims=True) + NORM_EPS) * norm_g
    return on * (z * _sigmoid(z))


def _gdn_chunk_kernel(qn, kn, vv, gb, betab, z, norm_g, o_ref, s_out, s_ref, *, nb):
    c = pl.program_id(0)
    C = GDN_CHUNK
    W = GDN_H * C

    @pl.when(c == 0)
    def _():
        s_ref[...] = jnp.zeros_like(s_ref)

    ii = lax.broadcasted_iota(I32, (C, W), 0)
    jj = lax.broadcasted_iota(I32, (C, W), 1) % C
    lower = ii >= jj
    strict = ii > jj
    eye = jnp.where(ii == jj, 1.0, 0.0).astype(F32)
    blockmask = (lax.broadcasted_iota(I32, (W, W), 0) // C) == (lax.broadcasted_iota(I32, (W, W), 1) // C)
    zpad = jnp.zeros((GDN_D - C, GDN_D), F32)
    ng = norm_g[...]
    heads = [slice(h * GDN_D, (h + 1) * GDN_D) for h in range(GDN_H)]
    lanes = lambda parts: jnp.concatenate(parts, axis=1)

    def block_diag(m):
        return jnp.where(blockmask, jnp.concatenate([m] * GDN_H, axis=0), 0.0)

    st = []
    for b in range(nb):
        q = [qn[b, :, sl] for sl in heads]
        k = [kn[b, :, sl] for sl in heads]
        beta = [betab[b, :, sl] for sl in heads]
        gc = [_cumsum_rows(gb[b, :, sl]) for sl in heads]
        kbeta = [k[h] * beta[h] for h in range(GDN_H)]
        gcol = lanes([g[:, 0:C] for g in gc])
        grow = lanes([jnp.concatenate([g, g], axis=0).T[0:C, 0:C] for g in gc])
        decay = jnp.where(lower, jnp.exp(jnp.where(lower, gcol - grow, 0.0)), 0.0)
        kk, qk = [], []
        for h in range(GDN_H):
            r = _dot_nt(jnp.concatenate([kbeta[h], q[h]], axis=0).astype(BF), k[h].astype(BF))
            kk.append(r[:C])
            qk.append(r[C:])
        a = jnp.where(strict, lanes(kk) * decay, 0.0)
        intra = lanes(qk) * decay
        rhs = jnp.concatenate(
            [lanes([vv[b, :, heads[h]] * beta[h], kbeta[h] * jnp.exp(gc[h])]) for h in range(GDN_H)], axis=0)
        st.append(dict(q=q, k=k, gc=gc, intra=intra, rhs=rhs, nm=-a))

    for d in st:
        d["x"] = eye + d["nm"]
        d["p"] = _dot3(d["nm"], block_diag(d["nm"]))
    step = 4
    while step < C:
        for d in st:
            r = _dot3(jnp.concatenate([d["x"], d["p"]], axis=0), block_diag(d["p"]))
            d["x"] = d["x"] + r[:C]
            d["p"] = r[C:]
        step *= 2
    for d in st:
        tinv = d["x"] + _dot3(d["x"], block_diag(d["p"]))
        d["uw"] = _dot3(block_diag(tinv), d["rhs"])

    for b in range(nb):
        d = st[b]
        for h in range(GDN_H):
            u = d["uw"][h * C:(h + 1) * C, 0:GDN_D]
            w = d["uw"][h * C:(h + 1) * C, GDN_D:]
            gc = d["gc"][h]
            s = s_ref[b, h]
            r = _dot(jnp.concatenate([w, d["q"][h] * jnp.exp(gc)], axis=0).astype(BF), s.astype(BF))
            v_new = u - r[:C]
            o = r[C:] + _dot(d["intra"][:, h * C:(h + 1) * C].astype(BF), v_new.astype(BF))
            g_last = gc[C - 1:C, :]
            kd = d["k"][h] * jnp.exp(g_last - gc)
            kdt = jnp.concatenate([kd, zpad], axis=0).T.astype(BF)
            vnp = jnp.concatenate([v_new, zpad], axis=0).astype(BF)
            s_ref[b, h] = s * jnp.exp(g_last) + _dot(kdt, vnp)
            o_ref[b, :, heads[h]] = _gated_rmsnorm(o, z[b, :, heads[h]], ng).astype(BF)

    @pl.when(c == pl.num_programs(0) - 1)
    def _():
        s_out[...] = s_ref[...]


def _gdn_chunked(qn, kn, vv, gb, betab, h3, norm_g):
    b, t, _ = qn.shape
    C = GDN_CHUNK
    blk = pl.BlockSpec((b, C, GDN_W), lambda c: (0, c, 0))
    return pl.pallas_call(
        functools.partial(_gdn_chunk_kernel, nb=b),
        grid=(t // C,),
        in_specs=[blk, blk, blk, blk, blk,
                  pl.BlockSpec((b, C, GDN_W), lambda c: (0, c, C_GZ // GDN_W)),
                  pl.BlockSpec((1, GDN_D), lambda c: (0, 0))],
        out_specs=[pl.BlockSpec((b, C, GDN_W), lambda c: (0, c, 0)),
                   pl.BlockSpec((b, GDN_H, GDN_D, GDN_D), lambda c: (0, 0, 0, 0))],
        out_shape=[jax.ShapeDtypeStruct((b, t, GDN_W), BF),
                   jax.ShapeDtypeStruct((b, GDN_H, GDN_D, GDN_D), F32)],
        scratch_shapes=[pltpu.VMEM((b, GDN_H, GDN_D, GDN_D), F32)],
        compiler_params=_cp("arbitrary"),
        name="gdn_chunked",
    )(qn, kn, vv, gb, betab, h3, norm_g)


def _layer_norm(y, g, b):
    mu = jnp.mean(y, -1, keepdims=True)
    yc = y - mu
    var = jnp.mean(yc * yc, -1, keepdims=True)
    return yc * lax.rsqrt(var + LN_EPS) * g + b


def _out_proj_kernel(attn, gdn, scm, x, wo, g, b, o_ref):
    y = _dot(attn[...], wo[0:ATTN_W, :])
    y = y + _dot(gdn[...], wo[ATTN_W:ATTN_W + GDN_W, :])
    y = y + _dot(scm[...], wo[ATTN_W + GDN_W:, :])
    o_ref[...] = _layer_norm(DN_ALPHA * x[...] + y, g[...], b[...])


def _out_proj(attn, gdn, scm, x, wo, layer, g, b, tm):
    m = x.shape[0]
    row = lambda w: pl.BlockSpec((tm, w), lambda i: (i, 0))
    full = lambda a: pl.BlockSpec((None,) + a.shape[1:], lambda i: (layer, 0, 0))
    return pl.pallas_call(
        _out_proj_kernel,
        grid=(m // tm,),
        in_specs=[row(ATTN_W), row(GDN_W), row(SC_W), row(D_MODEL), full(wo), full(g), full(b)],
        out_specs=row(D_MODEL),
        out_shape=jax.ShapeDtypeStruct((m, D_MODEL), F32),
        compiler_params=_cp("parallel"),
        name="out_proj_ln",
    )(attn, gdn, scm, x, wo, g, b)


def _ffn_kernel(h, wg, wu, wd, g, b, o_ref, hb, acc):
    f = pl.program_id(1)

    @pl.when(f == 0)
    def _():
        hb[...] = h[...].astype(BF)
        acc[...] = jnp.zeros_like(acc)

    x = hb[...]
    gate = _dot(x, wg[...])
    hid = gate * _sigmoid(gate) * _dot(x, wu[...])
    acc[...] += _dot(hid.astype(BF), wd[...])

    @pl.when(f == pl.num_programs(1) - 1)
    def _():
        o_ref[...] = _layer_norm(DN_ALPHA * h[...] + acc[...], g[...], b[...])


def _ffn(h, wg, wu, wd, layer, g, b, tm, tf):
    m = h.shape[0]
    full = lambda a: pl.BlockSpec((None,) + a.shape[1:], lambda i, f: (layer, 0, 0))
    return pl.pallas_call(
        _ffn_kernel,
        grid=(m // tm, FFN_DIM // tf),
        in_specs=[pl.BlockSpec((tm, D_MODEL), lambda i, f: (i, 0)),
                  pl.BlockSpec((None, D_MODEL, tf), lambda i, f: (layer, 0, f)),
                  pl.BlockSpec((None, D_MODEL, tf), lambda i, f: (layer, 0, f)),
                  pl.BlockSpec((None, tf, D_MODEL), lambda i, f: (layer, f, 0)),
                  full(g), full(b)],
        out_specs=pl.BlockSpec((tm, D_MODEL), lambda i, f: (i, 0)),
        out_shape=jax.ShapeDtypeStruct((m, D_MODEL), F32),
        scratch_shapes=[pltpu.VMEM((tm, D_MODEL), BF), pltpu.VMEM((tm, D_MODEL), F32)],
        compiler_params=_cp("parallel", "arbitrary"),
        name="ffn_ln",
    )(h, wg, wu, wd, g, b)


def _head_rows(v):
    return jnp.repeat(v.astype(F32), GDN_D)[None, :]


def _prompt_layer(xp, tabs, lw, stacks, tm_in=1024, tn_in=768, tr=256, tm_out=512, tm_ffn=512, tf=512):
    b, t, _ = xp.shape
    x2 = xp.reshape(b * t, D_MODEL)
    layer = lw["layer"]
    h = _matmul(x2, lw["w_in"], layer, min(tm_in, b * t), tn_in)
    h3 = h.reshape(b, t, IN_PACKED)
    k_st, v_st, ik_st, kb, vt, qt, iqt, ikb, iwt = _attn_prep(h3, tabs, tr, layer, *stacks)
    attn = _dsa_prompt(qt, iqt, iwt, kb, vt, ikb, tr)
    qn, kn, vv, gb, betab, scm, utail = _mix_prep(h3, lw["gdn_conv_w"], lw["sc_conv_w"],
                                                 lw["alog"], lw["dtb"], tr)
    gdn, s_new = _gdn_chunked(qn, kn, vv, gb, betab, h3, lw["norm_g"])
    h1 = _out_proj(attn.reshape(b * t, ATTN_W), gdn.reshape(b * t, GDN_W), scm.reshape(b * t, SC_W),
                   x2, lw["w_out"], layer, lw["ln1_g"], lw["ln1_b"], min(tm_out, b * t))
    y = _ffn(h1, lw["w_gate"], lw["w_up"], lw["w_down"], layer, lw["ln2_g"], lw["ln2_b"],
             min(tm_ffn, b * t), tf)
    outs = dict(
        stacks=(k_st, v_st, ik_st),
        s=s_new,
        gconv=h3[:, t - (GDN_CONV - 1):, C_GQKV:C_GQKV + GDN_CONV_DIM],
        sconv=utail[:, 8 - (SC_CONV - 1):, :],
        attn=attn, gdn=gdn, scm=scm,
    )
    return y.reshape(b, t, D_MODEL), outs


def _sample_prep_kernel(h, c128, s128, c64, s64, gst, sst, s_in, gw, sw, alog, dtb, norm_g,
                        q_o, k_o, iq_o, ik_o, iw_o, gst_o, sst_o, s_o, gdn_o, scm_o, o_buf, *, nb):
    cos = c128[...]
    sin = s128[...]
    for hh in range(N_Q):
        sl = slice(hh * HEAD_DIM, (hh + 1) * HEAD_DIM)
        q_o[:, sl] = _rope128(h[:, C_AQ + hh * HEAD_DIM:C_AQ + (hh + 1) * HEAD_DIM], cos, sin) * (HEAD_DIM ** -0.5)
    for hh in range(N_KV):
        sl = slice(hh * HEAD_DIM, (hh + 1) * HEAD_DIM)
        k_o[:, sl] = _rope128(h[:, C_AK + hh * HEAD_DIM:C_AK + (hh + 1) * HEAD_DIM], cos, sin)
    cos = c64[...]
    sin = s64[...]
    for j in range(IDX_H * IDX_D // 128):
        sl = slice(j * 128, (j + 1) * 128)
        iq_o[:, sl] = _rope64x2(h[:, C_IQ + j * 128:C_IQ + (j + 1) * 128], cos, sin) * (IDX_D ** -0.5)
    small = h[:, C_SM:C_SM + 128]
    ik_o[...] = _rope64x2(small, cos, sin)[:, SM_IK:SM_IK + IDX_D]
    iw_o[...] = small[:, SM_IW:SM_IW + IDX_H] * (IDX_H ** -0.5)

    gq = h[:, C_GQKV:C_GQKV + GDN_CONV_DIM]
    y = gq * gw[GDN_CONV - 1:GDN_CONV, :]
    for j in range(GDN_CONV - 1):
        y = y + gst[j] * gw[j:j + 1, :]
    for j in range(GDN_CONV - 2):
        gst_o[j] = gst[j + 1]
    gst_o[GDN_CONV - 2] = gq
    qs, ks, v, gs, bs = _gdn_activations(y, small, alog[...], dtb[...])
    for b in range(nb):
        for hh in range(GDN_H):
            sl = slice(hh * GDN_D, (hh + 1) * GDN_D)
            kc = jnp.broadcast_to(ks[hh][b:b + 1, :], (GDN_D, GDN_D)).T
            qc = jnp.broadcast_to(qs[hh][b:b + 1, :], (GDN_D, GDN_D)).T
            s = s_in[b, hh] * jnp.exp(gs[hh][b:b + 1, :])
            ks_row = jnp.sum(kc * s, axis=0, keepdims=True)
            delta = (v[b:b + 1, sl] - ks_row) * bs[hh][b:b + 1, :]
            s = s + kc * delta
            s_o[b, hh] = s
            o_buf[b:b + 1, sl] = jnp.sum(qc * s, axis=0, keepdims=True)
    ng = norm_g[...]
    for hh in range(GDN_H):
        sl = slice(hh * GDN_D, (hh + 1) * GDN_D)
        z = h[:, C_GZ + hh * GDN_D:C_GZ + (hh + 1) * GDN_D]
        gdn_o[:, sl] = _gated_rmsnorm(o_buf[:, sl], z, ng).astype(BF)

    u = h[:, C_SC:C_SC + SC_W] * h[:, C_SX:C_SX + SC_W]
    y = u * sw[SC_CONV - 1:SC_CONV, :]
    for j in range(SC_CONV - 1):
        y = y + sst[j] * sw[j:j + 1, :]
    for j in range(SC_CONV - 2):
        sst_o[j] = sst[j + 1]
    sst_o[SC_CONV - 2] = u
    scm_o[...] = (h[:, C_SB:C_SB + SC_W] * y).astype(BF)


def _sample_prep(h, tabs, gst, sst, s_in, gw, sw, alog, dtb, norm_g):
    nb = h.shape[0]
    sds = jax.ShapeDtypeStruct
    return pl.pallas_call(
        functools.partial(_sample_prep_kernel, nb=nb),
        out_shape=[sds((nb, ATTN_W), F32), sds((nb, KV_W), F32), sds((nb, IDX_H * IDX_D), F32),
                   sds((nb, IDX_D), F32), sds((nb, IDX_H), F32),
                   sds(gst.shape, F32), sds(sst.shape, F32), sds(s_in.shape, F32),
                   sds((nb, GDN_W), BF), sds((nb, SC_W), BF)],
        scratch_shapes=[pltpu.VMEM((nb, GDN_W), F32)],
        compiler_params=pltpu.CompilerParams(vmem_limit_bytes=VMEM_LIMIT),
        name="sample_prep",
    )(h, *tabs, gst, sst, s_in, gw, sw, alog, dtb, norm_g)


def _sample_index_kernel(pt, *refs, pps, topk):
    pages = refs[:pps]
    iq, iw, ikn, expand, bias_o, seln_o, keys = refs[pps:]
    s = pl.program_id(1)
    iqb = iq[...]
    w = jnp.broadcast_to(iw[...], (IDX_H, PAGE))
    for j in range(pps):
        sc = jnp.maximum(_dot(iqb, pages[j][...].astype(BF)), 0.0)
        keys[pl.ds(s * pps + j, 1), :] = _sortable(jnp.sum(sc * w, axis=0, keepdims=True))

    @pl.when(s == pl.num_programs(1) - 1)
    def _():
        prod = iqb.astype(F32) * ikn[...].astype(BF).astype(F32)
        sn = jnp.sum(prod, axis=1, keepdims=True)
        sn = jnp.sum(jnp.maximum(sn, 0.0) * iw[...], axis=0, keepdims=True)
        key_new = jnp.broadcast_to(_sortable(sn), (1, PAGE))
        kall = keys[...]

        def count_ge(cand):
            c = jnp.where(kall >= cand, 1.0, 0.0).sum(axis=0, keepdims=True).sum(axis=1, keepdims=True)
            return jnp.broadcast_to(c, (1, PAGE)) + jnp.where(key_new >= cand, 1.0, 0.0)

        theta = _kth_largest(count_ge, (1, PAGE), float(topk))
        sel = _dot(jnp.where(kall >= theta, 1.0, 0.0).astype(BF), expand[...])
        bias_o[...] = jnp.where(sel > 0.5, 0.0, MASKED)
        seln_o[...] = jnp.where(key_new >= theta, 0.0, MASKED)


def _expand_matrix():
    return jnp.asarray(np.arange(PAGE)[:, None] == (np.arange(PAGE * N_KV)[None, :] // N_KV), BF)


def _sample_index(page_table, kidx_t, layer, iq3, iw3, ikn3, pps=16):
    nb, npages = page_table.shape
    topk = min(TOPK_MAX, (npages * PAGE + 1) // 4)
    page = lambda j: pl.BlockSpec((None, None, IDX_D, PAGE),
                                  lambda b, s, pt, j=j: (layer, pt[b, s * pps + j], 0, 0))
    per_b = lambda shp: pl.BlockSpec((None,) + shp, lambda b, s, pt: (b, 0, 0))
    grid_spec = pltpu.PrefetchScalarGridSpec(
        num_scalar_prefetch=1,
        grid=(nb, npages // pps),
        in_specs=[page(j) for j in range(pps)]
                 + [per_b((IDX_H, IDX_D)), per_b((IDX_H, 1)), per_b((1, IDX_D)),
                    pl.BlockSpec((PAGE, PAGE * N_KV), lambda b, s, pt: (0, 0))],
        out_specs=[per_b((npages, PAGE * N_KV)), per_b((1, PAGE))],
        scratch_shapes=[pltpu.VMEM((npages, PAGE), I32)],
    )
    return pl.pallas_call(
        functools.partial(_sample_index_kernel, pps=pps, topk=topk),
        grid_spec=grid_spec,
        out_shape=[jax.ShapeDtypeStruct((nb, npages, PAGE * N_KV), F32), jax.ShapeDtypeStruct((nb, 1, PAGE), F32)],
        compiler_params=_cp("parallel", "arbitrary"),
        name="sample_index",
    )(page_table, *([kidx_t] * pps), iq3, iw3, ikn3, _expand_matrix())


def _sample_attn_kernel(pt, *refs, pps):
    kp = refs[:pps]
    vp = refs[pps:2 * pps]
    q8, bias, seln, kn8, vn8, o_ref, m_ref, l_ref, acc_ref = refs[2 * pps:]
    s = pl.program_id(1)
    rows = PAGE * N_KV

    @pl.when(s == 0)
    def _():
        m_ref[...] = jnp.full_like(m_ref, NEG_INF)
        l_ref[...] = jnp.zeros_like(l_ref)
        acc_ref[...] = jnp.zeros_like(acc_ref)

    q = q8[...]
    lane_head = lax.broadcasted_iota(I32, (N_Q, rows), 1) % N_KV
    own = lane_head == lax.broadcasted_iota(I32, (N_Q, rows), 0) // (N_Q // N_KV)
    head_mask = jnp.where(own, 0.0, MASKED)
    lg = [_dot_nt(q, kp[j][...].astype(BF)) + (bias[pl.ds(s * pps + j, 1), :] + head_mask) for j in range(pps)]
    m_old = m_ref[...]
    m_new = m_old
    for x in lg:
        m_new = jnp.maximum(m_new, jnp.max(x, axis=1, keepdims=True))
    alpha = jnp.exp(m_old - m_new)
    l_new = l_ref[...] * alpha
    acc = acc_ref[...] * alpha
    for j in range(pps):
        p = jnp.exp(lg[j] - m_new)
        l_new = l_new + jnp.sum(p, axis=1, keepdims=True)
        acc = acc + _dot(p.astype(BF), vp[j][...].astype(BF))
    m_ref[...] = m_new
    l_ref[...] = l_new
    acc_ref[...] = acc

    @pl.when(s == pl.num_programs(1) - 1)
    def _():
        knb = kn8[...].astype(BF).astype(F32)
        vnb = vn8[...].astype(BF).astype(F32)
        x = jnp.sum(q.astype(F32) * knb, axis=1, keepdims=True) + seln[:, 0:1]
        m_fin = jnp.maximum(m_new, x)
        a = jnp.exp(m_new - m_fin)
        p = jnp.exp(x - m_fin)
        l_fin = l_new * a + p
        o_ref[...] = ((acc * a + p.astype(BF).astype(F32) * vnb) / l_fin).astype(BF)


def _sample_attn(page_table, ck2, cv2, layer, q8, bias, seln, kn8, vn8, pps=8):
    nb, npages = page_table.shape
    rows = PAGE * N_KV
    page = lambda j: pl.BlockSpec((None, None, rows, HEAD_DIM),
                                  lambda b, s, pt, j=j: (layer, pt[b, s * pps + j], 0, 0))
    per_b = lambda shp: pl.BlockSpec((None,) + shp, lambda b, s, pt: (b, 0, 0))
    grid_spec = pltpu.PrefetchScalarGridSpec(
        num_scalar_prefetch=1,
        grid=(nb, npages // pps),
        in_specs=[page(j) for j in range(pps)] * 2
                 + [per_b((N_Q, HEAD_DIM)), per_b((npages, rows)), per_b((1, PAGE)),
                    per_b((N_Q, HEAD_DIM)), per_b((N_Q, HEAD_DIM))],
        out_specs=per_b((N_Q, HEAD_DIM)),
        scratch_shapes=[pltpu.VMEM((N_Q, 1), F32), pltpu.VMEM((N_Q, 1), F32), pltpu.VMEM((N_Q, HEAD_DIM), F32)],
    )
    return pl.pallas_call(
        functools.partial(_sample_attn_kernel, pps=pps),
        grid_spec=grid_spec,
        out_shape=jax.ShapeDtypeStruct((nb, N_Q, HEAD_DIM), BF),
        compiler_params=_cp("parallel", "arbitrary"),
        name="sample_attn",
    )(page_table, *([ck2] * pps), *([cv2] * pps), q8, bias, seln, kn8, vn8)


SAMPLE_ROWS = 16


def _sample_layer(xs, tabs, lw, layer, page_table, cache_k, cache_v, cache_kidx, gst, sst, s_in):
    nb = page_table.shape[0]
    h = _matmul(xs, lw["w_in"], layer, SAMPLE_ROWS, 768)[:nb]
    q, k, iq, ik, iw, gst_n, sst_n, s_n, gdn, scm = _sample_prep(
        h, tabs, gst, sst, s_in, lw["gdn_conv_w"], lw["sc_conv_w"], lw["alog"], lw["dtb"], lw["norm_g"])
    v = h[:, C_AV:C_AV + KV_W]
    bias, seln = _sample_index(page_table, cache_kidx, layer, iq.reshape(nb, IDX_H, IDX_D).astype(BF),
                               iw.reshape(nb, IDX_H, 1), ik.reshape(nb, 1, IDX_D))
    own_kv = lambda a: jnp.repeat(a.reshape(nb, N_KV, HEAD_DIM), N_Q // N_KV, axis=1)
    attn = _sample_attn(page_table, cache_k, cache_v, layer, q.reshape(nb, N_Q, HEAD_DIM).astype(BF),
                        bias, seln, own_kv(k), own_kv(v)).reshape(nb, ATTN_W)
    padr = lambda a: jnp.pad(a, ((0, SAMPLE_ROWS - nb), (0, 0)))
    h1 = _out_proj(padr(attn), padr(gdn), padr(scm), xs, lw["w_out"], layer, lw["ln1_g"], lw["ln1_b"],
                   SAMPLE_ROWS)
    y = _ffn(h1, lw["w_gate"], lw["w_up"], lw["w_down"], layer, lw["ln2_g"], lw["ln2_b"], SAMPLE_ROWS, 512)
    outs = dict(k=k.reshape(nb, 1, N_KV, HEAD_DIM), v=v.reshape(nb, 1, N_KV, HEAD_DIM), kidx=ik.reshape(nb, 1, IDX_D),
                s=s_n, gconv=jnp.swapaxes(gst_n, 0, 1), sconv=jnp.swapaxes(sst_n, 0, 1),
                attn=attn, gdn=gdn, scm=scm)
    return y, outs


def kernel(x_prompt, x_sample, cache_k, cache_v, cache_kidx, page_table, state_gdn, state_gdn_conv,
           state_sc_conv, w_in, gdn_conv_w, gdn_a_log, gdn_dt_bias, gdn_norm_g, sc_conv_w, w_out,
           ln1_g, ln1_b, w_gate, w_up, w_down, ln2_g, ln2_b):
    bp, tp, _ = x_prompt.shape
    nb, ts, _ = x_sample.shape
    assert ts == 1
    past = page_table.shape[1] * PAGE
    tabs_p = _rope_tables(jnp.arange(tp))
    tabs_s = _rope_tables(past + jnp.arange(ts))
    w_in_p = _pack_w_in(w_in)
    w_out_b, w_gate_b, w_up_b, w_down_b = (w.astype(BF) for w in (w_out, w_gate, w_up, w_down))
    depth, pool = cache_k.shape[:2]
    cache_k = cache_k.reshape(depth, pool, PAGE * N_KV, HEAD_DIM)
    cache_v = cache_v.reshape(depth, pool, PAGE * N_KV, HEAD_DIM)
    cache_kidx = jnp.swapaxes(cache_kidx, 2, 3)

    xp = x_prompt
    xs = jnp.pad(x_sample.reshape(nb, D_MODEL), ((0, SAMPLE_ROWS - nb), (0, 0)))
    stacks = (jnp.zeros((DEPTH, bp, tp * N_KV, HEAD_DIM), F32), jnp.zeros((DEPTH, bp, tp * N_KV, HEAD_DIM), F32),
              jnp.zeros((DEPTH, bp, IDX_D, tp), F32))
    po, so = [], []
    for l in range(DEPTH):
        lw = dict(layer=l, w_in=w_in_p, gdn_conv_w=gdn_conv_w[l], sc_conv_w=sc_conv_w[l],
                  alog=_head_rows(gdn_a_log[l]), dtb=_head_rows(gdn_dt_bias[l]), norm_g=gdn_norm_g[l][None, :],
                  w_out=w_out_b, ln1_g=ln1_g[:, None, :], ln1_b=ln1_b[:, None, :],
                  w_gate=w_gate_b, w_up=w_up_b, w_down=w_down_b,
                  ln2_g=ln2_g[:, None, :], ln2_b=ln2_b[:, None, :])
        xp, o = _prompt_layer(xp, tabs_p, lw, stacks)
        stacks = o["stacks"]
        po.append(o)
        xs, o = _sample_layer(xs, tabs_s, lw, l, page_table, cache_k, cache_v, cache_kidx,
                              jnp.swapaxes(state_gdn_conv[l], 0, 1), jnp.swapaxes(state_sc_conv[l], 0, 1),
                              state_gdn[l])
        so.append(o)

    st = lambda outs, n: jnp.stack([o[n] for o in outs])
    k_st, v_st, ik_st = stacks
    return (xp, xs[:nb].reshape(nb, ts, D_MODEL),
            k_st.reshape(DEPTH, bp, tp, N_KV, HEAD_DIM), v_st.reshape(DEPTH, bp, tp, N_KV, HEAD_DIM),
            jnp.swapaxes(ik_st, 2, 3), st(so, "k"), st(so, "v"), st(so, "kidx"),
            st(po, "s"), st(so, "s"), st(po, "gconv"), st(so, "gconv"), st(po, "sconv"), st(so, "sconv"))
```

```python
import functools
import math

import numpy as np
import jax
import jax.numpy as jnp
from jax import lax
from jax.experimental import pallas as pl
from jax.experimental.pallas import tpu as pltpu

F32 = jnp.float32
BF = jnp.bfloat16
I32 = jnp.int32

D_MODEL = 2048
DEPTH = 4
PAGE = 128
HEAD_DIM = 128
N_Q = 8
N_KV = 4
ATTN_W = N_Q * HEAD_DIM
KV_W = N_KV * HEAD_DIM
IDX_H = 16
IDX_D = 64
TOPK_MAX = 256
ROPE_THETA = 10000.0
GDN_H = 4
GDN_D = 128
GDN_W = GDN_H * GDN_D
GDN_CONV = 4
GDN_CONV_DIM = 3 * GDN_W
GDN_CHUNK = 64
SC_W = 512
SC_CONV = 3
FFN_DIM = 5632
DN_ALPHA = (2 * DEPTH) ** 0.25
LN_EPS = 1e-5
NORM_EPS = 1e-6
IN_SIZES = (ATTN_W, KV_W, KV_W, IDX_H * IDX_D, IDX_D, IDX_H, GDN_CONV_DIM, GDN_W, GDN_H, GDN_H,
            SC_W, SC_W, SC_W)

C_AQ, C_AK, C_AV, C_IQ, C_GQKV, C_GZ, C_SB, C_SC, C_SX, C_SM = (
    0, 1024, 1536, 2048, 3072, 4608, 5120, 5632, 6144, 6656)
SM_IK, SM_IW, SM_GB, SM_GA = 0, 64, 80, 84
IN_PACKED = 6912

NEG_INF = float("-inf")
MASKED = -1e30
Q_SCALE_LOG2 = HEAD_DIM ** -0.5 * math.log2(math.e)
VT_ROWS = HEAD_DIM + 16
SUB = 64
INT_MIN = -2 ** 31
INT_MAX = 2 ** 31 - 1

VMEM_LIMIT = 56 * 1024 * 1024


def _cp(*sem):
    return pltpu.CompilerParams(dimension_semantics=tuple(sem), vmem_limit_bytes=VMEM_LIMIT)


def _sigmoid(x):
    return 1.0 / (1.0 + jnp.exp(-x))


def _softplus(x):
    return jnp.maximum(x, 0.0) + jnp.log1p(jnp.exp(-jnp.abs(x)))


def _dot(a, b):
    return jnp.dot(a, b, preferred_element_type=F32)


def _dot_hi(a, b):
    return jnp.dot(a, b, preferred_element_type=F32, precision=lax.Precision.HIGHEST)


def _dot_nt(a, b):
    return lax.dot_general(a, b, (((1,), (1,)), ((), ())), preferred_element_type=F32)


_IN_OFF = np.concatenate([[0], np.cumsum(IN_SIZES)]).tolist()
_O_IK, _O_GQKV, _O_GB, _O_SB, _O_END = _IN_OFF[4], _IN_OFF[6], _IN_OFF[8], _IN_OFF[10], _IN_OFF[13]


PACK_COLS = 256
_PB_G, _PB_S, _PB_SM = C_GQKV // PACK_COLS, C_SB // PACK_COLS, C_SM // PACK_COLS
_SH_G, _SH_S = _O_GQKV - C_GQKV, _O_SB - C_SB


def _pack_kernel(a, b, o):
    j = pl.program_id(1)

    def emit(src):
        o[...] = src.T.astype(BF)

    @pl.when(j < _PB_G)
    def _():
        emit(a[...])

    @pl.when((j >= _PB_G) & (j < _PB_S))
    def _():
        emit(jnp.concatenate([a[_SH_G:, :], b[:_SH_G, :]], axis=0))

    @pl.when((j >= _PB_S) & (j < _PB_SM))
    def _():
        emit(jnp.concatenate([a[_SH_S:, :], b[:_SH_S, :]], axis=0))

    @pl.when(j == _PB_SM)
    def _():
        n_idx = _O_GQKV - _O_IK
        n_gdn = _O_SB - _O_GB
        lo = _O_GB - (_O_GB // 128) * 128
        pad = jnp.zeros((PACK_COLS - n_idx - n_gdn, a.shape[1]), F32)
        emit(jnp.concatenate([a[:n_idx, :], b[lo:lo + n_gdn, :], pad], axis=0))


def _pack_w_in(w_in):
    wt = jnp.swapaxes(w_in, 1, 2)
    depth, n, d = wt.shape
    nblk = IN_PACKED // PACK_COLS
    last_b = (n - 1) // 128

    def a_map(l, j):
        return l, jnp.where(j == _PB_SM, _O_IK // PACK_COLS, j), 0

    def b_map(l, j):
        return l, jnp.where(j == _PB_SM, _O_GB // 128, jnp.minimum(2 * j + 2, last_b)), 0

    return pl.pallas_call(
        _pack_kernel,
        grid=(depth, nblk),
        in_specs=[pl.BlockSpec((None, PACK_COLS, d), a_map), pl.BlockSpec((None, 128, d), b_map)],
        out_specs=pl.BlockSpec((None, d, PACK_COLS), lambda l, j: (l, 0, j)),
        out_shape=jax.ShapeDtypeStruct((depth, d, IN_PACKED), BF),
        compiler_params=_cp("parallel", "parallel"),
        name="pack_w_in",
    )(wt, wt)


def _mm_kernel(x_ref, w_ref, o_ref):
    o_ref[...] = _dot(x_ref[...].astype(BF), w_ref[...])


def _matmul(x, w, layer, tm, tn):
    m, k = x.shape
    n = w.shape[2]
    return pl.pallas_call(
        _mm_kernel,
        grid=(m // tm, n // tn),
        in_specs=[pl.BlockSpec((tm, k), lambda i, j: (i, 0)),
                  pl.BlockSpec((None, k, tn), lambda i, j: (layer, 0, j))],
        out_specs=pl.BlockSpec((tm, tn), lambda i, j: (i, j)),
        out_shape=jax.ShapeDtypeStruct((m, n), F32),
        compiler_params=_cp("parallel", "arbitrary"),
        name="proj_in",
    )(x, w)


def _rope_tables(pos):
    def tab(half):
        inv = ROPE_THETA ** (-jnp.arange(half, dtype=F32) / half)
        ang = pos.astype(F32)[:, None] * inv
        return jnp.cos(ang), jnp.sin(ang)
    c, s = tab(HEAD_DIM // 2)
    c128 = jnp.concatenate([c, c], -1)
    s128 = jnp.concatenate([-s, s], -1)
    c, s = tab(IDX_D // 2)
    c64 = jnp.concatenate([c, c, c, c], -1)
    s64 = jnp.concatenate([-s, s, -s, s], -1)
    return c128, s128, c64, s64


def _rope128(x, cos, sin):
    return x * cos + pltpu.roll(x, HEAD_DIM // 2, 1) * sin


def _rope64x2(x, cos, sin):
    lane = lax.broadcasted_iota(I32, x.shape, 1)
    first = (lane % IDX_D) < (IDX_D // 2)
    partner = jnp.where(first, pltpu.roll(x, 128 - IDX_D // 2, 1), pltpu.roll(x, IDX_D // 2, 1))
    return x * cos + partner * sin


def _attn_prep_kernel(aq, ak, av, iq, sm, c128, s128, c64, s64, k_st, v_st, ik_st,
                      ks_o, vs_o, iks_o, kb_o, vt_o, qt_o, iqt_o, ikb_o, iwt_o, *, tr):
    del k_st, v_st, ik_st
    cos = c128[...]
    sin = s128[...]
    for h in range(N_Q):
        sl = slice(h * HEAD_DIM, (h + 1) * HEAD_DIM)
        y = _rope128(aq[:, sl], cos, sin) * Q_SCALE_LOG2
        qt_o[sl, :] = y.T.astype(BF)
    for h in range(N_KV):
        sl = slice(h * HEAD_DIM, (h + 1) * HEAD_DIM)
        y = _rope128(ak[:, sl], cos, sin)
        v = av[:, sl]
        ks_o[pl.ds(h, tr, stride=N_KV), :] = y
        vs_o[pl.ds(h, tr, stride=N_KV), :] = v
        kb_o[:, sl] = y.astype(BF)
        vt_o[h * VT_ROWS:h * VT_ROWS + HEAD_DIM, :] = v.T.astype(BF)
        vt_o[h * VT_ROWS + HEAD_DIM:(h + 1) * VT_ROWS, :] = jnp.ones((VT_ROWS - HEAD_DIM, tr), BF)
    cos = c64[...]
    sin = s64[...]
    for j in range(IDX_H * IDX_D // 128):
        sl = slice(j * 128, (j + 1) * 128)
        y = _rope64x2(iq[:, sl], cos, sin) * (IDX_D ** -0.5)
        iqt_o[sl, :] = y.T.astype(BF)
    x = sm[...]
    y = _rope64x2(x, cos, sin)
    iks_o[...] = y.T[SM_IK:SM_IK + IDX_D, :]
    ikb_o[...] = y[:, SM_IK:SM_IK + IDX_D].astype(BF)
    iwt_o[...] = x.T[SM_IW:SM_IW + IDX_H, :] * (IDX_H ** -0.5)


def _attn_prep(h3, tabs, tr, layer, k_st, v_st, ik_st):
    b, t, _ = h3.shape
    nr = t // tr
    col = lambda w, off: pl.BlockSpec((None, tr, w), lambda bi, r, o=off // w: (bi, r, o))
    tab = pl.BlockSpec((tr, 128), lambda bi, r: (r, 0))
    anyspec = pl.BlockSpec(memory_space=pl.ANY)
    sds = jax.ShapeDtypeStruct
    return pl.pallas_call(
        functools.partial(_attn_prep_kernel, tr=tr),
        grid=(b, nr),
        in_specs=[col(ATTN_W, C_AQ), col(KV_W, C_AK), col(KV_W, C_AV), col(ATTN_W, C_IQ), col(128, C_SM),
                  tab, tab, tab, tab, anyspec, anyspec, anyspec],
        out_specs=[
            pl.BlockSpec((None, None, tr * N_KV, HEAD_DIM), lambda bi, r: (layer, bi, r, 0)),
            pl.BlockSpec((None, None, tr * N_KV, HEAD_DIM), lambda bi, r: (layer, bi, r, 0)),
            pl.BlockSpec((None, None, IDX_D, tr), lambda bi, r: (layer, bi, 0, r)),
            pl.BlockSpec((None, tr, KV_W), lambda bi, r: (bi, r, 0)),
            pl.BlockSpec((None, None, N_KV * VT_ROWS, tr), lambda bi, r: (bi, r, 0, 0)),
            pl.BlockSpec((None, ATTN_W, tr), lambda bi, r: (bi, 0, r)),
            pl.BlockSpec((None, ATTN_W, tr), lambda bi, r: (bi, 0, r)),
            pl.BlockSpec((None, tr, IDX_D), lambda bi, r: (bi, r, 0)),
            pl.BlockSpec((None, IDX_H, tr), lambda bi, r: (bi, 0, r)),
        ],
        out_shape=[
            sds(k_st.shape, F32), sds(v_st.shape, F32), sds(ik_st.shape, F32),
            sds((b, t, KV_W), BF),
            sds((b, nr, N_KV * VT_ROWS, tr), BF),
            sds((b, ATTN_W, t), BF),
            sds((b, ATTN_W, t), BF),
            sds((b, t, IDX_D), BF),
            sds((b, IDX_H, t), F32),
        ],
        input_output_aliases={9: 0, 10: 1, 11: 2},
        compiler_params=_cp("parallel", "parallel"),
        name="attn_prep",
    )(h3, h3, h3, h3, h3, *tabs, k_st, v_st, ik_st)


def _sortable(x):
    bits = pltpu.bitcast(x, I32)
    return jnp.where(bits >= 0, bits, bits ^ jnp.int32(0x7FFFFFFF))


def _kth_largest(count_ge, shape, k):
    lo = jnp.where(count_ge(jnp.zeros(shape, I32)) >= k, 0, INT_MIN).astype(I32)

    def body(it, lo):
        cand = lo + (jnp.int32(1) << (30 - it))
        return jnp.where(count_ge(cand) >= k, cand, lo)

    return lax.fori_loop(0, 31, body, lo)


def _dsa_prompt_kernel(qt, iqt, iwt, kb, vt, ikb, o_ref, keys, bias, m_ref, l_ref, acc_ref, s_ref, *, tq, topk):
    i = pl.program_id(1)
    nch = i + 1
    row = lax.broadcasted_iota(I32, (tq, tq), 0)
    lane = lax.broadcasted_iota(I32, (tq, tq), 1)

    def off_of(c):
        return pl.multiple_of(c * tq, tq)

    def causal_of(c):
        return (c * tq + row) <= (i * tq + lane)

    def fold8(x):
        return x.reshape(tq // 8, 8, tq)

    def indexer(c, carry):
        off = off_of(c)
        ikc = ikb[pl.ds(off, tq), :]
        acc = jnp.zeros((tq, tq), F32)
        for h in range(IDX_H):
            s = _dot(ikc, iqt[h * IDX_D:(h + 1) * IDX_D, :])
            acc = acc + jnp.maximum(s, 0.0) * iwt[h:h + 1, :]
        acc = jnp.where(causal_of(c), acc, NEG_INF)
        keys[pl.ds(off, tq), :] = _sortable(acc)
        return carry

    lax.fori_loop(0, nch, indexer, 0)

    def count_ge(cand):
        def hits(c):
            hit = jnp.where(keys[pl.ds(off_of(c), tq), :] >= cand, 1, 0)
            parts = [hit[r * 8:(r + 1) * 8] for r in range(tq // 8)]
            while len(parts) > 1:
                parts = [parts[j] + parts[j + 1] for j in range(0, len(parts), 2)]
            return parts[0]

        cnt = lax.fori_loop(0, nch // 2, lambda j, cnt: cnt + (hits(2 * j) + hits(2 * j + 1)),
                            jnp.zeros((8, tq), I32))
        cnt = cnt + lax.cond(nch % 2 == 1, lambda: hits(nch - 1), lambda: jnp.zeros((8, tq), I32))
        return cnt.sum(axis=0, keepdims=True)

    theta = _kth_largest(count_ge, (1, tq), topk)
    tied = jnp.max(count_ge(theta)) > topk

    @pl.when(jnp.logical_not(tied))
    def _():
        def make_bias(c, carry):
            off = off_of(c)
            sel = keys[pl.ds(off, tq), :] >= theta
            bias[pl.ds(off, tq), :] = jnp.where(causal_of(c), jnp.where(sel, 0.0, MASKED), MASKED)
            return carry

        lax.fori_loop(0, nch, make_bias, 0)

    @pl.when(tied)
    def _():
        n_gt = jnp.where(theta == INT_MAX, 0, count_ge(jnp.where(theta == INT_MAX, theta, theta + 1)))
        quota = (topk - n_gt).astype(F32)
        before = jnp.where(row > lane, 1.0, 0.0).astype(BF)

        def make_bias(c, seen):
            off = off_of(c)
            kc = keys[pl.ds(off, tq), :]
            tie = jnp.where(kc == theta, 1.0, 0.0)
            rank = seen + _dot(before, tie.astype(BF))
            sel = jnp.where(kc > theta, 1.0, jnp.where(rank < quota, tie, 0.0))
            bias[pl.ds(off, tq), :] = jnp.where(causal_of(c), jnp.where(sel > 0.5, 0.0, MASKED), MASKED)
            return seen + jnp.sum(tie, axis=0, keepdims=True)

        lax.fori_loop(0, nch, make_bias, jnp.zeros((1, tq), F32))

    m_ref[...] = jnp.full_like(m_ref, NEG_INF)
    l_ref[...] = jnp.zeros_like(l_ref)
    acc_ref[...] = jnp.zeros_like(acc_ref)

    def attend(c, carry):
        off = off_of(c)
        m_new = []
        for h in range(N_Q):
            gsl = slice(h // (N_Q // N_KV) * HEAD_DIM, (h // (N_Q // N_KV) + 1) * HEAD_DIM)
            qh = qt[h * HEAD_DIM:(h + 1) * HEAD_DIM, :]
            m8 = jnp.full((8, tq), NEG_INF, F32)
            for sb in range(tq // SUB):
                rows = pl.ds(pl.multiple_of(off + sb * SUB, SUB), SUB)
                s = _dot(kb[rows, gsl], qh) + bias[rows, :]
                s_ref[h, sb * SUB:(sb + 1) * SUB, :] = s
                m8 = jnp.maximum(m8, s.reshape(SUB // 8, 8, tq).max(axis=0))
            m_new.append(jnp.maximum(m_ref[h], m8.max(axis=0, keepdims=True)))
        for h in range(N_Q):
            g = h // (N_Q // N_KV)
            p = jnp.exp2(s_ref[h] - m_new[h]).astype(BF)
            alpha = jnp.exp2(m_ref[h] - m_new[h])
            pv = _dot(vt[c, g * VT_ROWS:(g + 1) * VT_ROWS, :], p)
            acc_ref[h] = acc_ref[h] * alpha + pv[0:HEAD_DIM]
            l_ref[h] = l_ref[h] * alpha + pv[HEAD_DIM:HEAD_DIM + 1]
            m_ref[h] = m_new[h]
        return carry

    lax.fori_loop(0, nch, attend, 0)

    for h in range(N_Q):
        o_ref[:, h * HEAD_DIM:(h + 1) * HEAD_DIM] = (acc_ref[h] * (1.0 / l_ref[h])).T.astype(BF)


def _dsa_prompt(qt, iqt, iwt, kb, vt, ikb, tq):
    b, _, t = qt.shape
    topk = min(TOPK_MAX, t // 4)
    kern = functools.partial(_dsa_prompt_kernel, tq=tq, topk=topk)
    return pl.pallas_call(
        kern,
        grid=(b, t // tq),
        in_specs=[
            pl.BlockSpec((None, ATTN_W, tq), lambda bi, i: (bi, 0, i)),
            pl.BlockSpec((None, ATTN_W, tq), lambda bi, i: (bi, 0, i)),
            pl.BlockSpec((None, IDX_H, tq), lambda bi, i: (bi, 0, i)),
            pl.BlockSpec((None, t, KV_W), lambda bi, i: (bi, 0, 0)),
            pl.BlockSpec((None, t // tq, N_KV * VT_ROWS, tq), lambda bi, i: (bi, 0, 0, 0)),
            pl.BlockSpec((None, t, IDX_D), lambda bi, i: (bi, 0, 0)),
        ],
        out_specs=pl.BlockSpec((None, tq, ATTN_W), lambda bi, i: (bi, i, 0)),
        out_shape=jax.ShapeDtypeStruct((b, t, ATTN_W), BF),
        scratch_shapes=[pltpu.VMEM((t, tq), I32), pltpu.VMEM((t, tq), F32),
                        pltpu.VMEM((N_Q, 1, tq), F32), pltpu.VMEM((N_Q, 1, tq), F32),
                        pltpu.VMEM((N_Q, HEAD_DIM, tq), F32), pltpu.VMEM((N_Q, tq, tq), F32)],
        compiler_params=_cp("parallel", "arbitrary"),
        name="dsa_prompt",
    )(qt, iqt, iwt, kb, vt, ikb)


def _l2norm(x):
    return x * lax.rsqrt(jnp.sum(x * x, -1, keepdims=True) + NORM_EPS)


def _gdn_activations(y, small, alog, dtb):
    y = y * _sigmoid(y)
    qs, ks, gs, bs = [], [], [], []
    r = y.shape[0]
    for h in range(GDN_H):
        sl = slice(h * GDN_D, (h + 1) * GDN_D)
        qs.append(_l2norm(y[:, sl]) * (GDN_D ** -0.5))
        ks.append(_l2norm(y[:, GDN_W + h * GDN_D:GDN_W + (h + 1) * GDN_D]))
        b_raw = jnp.broadcast_to(small[:, SM_GB + h:SM_GB + h + 1], (r, GDN_D))
        a_raw = jnp.broadcast_to(small[:, SM_GA + h:SM_GA + h + 1], (r, GDN_D))
        bs.append(_sigmoid(b_raw))
        gs.append(-jnp.exp(alog[:, sl]) * _softplus(a_raw + dtb[:, sl]))
    return qs, ks, y[:, 2 * GDN_W:], gs, bs


def _mix_prep_kernel(gq, gq_prev, sm, sb, sc, sx, sc_prev, sx_prev, gw, sw, alog, dtb,
                     qn_o, kn_o, vv_o, g_o, beta_o, scm_o, utail_o, xbuf, ubuf, *, tr):
    r = pl.program_id(1)
    first = r == 0
    xbuf[0:8, :] = jnp.where(first, 0.0, gq_prev[...])
    xbuf[8:, :] = gq[...]
    y = xbuf[5:5 + tr, :] * gw[0:1, :]
    for j in range(1, GDN_CONV):
        y = y + xbuf[5 + j:5 + j + tr, :] * gw[j:j + 1, :]
    qs, ks, v, gs, bs = _gdn_activations(y, sm[...], alog[...], dtb[...])
    for h in range(GDN_H):
        sl = slice(h * GDN_D, (h + 1) * GDN_D)
        qn_o[:, sl] = qs[h]
        kn_o[:, sl] = ks[h]
        g_o[:, sl] = gs[h]
        beta_o[:, sl] = bs[h]
    vv_o[...] = v

    u = sc[...] * sx[...]
    ubuf[0:8, :] = jnp.where(first, 0.0, sc_prev[...] * sx_prev[...])
    ubuf[8:, :] = u
    y = ubuf[6:6 + tr, :] * sw[0:1, :]
    for j in range(1, SC_CONV):
        y = y + ubuf[6 + j:6 + j + tr, :] * sw[j:j + 1, :]
    scm_o[...] = (sb[...] * y).astype(BF)

    @pl.when(r == pl.num_programs(1) - 1)
    def _():
        utail_o[...] = u[tr - 8:, :]


def _mix_prep(h3, gw, sw, alog, dtb, tr):
    b, t, _ = h3.shape
    nr = t // tr
    col = lambda w, off: pl.BlockSpec((None, tr, w), lambda bi, r, o=off // w: (bi, r, o))
    prev = lambda w, off: pl.BlockSpec(
        (None, 8, w), lambda bi, r, o=off // w: (bi, jnp.maximum(r * (tr // 8) - 1, 0), o))
    full = lambda a: pl.BlockSpec(a.shape, lambda bi, r: (0, 0))
    out = pl.BlockSpec((None, tr, GDN_W), lambda bi, r: (bi, r, 0))
    act = jax.ShapeDtypeStruct((b, t, GDN_W), F32)
    return pl.pallas_call(
        functools.partial(_mix_prep_kernel, tr=tr),
        grid=(b, nr),
        in_specs=[col(GDN_CONV_DIM, C_GQKV), prev(GDN_CONV_DIM, C_GQKV), col(128, C_SM),
                  col(SC_W, C_SB), col(SC_W, C_SC), col(SC_W, C_SX), prev(SC_W, C_SC), prev(SC_W, C_SX),
                  full(gw), full(sw), full(alog), full(dtb)],
        out_specs=[out, out, out, out, out, out,
                   pl.BlockSpec((None, 8, SC_W), lambda bi, r: (bi, 0, 0))],
        out_shape=[act, act, act, act, act,
                   jax.ShapeDtypeStruct((b, t, SC_W), BF),
                   jax.ShapeDtypeStruct((b, 8, SC_W), F32)],
        scratch_shapes=[pltpu.VMEM((tr + 8, GDN_CONV_DIM), F32), pltpu.VMEM((tr + 8, SC_W), F32)],
        compiler_params=_cp("parallel", "arbitrary"),
        name="mix_prep",
    )(h3, h3, h3, h3, h3, h3, h3, h3, gw, sw, alog, dtb)


def _cumsum_rows(x):
    row = lax.broadcasted_iota(I32, x.shape, 0)
    s = 1
    while s < x.shape[0]:
        x = x + jnp.where(row >= s, pltpu.roll(x, s, 0), 0.0)
        s *= 2
    return x


def _split_bf(x):
    hi = x.astype(BF)
    return hi, (x - hi.astype(F32)).astype(BF)


def _dot3(a, b):
    m = a.shape[0]
    ah, al = _split_bf(a)
    bh, bl = _split_bf(b)
    r = _dot(jnp.concatenate([ah, al], axis=0), bh)
    return r[:m] + r[m:] + _dot(ah, bl)


def _gated_rmsnorm(o, z, norm_g):
    on = o * lax.rsqrt(jnp.mean(o * o, -1, keepdims=True) + NORM_EPS) * norm_g
    return on * (z * _sigmoid(z))


def _gdn_chunk_kernel(qn, kn, vv, gb, betab, z, norm_g, o_ref, s_out, s_ref, *, nb):
    c = pl.program_id(0)
    C = GDN_CHUNK
    W = GDN_H * C

    @pl.when(c == 0)
    def _():
        s_ref[...] = jnp.zeros_like(s_ref)

    ii = lax.broadcasted_iota(I32, (C, W), 0)
    jj = lax.broadcasted_iota(I32, (C, W), 1) % C
    lower = ii >= jj
    strict = ii > jj
    eye = jnp.where(ii == jj, 1.0, 0.0).astype(F32)
    blockmask = (lax.broadcasted_iota(I32, (W, W), 0) // C) == (lax.broadcasted_iota(I32, (W, W), 1) // C)
    zpad = jnp.zeros((GDN_D - C, GDN_D), F32)
    ng = norm_g[...]
    heads = [slice(h * GDN_D, (h + 1) * GDN_D) for h in range(GDN_H)]
    lanes = lambda parts: jnp.concatenate(parts, axis=1)

    def block_diag(m):
        return jnp.where(blockmask, jnp.concatenate([m] * GDN_H, axis=0), 0.0)

    st = []
    for b in range(nb):
        q = [qn[b, :, sl] for sl in heads]
        k = [kn[b, :, sl] for sl in heads]
        beta = [betab[b, :, sl] for sl in heads]
        gc = [_cumsum_rows(gb[b, :, sl]) for sl in heads]
        kbeta = [k[h] * beta[h] for h in range(GDN_H)]
        gcol = lanes([g[:, 0:C] for g in gc])
        grow = lanes([jnp.concatenate([g, g], axis=0).T[0:C, 0:C] for g in gc])
        decay = jnp.where(lower, jnp.exp(jnp.where(lower, gcol - grow, 0.0)), 0.0)
        kk, qk = [], []
        for h in range(GDN_H):
            r = _dot_nt(jnp.concatenate([kbeta[h], q[h]], axis=0).astype(BF), k[h].astype(BF))
            kk.append(r[:C])
            qk.append(r[C:])
        a = jnp.where(strict, lanes(kk) * decay, 0.0)
        intra = lanes(qk) * decay
        rhs = jnp.concatenate(
            [lanes([vv[b, :, heads[h]] * beta[h], kbeta[h] * jnp.exp(gc[h])]) for h in range(GDN_H)], axis=0)
        st.append(dict(q=q, k=k, gc=gc, intra=intra, rhs=rhs, nm=-a))

    for d in st:
        d["x"] = eye + d["nm"]
        d["p"] = _dot3(d["nm"], block_diag(d["nm"]))
    step = 4
    while step < C:
        for d in st:
            r = _dot3(jnp.concatenate([d["x"], d["p"]], axis=0), block_diag(d["p"]))
            d["x"] = d["x"] + r[:C]
            d["p"] = r[C:]
        step *= 2
    for d in st:
        tinv = d["x"] + _dot3(d["x"], block_diag(d["p"]))
        d["uw"] = _dot3(block_diag(tinv), d["rhs"])

    for b in range(nb):
        d = st[b]
        for h in range(GDN_H):
            u = d["uw"][h * C:(h + 1) * C, 0:GDN_D]
            w = d["uw"][h * C:(h + 1) * C, GDN_D:]
            gc = d["gc"][h]
            s = s_ref[b, h]
            r = _dot(jnp.concatenate([w, d["q"][h] * jnp.exp(gc)], axis=0).astype(BF), s.astype(BF))
            v_new = u - r[:C]
            o = r[C:] + _dot(d["intra"][:, h * C:(h + 1) * C].astype(BF), v_new.astype(BF))
            g_last = gc[C - 1:C, :]
            kd = d["k"][h] * jnp.exp(g_last - gc)
            kdt = jnp.concatenate([kd, zpad], axis=0).T.astype(BF)
            vnp = jnp.concatenate([v_new, zpad], axis=0).astype(BF)
            s_ref[b, h] = s * jnp.exp(g_last) + _dot(kdt, vnp)
            o_ref[b, :, heads[h]] = _gated_rmsnorm(o, z[b, :, heads[h]], ng).astype(BF)

    @pl.when(c == pl.num_programs(0) - 1)
    def _():
        s_out[...] = s_ref[...]


def _gdn_chunked(qn, kn, vv, gb, betab, h3, norm_g):
    b, t, _ = qn.shape
    C = GDN_CHUNK
    blk = pl.BlockSpec((b, C, GDN_W), lambda c: (0, c, 0))
    return pl.pallas_call(
        functools.partial(_gdn_chunk_kernel, nb=b),
        grid=(t // C,),
        in_specs=[blk, blk, blk, blk, blk,
                  pl.BlockSpec((b, C, GDN_W), lambda c: (0, c, C_GZ // GDN_W)),
                  pl.BlockSpec((1, GDN_D), lambda c: (0, 0))],
        out_specs=[pl.BlockSpec((b, C, GDN_W), lambda c: (0, c, 0)),
                   pl.BlockSpec((b, GDN_H, GDN_D, GDN_D), lambda c: (0, 0, 0, 0))],
        out_shape=[jax.ShapeDtypeStruct((b, t, GDN_W), BF),
                   jax.ShapeDtypeStruct((b, GDN_H, GDN_D, GDN_D), F32)],
        scratch_shapes=[pltpu.VMEM((b, GDN_H, GDN_D, GDN_D), F32)],
        compiler_params=_cp("arbitrary"),
        name="gdn_chunked",
    )(qn, kn, vv, gb, betab, h3, norm_g)


def _layer_norm(y, g, b):
    mu = jnp.mean(y, -1, keepdims=True)
    yc = y - mu
    var = jnp.mean(yc * yc, -1, keepdims=True)
    return yc * lax.rsqrt(var + LN_EPS) * g + b


def _out_proj_kernel(attn, gdn, scm, x, wo, g, b, o_ref):
    y = _dot(attn[...], wo[0:ATTN_W, :])
    y = y + _dot(gdn[...], wo[ATTN_W:ATTN_W + GDN_W, :])
    y = y + _dot(scm[...], wo[ATTN_W + GDN_W:, :])
    o_ref[...] = _layer_norm(DN_ALPHA * x[...] + y, g[...], b[...])


def _out_proj(attn, gdn, scm, x, wo, layer, g, b, tm):
    m = x.shape[0]
    row = lambda w: pl.BlockSpec((tm, w), lambda i: (i, 0))
    full = lambda a: pl.BlockSpec((None,) + a.shape[1:], lambda i: (layer, 0, 0))
    return pl.pallas_call(
        _out_proj_kernel,
        grid=(m // tm,),
        in_specs=[row(ATTN_W), row(GDN_W), row(SC_W), row(D_MODEL), full(wo), full(g), full(b)],
        out_specs=row(D_MODEL),
        out_shape=jax.ShapeDtypeStruct((m, D_MODEL), F32),
        compiler_params=_cp("parallel"),
        name="out_proj_ln",
    )(attn, gdn, scm, x, wo, g, b)


def _ffn_kernel(h, wg, wu, wd, g, b, o_ref, hb, acc):
    f = pl.program_id(1)

    @pl.when(f == 0)
    def _():
        hb[...] = h[...].astype(BF)
        acc[...] = jnp.zeros_like(acc)

    x = hb[...]
    gate = _dot(x, wg[...])
    hid = gate * _sigmoid(gate) * _dot(x, wu[...])
    acc[...] += _dot(hid.astype(BF), wd[...])

    @pl.when(f == pl.num_programs(1) - 1)
    def _():
        o_ref[...] = _layer_norm(DN_ALPHA * h[...] + acc[...], g[...], b[...])


def _ffn(h, wg, wu, wd, layer, g, b, tm, tf):
    m = h.shape[0]
    full = lambda a: pl.BlockSpec((None,) + a.shape[1:], lambda i, f: (layer, 0, 0))
    return pl.pallas_call(
        _ffn_kernel,
        grid=(m // tm, FFN_DIM // tf),
        in_specs=[pl.BlockSpec((tm, D_MODEL), lambda i, f: (i, 0)),
                  pl.BlockSpec((None, D_MODEL, tf), lambda i, f: (layer, 0, f)),
                  pl.BlockSpec((None, D_MODEL, tf), lambda i, f: (layer, 0, f)),
                  pl.BlockSpec((None, tf, D_MODEL), lambda i, f: (layer, f, 0)),
                  full(g), full(b)],
        out_specs=pl.BlockSpec((tm, D_MODEL), lambda i, f: (i, 0)),
        out_shape=jax.ShapeDtypeStruct((m, D_MODEL), F32),
        scratch_shapes=[pltpu.VMEM((tm, D_MODEL), BF), pltpu.VMEM((tm, D_MODEL), F32)],
        compiler_params=_cp("parallel", "arbitrary"),
        name="ffn_ln",
    )(h, wg, wu, wd, g, b)


def _head_rows(v):
    return jnp.repeat(v.astype(F32), GDN_D)[None, :]


def _prompt_layer(xp, tabs, lw, stacks, tm_in=1024, tn_in=768, tr=256, tm_out=512, tm_ffn=512, tf=512):
    b, t, _ = xp.shape
    x2 = xp.reshape(b * t, D_MODEL)
    layer = lw["layer"]
    h = _matmul(x2, lw["w_in"], layer, min(tm_in, b * t), tn_in)
    h3 = h.reshape(b, t, IN_PACKED)
    k_st, v_st, ik_st, kb, vt, qt, iqt, ikb, iwt = _attn_prep(h3, tabs, tr, layer, *stacks)
    attn = _dsa_prompt(qt, iqt, iwt, kb, vt, ikb, tr)
    qn, kn, vv, gb, betab, scm, utail = _mix_prep(h3, lw["gdn_conv_w"], lw["sc_conv_w"],
                                                 lw["alog"], lw["dtb"], tr)
    gdn, s_new = _gdn_chunked(qn, kn, vv, gb, betab, h3, lw["norm_g"])
    h1 = _out_proj(attn.reshape(b * t, ATTN_W), gdn.reshape(b * t, GDN_W), scm.reshape(b * t, SC_W),
                   x2, lw["w_out"], layer, lw["ln1_g"], lw["ln1_b"], min(tm_out, b * t))
    y = _ffn(h1, lw["w_gate"], lw["w_up"], lw["w_down"], layer, lw["ln2_g"], lw["ln2_b"],
             min(tm_ffn, b * t), tf)
    outs = dict(
        stacks=(k_st, v_st, ik_st),
        s=s_new,
        gconv=h3[:, t - (GDN_CONV - 1):, C_GQKV:C_GQKV + GDN_CONV_DIM],
        sconv=utail[:, 8 - (SC_CONV - 1):, :],
        attn=attn, gdn=gdn, scm=scm,
    )
    return y.reshape(b, t, D_MODEL), outs


def _sample_prep_kernel(h, c128, s128, c64, s64, gst, sst, s_in, gw, sw, alog, dtb, norm_g,
                        q_o, k_o, iq_o, ik_o, iw_o, gst_o, sst_o, s_o, gdn_o, scm_o, o_buf, *, nb):
    cos = c128[...]
    sin = s128[...]
    for hh in range(N_Q):
        sl = slice(hh * HEAD_DIM, (hh + 1) * HEAD_DIM)
        q_o[:, sl] = _rope128(h[:, C_AQ + hh * HEAD_DIM:C_AQ + (hh + 1) * HEAD_DIM], cos, sin) * (HEAD_DIM ** -0.5)
    for hh in range(N_KV):
        sl = slice(hh * HEAD_DIM, (hh + 1) * HEAD_DIM)
        k_o[:, sl] = _rope128(h[:, C_AK + hh * HEAD_DIM:C_AK + (hh + 1) * HEAD_DIM], cos, sin)
    cos = c64[...]
    sin = s64[...]
    for j in range(IDX_H * IDX_D // 128):
        sl = slice(j * 128, (j + 1) * 128)
        iq_o[:, sl] = _rope64x2(h[:, C_IQ + j * 128:C_IQ + (j + 1) * 128], cos, sin) * (IDX_D ** -0.5)
    small = h[:, C_SM:C_SM + 128]
    ik_o[...] = _rope64x2(small, cos, sin)[:, SM_IK:SM_IK + IDX_D]
    iw_o[...] = small[:, SM_IW:SM_IW + IDX_H] * (IDX_H ** -0.5)

    gq = h[:, C_GQKV:C_GQKV + GDN_CONV_DIM]
    y = gq * gw[GDN_CONV - 1:GDN_CONV, :]
    for j in range(GDN_CONV - 1):
        y = y + gst[j] * gw[j:j + 1, :]
    for j in range(GDN_CONV - 2):
        gst_o[j] = gst[j + 1]
    gst_o[GDN_CONV - 2] = gq
    qs, ks, v, gs, bs = _gdn_activations(y, small, alog[...], dtb[...])
    for b in range(nb):
        for hh in range(GDN_H):
            sl = slice(hh * GDN_D, (hh + 1) * GDN_D)
            kc = jnp.broadcast_to(ks[hh][b:b + 1, :], (GDN_D, GDN_D)).T
            qc = jnp.broadcast_to(qs[hh][b:b + 1, :], (GDN_D, GDN_D)).T
            s = s_in[b, hh] * jnp.exp(gs[hh][b:b + 1, :])
            ks_row = jnp.sum(kc * s, axis=0, keepdims=True)
            delta = (v[b:b + 1, sl] - ks_row) * bs[hh][b:b + 1, :]
            s = s + kc * delta
            s_o[b, hh] = s
            o_buf[b:b + 1, sl] = jnp.sum(qc * s, axis=0, keepdims=True)
    ng = norm_g[...]
    for hh in range(GDN_H):
        sl = slice(hh * GDN_D, (hh + 1) * GDN_D)
        z = h[:, C_GZ + hh * GDN_D:C_GZ + (hh + 1) * GDN_D]
        gdn_o[:, sl] = _gated_rmsnorm(o_buf[:, sl], z, ng).astype(BF)

    u = h[:, C_SC:C_SC + SC_W] * h[:, C_SX:C_SX + SC_W]
    y = u * sw[SC_CONV - 1:SC_CONV, :]
    for j in range(SC_CONV - 1):
        y = y + sst[j] * sw[j:j + 1, :]
    for j in range(SC_CONV - 2):
        sst_o[j] = sst[j + 1]
    sst_o[SC_CONV - 2] = u
    scm_o[...] = (h[:, C_SB:C_SB + SC_W] * y).astype(BF)


def _sample_prep(h, tabs, gst, sst, s_in, gw, sw, alog, dtb, norm_g):
    nb = h.shape[0]
    sds = jax.ShapeDtypeStruct
    return pl.pallas_call(
        functools.partial(_sample_prep_kernel, nb=nb),
        out_shape=[sds((nb, ATTN_W), F32), sds((nb, KV_W), F32), sds((nb, IDX_H * IDX_D), F32),
                   sds((nb, IDX_D), F32), sds((nb, IDX_H), F32),
                   sds(gst.shape, F32), sds(sst.shape, F32), sds(s_in.shape, F32),
                   sds((nb, GDN_W), BF), sds((nb, SC_W), BF)],
        scratch_shapes=[pltpu.VMEM((nb, GDN_W), F32)],
        compiler_params=pltpu.CompilerParams(vmem_limit_bytes=VMEM_LIMIT),
        name="sample_prep",
    )(h, *tabs, gst, sst, s_in, gw, sw, alog, dtb, norm_g)


def _sample_index_kernel(pt, *refs, pps, topk):
    pages = refs[:pps]
    iq, iw, ikn, expand, bias_o, seln_o, keys = refs[pps:]
    s = pl.program_id(1)
    iqb = iq[...]
    w = jnp.broadcast_to(iw[...], (IDX_H, PAGE))
    for j in range(pps):
        sc = jnp.maximum(_dot(iqb, pages[j][...].astype(BF)), 0.0)
        keys[pl.ds(s * pps + j, 1), :] = _sortable(jnp.sum(sc * w, axis=0, keepdims=True))

    @pl.when(s == pl.num_programs(1) - 1)
    def _():
        prod = iqb.astype(F32) * ikn[...].astype(BF).astype(F32)
        sn = jnp.sum(prod, axis=1, keepdims=True)
        sn = jnp.sum(jnp.maximum(sn, 0.0) * iw[...], axis=0, keepdims=True)
        key_new = jnp.broadcast_to(_sortable(sn), (1, PAGE))
        kall = keys[...]

        def count_ge(cand):
            c = jnp.where(kall >= cand, 1.0, 0.0).sum(axis=0, keepdims=True).sum(axis=1, keepdims=True)
            return jnp.broadcast_to(c, (1, PAGE)) + jnp.where(key_new >= cand, 1.0, 0.0)

        theta = _kth_largest(count_ge, (1, PAGE), float(topk))
        gt = jnp.where(kall > theta, 1.0, 0.0)
        tie = jnp.where(kall == theta, 1.0, 0.0)
        new_gt = jnp.where(key_new > theta, 1.0, 0.0)
        n_gt = jnp.broadcast_to(gt.sum(axis=0, keepdims=True).sum(axis=1, keepdims=True), (1, PAGE)) + new_gt
        quota = float(topk) - n_gt
        ii = lax.broadcasted_iota(I32, (PAGE, PAGE), 0)
        jj = lax.broadcasted_iota(I32, (PAGE, PAGE), 1)
        in_page = _dot(tie.astype(BF), jnp.where(ii < jj, 1.0, 0.0).astype(BF))
        per_page = jnp.broadcast_to(jnp.sum(tie, axis=1, keepdims=True), (kall.shape[0], PAGE))
        npg = kall.shape[0]
        pi = lax.broadcasted_iota(I32, (npg, npg), 0)
        pj = lax.broadcasted_iota(I32, (npg, npg), 1)
        earlier = _dot(jnp.where(pj < pi, 1.0, 0.0).astype(BF), per_page.astype(BF))
        taken = jnp.where(in_page + earlier < quota, tie, 0.0)
        n_tie = jnp.broadcast_to(tie.sum(axis=0, keepdims=True).sum(axis=1, keepdims=True), (1, PAGE))
        new_sel = jnp.where(key_new > theta, 1.0, jnp.where((key_new == theta) & (n_tie < quota), 1.0, 0.0))
        sel = _dot((gt + taken).astype(BF), expand[...])
        bias_o[...] = jnp.where(sel > 0.5, 0.0, MASKED)
        seln_o[...] = jnp.where(new_sel > 0.5, 0.0, MASKED)


def _expand_matrix():
    return jnp.asarray(np.arange(PAGE)[:, None] == (np.arange(PAGE * N_KV)[None, :] // N_KV), BF)


def _sample_index(page_table, kidx_t, layer, iq3, iw3, ikn3, pps=16):
    nb, npages = page_table.shape
    topk = min(TOPK_MAX, (npages * PAGE + 1) // 4)
    page = lambda j: pl.BlockSpec((None, None, IDX_D, PAGE),
                                  lambda b, s, pt, j=j: (layer, pt[b, s * pps + j], 0, 0))
    per_b = lambda shp: pl.BlockSpec((None,) + shp, lambda b, s, pt: (b, 0, 0))
    grid_spec = pltpu.PrefetchScalarGridSpec(
        num_scalar_prefetch=1,
        grid=(nb, npages // pps),
        in_specs=[page(j) for j in range(pps)]
                 + [per_b((IDX_H, IDX_D)), per_b((IDX_H, 1)), per_b((1, IDX_D)),
                    pl.BlockSpec((PAGE, PAGE * N_KV), lambda b, s, pt: (0, 0))],
        out_specs=[per_b((npages, PAGE * N_KV)), per_b((1, PAGE))],
        scratch_shapes=[pltpu.VMEM((npages, PAGE), I32)],
    )
    return pl.pallas_call(
        functools.partial(_sample_index_kernel, pps=pps, topk=topk),
        grid_spec=grid_spec,
        out_shape=[jax.ShapeDtypeStruct((nb, npages, PAGE * N_KV), F32), jax.ShapeDtypeStruct((nb, 1, PAGE), F32)],
        compiler_params=_cp("parallel", "arbitrary"),
        name="sample_index",
    )(page_table, *([kidx_t] * pps), iq3, iw3, ikn3, _expand_matrix())


def _sample_attn_kernel(pt, *refs, pps):
    kp = refs[:pps]
    vp = refs[pps:2 * pps]
    q8, bias, seln, kn8, vn8, o_ref, m_ref, l_ref, acc_ref = refs[2 * pps:]
    s = pl.program_id(1)
    rows = PAGE * N_KV

    @pl.when(s == 0)
    def _():
        m_ref[...] = jnp.full_like(m_ref, NEG_INF)
        l_ref[...] = jnp.zeros_like(l_ref)
        acc_ref[...] = jnp.zeros_like(acc_ref)

    q = q8[...]
    lane_head = lax.broadcasted_iota(I32, (N_Q, rows), 1) % N_KV
    own = lane_head == lax.broadcasted_iota(I32, (N_Q, rows), 0) // (N_Q // N_KV)
    head_mask = jnp.where(own, 0.0, MASKED)
    lg = [_dot_nt(q, kp[j][...].astype(BF)) + (bias[pl.ds(s * pps + j, 1), :] + head_mask) for j in range(pps)]
    m_old = m_ref[...]
    m_new = m_old
    for x in lg:
        m_new = jnp.maximum(m_new, jnp.max(x, axis=1, keepdims=True))
    alpha = jnp.exp(m_old - m_new)
    l_new = l_ref[...] * alpha
    acc = acc_ref[...] * alpha
    for j in range(pps):
        p = jnp.exp(lg[j] - m_new)
        l_new = l_new + jnp.sum(p, axis=1, keepdims=True)
        acc = acc + _dot(p.astype(BF), vp[j][...].astype(BF))
    m_ref[...] = m_new
    l_ref[...] = l_new
    acc_ref[...] = acc

    @pl.when(s == pl.num_programs(1) - 1)
    def _():
        knb = kn8[...].astype(BF).astype(F32)
        vnb = vn8[...].astype(BF).astype(F32)
        x = jnp.sum(q.astype(F32) * knb, axis=1, keepdims=True) + seln[:, 0:1]
        m_fin = jnp.maximum(m_new, x)
        a = jnp.exp(m_new - m_fin)
        p = jnp.exp(x - m_fin)
        l_fin = l_new * a + p
        o_ref[...] = ((acc * a + p.astype(BF).astype(F32) * vnb) / l_fin).astype(BF)


def _sample_attn(page_table, ck2, cv2, layer, q8, bias, seln, kn8, vn8, pps=8):
    nb, npages = page_table.shape
    rows = PAGE * N_KV
    page = lambda j: pl.BlockSpec((None, None, rows, HEAD_DIM),
                                  lambda b, s, pt, j=j: (layer, pt[b, s * pps + j], 0, 0))
    per_b = lambda shp: pl.BlockSpec((None,) + shp, lambda b, s, pt: (b, 0, 0))
    grid_spec = pltpu.PrefetchScalarGridSpec(
        num_scalar_prefetch=1,
        grid=(nb, npages // pps),
        in_specs=[page(j) for j in range(pps)] * 2
                 + [per_b((N_Q, HEAD_DIM)), per_b((npages, rows)), per_b((1, PAGE)),
                    per_b((N_Q, HEAD_DIM)), per_b((N_Q, HEAD_DIM))],
        out_specs=per_b((N_Q, HEAD_DIM)),
        scratch_shapes=[pltpu.VMEM((N_Q, 1), F32), pltpu.VMEM((N_Q, 1), F32), pltpu.VMEM((N_Q, HEAD_DIM), F32)],
    )
    return pl.pallas_call(
        functools.partial(_sample_attn_kernel, pps=pps),
        grid_spec=grid_spec,
        out_shape=jax.ShapeDtypeStruct((nb, N_Q, HEAD_DIM), BF),
        compiler_params=_cp("parallel", "arbitrary"),
        name="sample_attn",
    )(page_table, *([ck2] * pps), *([cv2] * pps), q8, bias, seln, kn8, vn8)


SAMPLE_ROWS = 16


def _sample_layer(xs, tabs, lw, layer, page_table, cache_k, cache_v, cache_kidx, gst, sst, s_in):
    nb = page_table.shape[0]
    h = _matmul(xs, lw["w_in"], layer, SAMPLE_ROWS, 768)[:nb]
    q, k, iq, ik, iw, gst_n, sst_n, s_n, gdn, scm = _sample_prep(
        h, tabs, gst, sst, s_in, lw["gdn_conv_w"], lw["sc_conv_w"], lw["alog"], lw["dtb"], lw["norm_g"])
    v = h[:, C_AV:C_AV + KV_W]
    bias, seln = _sample_index(page_table, cache_kidx, layer, iq.reshape(nb, IDX_H, IDX_D).astype(BF),
                               iw.reshape(nb, IDX_H, 1), ik.reshape(nb, 1, IDX_D))
    own_kv = lambda a: jnp.repeat(a.reshape(nb, N_KV, HEAD_DIM), N_Q // N_KV, axis=1)
    attn = _sample_attn(page_table, cache_k, cache_v, layer, q.reshape(nb, N_Q, HEAD_DIM).astype(BF),
                        bias, seln, own_kv(k), own_kv(v)).reshape(nb, ATTN_W)
    padr = lambda a: jnp.pad(a, ((0, SAMPLE_ROWS - nb), (0, 0)))
    h1 = _out_proj(padr(attn), padr(gdn), padr(scm), xs, lw["w_out"], layer, lw["ln1_g"], lw["ln1_b"],
                   SAMPLE_ROWS)
    y = _ffn(h1, lw["w_gate"], lw["w_up"], lw["w_down"], layer, lw["ln2_g"], lw["ln2_b"], SAMPLE_ROWS, 512)
    outs = dict(k=k.reshape(nb, 1, N_KV, HEAD_DIM), v=v.reshape(nb, 1, N_KV, HEAD_DIM), kidx=ik.reshape(nb, 1, IDX_D),
                s=s_n, gconv=jnp.swapaxes(gst_n, 0, 1), sconv=jnp.swapaxes(sst_n, 0, 1),
                attn=attn, gdn=gdn, scm=scm)
    return y, outs


def kernel(x_prompt, x_sample, cache_k, cache_v, cache_kidx, page_table, state_gdn, state_gdn_conv,
           state_sc_conv, w_in, gdn_conv_w, gdn_a_log, gdn_dt_bias, gdn_norm_g, sc_conv_w, w_out,
           ln1_g, ln1_b, w_gate, w_up, w_down, ln2_g, ln2_b):
    bp, tp, _ = x_prompt.shape
    nb, ts, _ = x_sample.shape
    assert ts == 1
    past = page_table.shape[1] * PAGE
    tabs_p = _rope_tables(jnp.arange(tp))
    tabs_s = _rope_tables(past + jnp.arange(ts))
    w_in_p = _pack_w_in(w_in)
    w_out_b, w_gate_b, w_up_b, w_down_b = (w.astype(BF) for w in (w_out, w_gate, w_up, w_down))
    depth, pool = cache_k.shape[:2]
    cache_k = cache_k.reshape(depth, pool, PAGE * N_KV, HEAD_DIM)
    cache_v = cache_v.reshape(depth, pool, PAGE * N_KV, HEAD_DIM)
    cache_kidx = jnp.swapaxes(cache_kidx, 2, 3)

    xp = x_prompt
    xs = jnp.pad(x_sample.reshape(nb, D_MODEL), ((0, SAMPLE_ROWS - nb), (0, 0)))
    stacks = (jnp.zeros((DEPTH, bp, tp * N_KV, HEAD_DIM), F32), jnp.zeros((DEPTH, bp, tp * N_KV, HEAD_DIM), F32),
              jnp.zeros((DEPTH, bp, IDX_D, tp), F32))
    po, so = [], []
    for l in range(DEPTH):
        lw = dict(layer=l, w_in=w_in_p, gdn_conv_w=gdn_conv_w[l], sc_conv_w=sc_conv_w[l],
                  alog=_head_rows(gdn_a_log[l]), dtb=_head_rows(gdn_dt_bias[l]), norm_g=gdn_norm_g[l][None, :],
                  w_out=w_out_b, ln1_g=ln1_g[:, None, :], ln1_b=ln1_b[:, None, :],
                  w_gate=w_gate_b, w_up=w_up_b, w_down=w_down_b,
                  ln2_g=ln2_g[:, None, :], ln2_b=ln2_b[:, None, :])
        xp, o = _prompt_layer(xp, tabs_p, lw, stacks)
        stacks = o["stacks"]
        po.append(o)
        xs, o = _sample_layer(xs, tabs_s, lw, l, page_table, cache_k, cache_v, cache_kidx,
                              jnp.swapaxes(state_gdn_conv[l], 0, 1), jnp.swapaxes(state_sc_conv[l], 0, 1),
                              state_gdn[l])
        so.append(o)

    st = lambda outs, n: jnp.stack([o[n] for o in outs])
    k_st, v_st, ik_st = stacks
    return (xp, xs[:nb].reshape(nb, ts, D_MODEL),
            k_st.reshape(DEPTH, bp, tp, N_KV, HEAD_DIM), v_st.reshape(DEPTH, bp, tp, N_KV, HEAD_DIM),
            jnp.swapaxes(ik_st, 2, 3), st(so, "k"), st(so, "v"), st(so, "kidx"),
            st(po, "s"), st(so, "s"), st(po, "gconv"), st(so, "gconv"), st(po, "sconv"), st(so, "sconv"))
```

```python
import functools
import math

import numpy as np
import jax
import jax.numpy as jnp
from jax import lax
from jax.experimental import pallas as pl
from jax.experimental.pallas import tpu as pltpu

F32 = jnp.float32
BF = jnp.bfloat16
I32 = jnp.int32

D_MODEL = 2048
DEPTH = 4
PAGE = 128
HEAD_DIM = 128
N_Q = 8
N_KV = 4
ATTN_W = N_Q * HEAD_DIM
KV_W = N_KV * HEAD_DIM
IDX_H = 16
IDX_D = 64
TOPK_MAX = 256
ROPE_THETA = 10000.0
GDN_H = 4
GDN_D = 128
GDN_W = GDN_H * GDN_D
GDN_CONV = 4
GDN_CONV_DIM = 3 * GDN_W
GDN_CHUNK = 64
GDN_CPS = 4
SC_W = 512
SC_CONV = 3
FFN_DIM = 5632
DN_ALPHA = (2 * DEPTH) ** 0.25
LN_EPS = 1e-5
NORM_EPS = 1e-6
IN_SIZES = (ATTN_W, KV_W, KV_W, IDX_H * IDX_D, IDX_D, IDX_H, GDN_CONV_DIM, GDN_W, GDN_H, GDN_H,
            SC_W, SC_W, SC_W)

C_AQ, C_AK, C_AV, C_IQ, C_GQKV, C_GZ, C_SB, C_SC, C_SX, C_SM = (
    0, 1024, 1536, 2048, 3072, 4608, 5120, 5632, 6144, 6656)
SM_IK, SM_IW, SM_GB, SM_GA = 0, 64, 80, 84
IN_PACKED = 6912

NEG_INF = float("-inf")
MASKED = -1e30
Q_SCALE_LOG2 = HEAD_DIM ** -0.5 * math.log2(math.e)
VT_ROWS = HEAD_DIM + 16
SUB = 64
INT_MIN = -2 ** 31
INT_MAX = 2 ** 31 - 1

VMEM_LIMIT = 56 * 1024 * 1024


def _cp(*sem):
    return pltpu.CompilerParams(dimension_semantics=tuple(sem), vmem_limit_bytes=VMEM_LIMIT)


def _sigmoid(x):
    return 1.0 / (1.0 + jnp.exp(-x))


def _softplus(x):
    return jnp.maximum(x, 0.0) + jnp.log1p(jnp.exp(-jnp.abs(x)))


def _dot(a, b):
    return jnp.dot(a, b, preferred_element_type=F32)


def _dot_hi(a, b):
    return jnp.dot(a, b, preferred_element_type=F32, precision=lax.Precision.HIGHEST)


def _dot_nt(a, b):
    return lax.dot_general(a, b, (((1,), (1,)), ((), ())), preferred_element_type=F32)


_IN_OFF = np.concatenate([[0], np.cumsum(IN_SIZES)]).tolist()
_O_IK, _O_GQKV, _O_GB, _O_SB, _O_END = _IN_OFF[4], _IN_OFF[6], _IN_OFF[8], _IN_OFF[10], _IN_OFF[13]


PACK_COLS = 256
_PB_G, _PB_S, _PB_SM = C_GQKV // PACK_COLS, C_SB // PACK_COLS, C_SM // PACK_COLS
_SH_G, _SH_S = _O_GQKV - C_GQKV, _O_SB - C_SB


def _pack_kernel(a, b, o):
    j = pl.program_id(1)

    def emit(src):
        o[...] = src.T.astype(BF)

    @pl.when(j < _PB_G)
    def _():
        emit(a[...])

    @pl.when((j >= _PB_G) & (j < _PB_S))
    def _():
        emit(jnp.concatenate([a[_SH_G:, :], b[:_SH_G, :]], axis=0))

    @pl.when((j >= _PB_S) & (j < _PB_SM))
    def _():
        emit(jnp.concatenate([a[_SH_S:, :], b[:_SH_S, :]], axis=0))

    @pl.when(j == _PB_SM)
    def _():
        n_idx = _O_GQKV - _O_IK
        n_gdn = _O_SB - _O_GB
        lo = _O_GB - (_O_GB // 128) * 128
        pad = jnp.zeros((PACK_COLS - n_idx - n_gdn, a.shape[1]), F32)
        emit(jnp.concatenate([a[:n_idx, :], b[lo:lo + n_gdn, :], pad], axis=0))


def _pack_w_in(w_in):
    wt = jnp.swapaxes(w_in, 1, 2)
    depth, n, d = wt.shape
    nblk = IN_PACKED // PACK_COLS
    last_b = (n - 1) // 128

    def a_map(l, j):
        return l, jnp.where(j == _PB_SM, _O_IK // PACK_COLS, j), 0

    def b_map(l, j):
        return l, jnp.where(j == _PB_SM, _O_GB // 128, jnp.minimum(2 * j + 2, last_b)), 0

    return pl.pallas_call(
        _pack_kernel,
        grid=(depth, nblk),
        in_specs=[pl.BlockSpec((None, PACK_COLS, d), a_map), pl.BlockSpec((None, 128, d), b_map)],
        out_specs=pl.BlockSpec((None, d, PACK_COLS), lambda l, j: (l, 0, j)),
        out_shape=jax.ShapeDtypeStruct((depth, d, IN_PACKED), BF),
        compiler_params=_cp("parallel", "parallel"),
        name="pack_w_in",
    )(wt, wt)


def _mm_kernel(x_ref, w_ref, o_ref):
    o_ref[...] = _dot(x_ref[...].astype(BF), w_ref[...])


def _matmul(x, w, layer, tm, tn):
    m, k = x.shape
    n = w.shape[2]
    return pl.pallas_call(
        _mm_kernel,
        grid=(m // tm, n // tn),
        in_specs=[pl.BlockSpec((tm, k), lambda i, j: (i, 0)),
                  pl.BlockSpec((None, k, tn), lambda i, j: (layer, 0, j))],
        out_specs=pl.BlockSpec((tm, tn), lambda i, j: (i, j)),
        out_shape=jax.ShapeDtypeStruct((m, n), F32),
        compiler_params=_cp("parallel", "arbitrary"),
        name="proj_in",
    )(x, w)


def _rope_tables(pos):
    def tab(half):
        inv = ROPE_THETA ** (-jnp.arange(half, dtype=F32) / half)
        ang = pos.astype(F32)[:, None] * inv
        return jnp.cos(ang), jnp.sin(ang)
    c, s = tab(HEAD_DIM // 2)
    c128 = jnp.concatenate([c, c], -1)
    s128 = jnp.concatenate([-s, s], -1)
    c, s = tab(IDX_D // 2)
    c64 = jnp.concatenate([c, c, c, c], -1)
    s64 = jnp.concatenate([-s, s, -s, s], -1)
    return c128, s128, c64, s64


def _rope128(x, cos, sin):
    return x * cos + pltpu.roll(x, HEAD_DIM // 2, 1) * sin


def _rope64x2(x, cos, sin):
    lane = lax.broadcasted_iota(I32, x.shape, 1)
    first = (lane % IDX_D) < (IDX_D // 2)
    partner = jnp.where(first, pltpu.roll(x, 128 - IDX_D // 2, 1), pltpu.roll(x, IDX_D // 2, 1))
    return x * cos + partner * sin


def _attn_prep_kernel(aq, ak, av, iq, sm, c128, s128, c64, s64, k_st, v_st, ik_st,
                      ks_o, vs_o, iks_o, kb_o, vt_o, qt_o, iqt_o, ikb_o, iwt_o, *, tr):
    del k_st, v_st, ik_st
    cos = c128[...]
    sin = s128[...]
    for h in range(N_Q):
        sl = slice(h * HEAD_DIM, (h + 1) * HEAD_DIM)
        y = _rope128(aq[:, sl], cos, sin) * Q_SCALE_LOG2
        qt_o[sl, :] = y.T.astype(BF)
    for h in range(N_KV):
        sl = slice(h * HEAD_DIM, (h + 1) * HEAD_DIM)
        y = _rope128(ak[:, sl], cos, sin)
        v = av[:, sl]
        ks_o[pl.ds(h, tr, stride=N_KV), :] = y
        vs_o[pl.ds(h, tr, stride=N_KV), :] = v
        kb_o[:, sl] = y.astype(BF)
        vt_o[h * VT_ROWS:h * VT_ROWS + HEAD_DIM, :] = v.T.astype(BF)
        vt_o[h * VT_ROWS + HEAD_DIM:(h + 1) * VT_ROWS, :] = jnp.ones((VT_ROWS - HEAD_DIM, tr), BF)
    cos = c64[...]
    sin = s64[...]
    for j in range(IDX_H * IDX_D // 128):
        sl = slice(j * 128, (j + 1) * 128)
        y = _rope64x2(iq[:, sl], cos, sin) * (IDX_D ** -0.5)
        iqt_o[sl, :] = y.T.astype(BF)
    x = sm[...]
    y = _rope64x2(x, cos, sin)
    iks_o[...] = y.T[SM_IK:SM_IK + IDX_D, :]
    ikb_o[...] = y[:, SM_IK:SM_IK + IDX_D].astype(BF)
    iwt_o[...] = x.T[SM_IW:SM_IW + IDX_H, :] * (IDX_H ** -0.5)


def _attn_prep(h3, tabs, tr, layer, k_st, v_st, ik_st):
    b, t, _ = h3.shape
    nr = t // tr
    col = lambda w, off: pl.BlockSpec((None, tr, w), lambda bi, r, o=off // w: (bi, r, o))
    tab = pl.BlockSpec((tr, 128), lambda bi, r: (r, 0))
    anyspec = pl.BlockSpec(memory_space=pl.ANY)
    sds = jax.ShapeDtypeStruct
    return pl.pallas_call(
        functools.partial(_attn_prep_kernel, tr=tr),
        grid=(b, nr),
        in_specs=[col(ATTN_W, C_AQ), col(KV_W, C_AK), col(KV_W, C_AV), col(ATTN_W, C_IQ), col(128, C_SM),
                  tab, tab, tab, tab, anyspec, anyspec, anyspec],
        out_specs=[
            pl.BlockSpec((None, None, tr * N_KV, HEAD_DIM), lambda bi, r: (layer, bi, r, 0)),
            pl.BlockSpec((None, None, tr * N_KV, HEAD_DIM), lambda bi, r: (layer, bi, r, 0)),
            pl.BlockSpec((None, None, IDX_D, tr), lambda bi, r: (layer, bi, 0, r)),
            pl.BlockSpec((None, tr, KV_W), lambda bi, r: (bi, r, 0)),
            pl.BlockSpec((None, None, N_KV * VT_ROWS, tr), lambda bi, r: (bi, r, 0, 0)),
            pl.BlockSpec((None, ATTN_W, tr), lambda bi, r: (bi, 0, r)),
            pl.BlockSpec((None, ATTN_W, tr), lambda bi, r: (bi, 0, r)),
            pl.BlockSpec((None, tr, IDX_D), lambda bi, r: (bi, r, 0)),
            pl.BlockSpec((None, IDX_H, tr), lambda bi, r: (bi, 0, r)),
        ],
        out_shape=[
            sds(k_st.shape, F32), sds(v_st.shape, F32), sds(ik_st.shape, F32),
            sds((b, t, KV_W), BF),
            sds((b, nr, N_KV * VT_ROWS, tr), BF),
            sds((b, ATTN_W, t), BF),
            sds((b, ATTN_W, t), BF),
            sds((b, t, IDX_D), BF),
            sds((b, IDX_H, t), F32),
        ],
        input_output_aliases={9: 0, 10: 1, 11: 2},
        compiler_params=_cp("parallel", "parallel"),
        name="attn_prep",
    )(h3, h3, h3, h3, h3, *tabs, k_st, v_st, ik_st)


def _sortable(x):
    bits = pltpu.bitcast(x, I32)
    return jnp.where(bits >= 0, bits, bits ^ jnp.int32(0x7FFFFFFF))


def _kth_largest(count_ge, shape, k):
    lo = jnp.where(count_ge(jnp.zeros(shape, I32)) >= k, 0, INT_MIN).astype(I32)

    def body(it, lo):
        cand = lo + (jnp.int32(1) << (30 - it))
        return jnp.where(count_ge(cand) >= k, cand, lo)

    return lax.fori_loop(0, 31, body, lo)


def _dsa_prompt_kernel(qt, iqt, iwt, kb, vt, ikb, o_ref, keys, bias, m_ref, l_ref, acc_ref, s_ref, *, tq, topk):
    i = pl.program_id(1)
    nch = i + 1
    row = lax.broadcasted_iota(I32, (tq, tq), 0)
    lane = lax.broadcasted_iota(I32, (tq, tq), 1)

    def off_of(c):
        return pl.multiple_of(c * tq, tq)

    def causal_of(c):
        return (c * tq + row) <= (i * tq + lane)

    def fold8(x):
        return x.reshape(tq // 8, 8, tq)

    def indexer(c, carry):
        off = off_of(c)
        ikc = ikb[pl.ds(off, tq), :]
        acc = jnp.zeros((tq, tq), F32)
        for h in range(IDX_H):
            s = _dot(ikc, iqt[h * IDX_D:(h + 1) * IDX_D, :])
            acc = acc + jnp.maximum(s, 0.0) * iwt[h:h + 1, :]
        acc = jnp.where(causal_of(c), acc, NEG_INF)
        keys[pl.ds(off, tq), :] = _sortable(acc)
        return carry

    lax.fori_loop(0, nch, indexer, 0)

    def count_ge(cand):
        def hits(c):
            hit = jnp.where(keys[pl.ds(off_of(c), tq), :] >= cand, 1, 0)
            parts = [hit[r * 8:(r + 1) * 8] for r in range(tq // 8)]
            while len(parts) > 1:
                parts = [parts[j] + parts[j + 1] for j in range(0, len(parts), 2)]
            return parts[0]

        cnt = lax.fori_loop(0, nch // 2, lambda j, cnt: cnt + (hits(2 * j) + hits(2 * j + 1)),
                            jnp.zeros((8, tq), I32))
        cnt = cnt + lax.cond(nch % 2 == 1, lambda: hits(nch - 1), lambda: jnp.zeros((8, tq), I32))
        return cnt.sum(axis=0, keepdims=True)

    theta = _kth_largest(count_ge, (1, tq), topk)
    tied = jnp.max(count_ge(theta)) > topk

    @pl.when(jnp.logical_not(tied))
    def _():
        def make_bias(c, carry):
            off = off_of(c)
            sel = keys[pl.ds(off, tq), :] >= theta
            bias[pl.ds(off, tq), :] = jnp.where(causal_of(c), jnp.where(sel, 0.0, MASKED), MASKED)
            return carry

        lax.fori_loop(0, nch, make_bias, 0)

    @pl.when(tied)
    def _():
        n_gt = jnp.where(theta == INT_MAX, 0, count_ge(jnp.where(theta == INT_MAX, theta, theta + 1)))
        quota = (topk - n_gt).astype(F32)
        before = jnp.where(row > lane, 1.0, 0.0).astype(BF)

        def make_bias(c, seen):
            off = off_of(c)
            kc = keys[pl.ds(off, tq), :]
            tie = jnp.where(kc == theta, 1.0, 0.0)
            rank = seen + _dot(before, tie.astype(BF))
            sel = jnp.where(kc > theta, 1.0, jnp.where(rank < quota, tie, 0.0))
            bias[pl.ds(off, tq), :] = jnp.where(causal_of(c), jnp.where(sel > 0.5, 0.0, MASKED), MASKED)
            return seen + jnp.sum(tie, axis=0, keepdims=True)

        lax.fori_loop(0, nch, make_bias, jnp.zeros((1, tq), F32))

    m_ref[...] = jnp.full_like(m_ref, NEG_INF)
    l_ref[...] = jnp.zeros_like(l_ref)
    acc_ref[...] = jnp.zeros_like(acc_ref)

    def attend(c, carry):
        off = off_of(c)
        m_new = []
        for h in range(N_Q):
            gsl = slice(h // (N_Q // N_KV) * HEAD_DIM, (h // (N_Q // N_KV) + 1) * HEAD_DIM)
            qh = qt[h * HEAD_DIM:(h + 1) * HEAD_DIM, :]
            m8 = jnp.full((8, tq), NEG_INF, F32)
            for sb in range(tq // SUB):
                rows = pl.ds(pl.multiple_of(off + sb * SUB, SUB), SUB)
                s = _dot(kb[rows, gsl], qh) + bias[rows, :]
                s_ref[h, sb * SUB:(sb + 1) * SUB, :] = s
                m8 = jnp.maximum(m8, s.reshape(SUB // 8, 8, tq).max(axis=0))
            m_new.append(jnp.maximum(m_ref[h], m8.max(axis=0, keepdims=True)))
        for h in range(N_Q):
            g = h // (N_Q // N_KV)
            p = jnp.exp2(s_ref[h] - m_new[h]).astype(BF)
            alpha = jnp.exp2(m_ref[h] - m_new[h])
            pv = _dot(vt[c, g * VT_ROWS:(g + 1) * VT_ROWS, :], p)
            acc_ref[h] = acc_ref[h] * alpha + pv[0:HEAD_DIM]
            l_ref[h] = l_ref[h] * alpha + pv[HEAD_DIM:HEAD_DIM + 1]
            m_ref[h] = m_new[h]
        return carry

    lax.fori_loop(0, nch, attend, 0)

    for h in range(N_Q):
        o_ref[:, h * HEAD_DIM:(h + 1) * HEAD_DIM] = (acc_ref[h] * (1.0 / l_ref[h])).T.astype(BF)


def _dsa_prompt(qt, iqt, iwt, kb, vt, ikb, tq):
    b, _, t = qt.shape
    topk = min(TOPK_MAX, t // 4)
    kern = functools.partial(_dsa_prompt_kernel, tq=tq, topk=topk)
    return pl.pallas_call(
        kern,
        grid=(b, t // tq),
        in_specs=[
            pl.BlockSpec((None, ATTN_W, tq), lambda bi, i: (bi, 0, i)),
            pl.BlockSpec((None, ATTN_W, tq), lambda bi, i: (bi, 0, i)),
            pl.BlockSpec((None, IDX_H, tq), lambda bi, i: (bi, 0, i)),
            pl.BlockSpec((None, t, KV_W), lambda bi, i: (bi, 0, 0)),
            pl.BlockSpec((None, t // tq, N_KV * VT_ROWS, tq), lambda bi, i: (bi, 0, 0, 0)),
            pl.BlockSpec((None, t, IDX_D), lambda bi, i: (bi, 0, 0)),
        ],
        out_specs=pl.BlockSpec((None, tq, ATTN_W), lambda bi, i: (bi, i, 0)),
        out_shape=jax.ShapeDtypeStruct((b, t, ATTN_W), BF),
        scratch_shapes=[pltpu.VMEM((t, tq), I32), pltpu.VMEM((t, tq), F32),
                        pltpu.VMEM((N_Q, 1, tq), F32), pltpu.VMEM((N_Q, 1, tq), F32),
                        pltpu.VMEM((N_Q, HEAD_DIM, tq), F32), pltpu.VMEM((N_Q, tq, tq), F32)],
        compiler_params=_cp("parallel", "arbitrary"),
        name="dsa_prompt",
    )(qt, iqt, iwt, kb, vt, ikb)


def _l2norm(x):
    return x * lax.rsqrt(jnp.sum(x * x, -1, keepdims=True) + NORM_EPS)


def _gdn_activations(y, small, alog, dtb):
    y = y * _sigmoid(y)
    qs, ks, gs, bs = [], [], [], []
    r = y.shape[0]
    for h in range(GDN_H):
        sl = slice(h * GDN_D, (h + 1) * GDN_D)
        qs.append(_l2norm(y[:, sl]) * (GDN_D ** -0.5))
        ks.append(_l2norm(y[:, GDN_W + h * GDN_D:GDN_W + (h + 1) * GDN_D]))
        b_raw = jnp.broadcast_to(small[:, SM_GB + h:SM_GB + h + 1], (r, GDN_D))
        a_raw = jnp.broadcast_to(small[:, SM_GA + h:SM_GA + h + 1], (r, GDN_D))
        bs.append(_sigmoid(b_raw))
        gs.append(-jnp.exp(alog[:, sl]) * _softplus(a_raw + dtb[:, sl]))
    return qs, ks, y[:, 2 * GDN_W:], gs, bs


def _mix_prep_kernel(gq, gq_prev, sm, sb, sc, sx, sc_prev, sx_prev, gw, sw, alog, dtb,
                     qn_o, kn_o, vv_o, g_o, beta_o, scm_o, utail_o, xbuf, ubuf, *, tr):
    r = pl.program_id(1)
    first = r == 0
    xbuf[0:8, :] = jnp.where(first, 0.0, gq_prev[...])
    xbuf[8:, :] = gq[...]
    y = xbuf[5:5 + tr, :] * gw[0:1, :]
    for j in range(1, GDN_CONV):
        y = y + xbuf[5 + j:5 + j + tr, :] * gw[j:j + 1, :]
    qs, ks, v, gs, bs = _gdn_activations(y, sm[...], alog[...], dtb[...])
    for h in range(GDN_H):
        sl = slice(h * GDN_D, (h + 1) * GDN_D)
        qn_o[:, sl] = qs[h]
        kn_o[:, sl] = ks[h]
        g_o[:, sl] = gs[h]
        beta_o[:, sl] = bs[h]
    vv_o[...] = v

    u = sc[...] * sx[...]
    ubuf[0:8, :] = jnp.where(first, 0.0, sc_prev[...] * sx_prev[...])
    ubuf[8:, :] = u
    y = ubuf[6:6 + tr, :] * sw[0:1, :]
    for j in range(1, SC_CONV):
        y = y + ubuf[6 + j:6 + j + tr, :] * sw[j:j + 1, :]
    scm_o[...] = (sb[...] * y).astype(BF)

    @pl.when(r == pl.num_programs(1) - 1)
    def _():
        utail_o[...] = u[tr - 8:, :]


def _mix_prep(h3, gw, sw, alog, dtb, tr):
    b, t, _ = h3.shape
    nr = t // tr
    col = lambda w, off: pl.BlockSpec((None, tr, w), lambda bi, r, o=off // w: (bi, r, o))
    prev = lambda w, off: pl.BlockSpec(
        (None, 8, w), lambda bi, r, o=off // w: (bi, jnp.maximum(r * (tr // 8) - 1, 0), o))
    full = lambda a: pl.BlockSpec(a.shape, lambda bi, r: (0, 0))
    out = pl.BlockSpec((None, tr, GDN_W), lambda bi, r: (bi, r, 0))
    act = jax.ShapeDtypeStruct((b, t, GDN_W), F32)
    return pl.pallas_call(
        functools.partial(_mix_prep_kernel, tr=tr),
        grid=(b, nr),
        in_specs=[col(GDN_CONV_DIM, C_GQKV), prev(GDN_CONV_DIM, C_GQKV), col(128, C_SM),
                  col(SC_W, C_SB), col(SC_W, C_SC), col(SC_W, C_SX), prev(SC_W, C_SC), prev(SC_W, C_SX),
                  full(gw), full(sw), full(alog), full(dtb)],
        out_specs=[out, out, out, out, out, out,
                   pl.BlockSpec((None, 8, SC_W), lambda bi, r: (bi, 0, 0))],
        out_shape=[act, act, act, act, act,
                   jax.ShapeDtypeStruct((b, t, SC_W), BF),
                   jax.ShapeDtypeStruct((b, 8, SC_W), F32)],
        scratch_shapes=[pltpu.VMEM((tr + 8, GDN_CONV_DIM), F32), pltpu.VMEM((tr + 8, SC_W), F32)],
        compiler_params=_cp("parallel", "arbitrary"),
        name="mix_prep",
    )(h3, h3, h3, h3, h3, h3, h3, h3, gw, sw, alog, dtb)


def _cumsum_rows(x):
    row = lax.broadcasted_iota(I32, x.shape, 0)
    s = 1
    while s < x.shape[0]:
        x = x + jnp.where(row >= s, pltpu.roll(x, s, 0), 0.0)
        s *= 2
    return x


def _split_bf(x):
    hi = x.astype(BF)
    return hi, (x - hi.astype(F32)).astype(BF)


def _dot3(ah, al, bh, bl):
    m = ah.shape[0]
    r = _dot(jnp.concatenate([ah, al], axis=0), bh)
    return r[:m] + r[m:] + _dot(ah, bl)


def _gated_rmsnorm(o, z, norm_g):
    on = o * lax.rsqrt(jnp.mean(o * o, -1, keepdims=True) + NORM_EPS) * norm_g
    return on * (z * _sigmoid(z))


def _gdn_chunk_kernel(qn, kn, vv, gb, betab, z, norm_g, o_ref, s_out, s_ref, *, nb):
    c = pl.program_id(0)
    C = GDN_CHUNK
    W = GDN_H * C

    @pl.when(c == 0)
    def _():
        s_ref[...] = jnp.zeros_like(s_ref)

    ii = lax.broadcasted_iota(I32, (C, W), 0)
    jj = lax.broadcasted_iota(I32, (C, W), 1) % C
    lower = ii >= jj
    strict = ii > jj
    eye = jnp.where(ii == jj, 1.0, 0.0).astype(F32)
    blockmask = (lax.broadcasted_iota(I32, (W, W), 0) // C) == (lax.broadcasted_iota(I32, (W, W), 1) // C)
    zpad = jnp.zeros((GDN_D - C, GDN_D), F32)
    ng = norm_g[...]
    heads = [slice(h * GDN_D, (h + 1) * GDN_D) for h in range(GDN_H)]
    lanes = lambda parts: jnp.concatenate(parts, axis=1)

    blockmask = jnp.where(blockmask, 1.0, 0.0).astype(BF)

    def block_diag(m):
        return jnp.concatenate([m] * GDN_H, axis=0) * blockmask

    st = []
    for b, rows in [(b, slice(j * C, (j + 1) * C)) for b in range(nb) for j in range(GDN_CPS)]:
        q = [qn[b, rows, sl] for sl in heads]
        k = [kn[b, rows, sl] for sl in heads]
        beta = [betab[b, rows, sl] for sl in heads]
        gc = [_cumsum_rows(gb[b, rows, sl]) for sl in heads]
        kbeta = [k[h] * beta[h] for h in range(GDN_H)]
        gcol = lanes([g[:, 0:C] for g in gc])
        grow = lanes([jnp.concatenate([g, g], axis=0).T[0:C, 0:C] for g in gc])
        decay = jnp.where(lower, jnp.exp(jnp.where(lower, gcol - grow, 0.0)), 0.0)
        kk, qk = [], []
        for h in range(GDN_H):
            r = _dot_nt(jnp.concatenate([kbeta[h], q[h]], axis=0).astype(BF), k[h].astype(BF))
            kk.append(r[:C])
            qk.append(r[C:])
        a = jnp.where(strict, lanes(kk) * decay, 0.0)
        intra = lanes(qk) * decay
        rhs = jnp.concatenate(
            [lanes([vv[b, rows, heads[h]] * beta[h], kbeta[h] * jnp.exp(gc[h])]) for h in range(GDN_H)], axis=0)
        st.append(dict(b=b, rows=rows, q=q, k=k, gc=gc, intra=intra, rhs=rhs, nm=-a))

    for d in st:
        nh, nl = _split_bf(d["nm"])
        d["x"] = eye + d["nm"]
        d["p"] = _dot3(nh, nl, block_diag(nh), block_diag(nl))
    step = 4
    while step < C:
        for d in st:
            lh, ll = _split_bf(jnp.concatenate([d["x"], d["p"]], axis=0))
            r = _dot3(lh, ll, block_diag(lh[C:]), block_diag(ll[C:]))
            d["x"] = d["x"] + r[:C]
            d["p"] = r[C:]
        step *= 2
    for d in st:
        xh, xl = _split_bf(d["x"])
        p_hi, p_lo = _split_bf(d["p"])
        th, tl = _split_bf(d["x"] + _dot3(xh, xl, block_diag(p_hi), block_diag(p_lo)))
        rh, rl = _split_bf(d["rhs"])
        d["uw"] = _dot3(block_diag(th), block_diag(tl), rh, rl)

    for d in st:
        b, rows = d["b"], d["rows"]
        for h in range(GDN_H):
            u = d["uw"][h * C:(h + 1) * C, 0:GDN_D]
            w = d["uw"][h * C:(h + 1) * C, GDN_D:]
            gc = d["gc"][h]
            s = s_ref[b, h]
            r = _dot(jnp.concatenate([w, d["q"][h] * jnp.exp(gc)], axis=0).astype(BF), s.astype(BF))
            v_new = u - r[:C]
            o = r[C:] + _dot(d["intra"][:, h * C:(h + 1) * C].astype(BF), v_new.astype(BF))
            g_last = gc[C - 1:C, :]
            kd = d["k"][h] * jnp.exp(g_last - gc)
            kdt = jnp.concatenate([kd, zpad], axis=0).T.astype(BF)
            vnp = jnp.concatenate([v_new, zpad], axis=0).astype(BF)
            s_ref[b, h] = s * jnp.exp(g_last) + _dot(kdt, vnp)
            o_ref[b, rows, heads[h]] = _gated_rmsnorm(o, z[b, rows, heads[h]], ng).astype(BF)

    @pl.when(c == pl.num_programs(0) - 1)
    def _():
        s_out[...] = s_ref[...]


def _gdn_chunked(qn, kn, vv, gb, betab, h3, norm_g):
    b, t, _ = qn.shape
    C = GDN_CHUNK * GDN_CPS
    blk = pl.BlockSpec((b, C, GDN_W), lambda c: (0, c, 0))
    return pl.pallas_call(
        functools.partial(_gdn_chunk_kernel, nb=b),
        grid=(t // C,),
        in_specs=[blk, blk, blk, blk, blk,
                  pl.BlockSpec((b, C, GDN_W), lambda c: (0, c, C_GZ // GDN_W)),
                  pl.BlockSpec((1, GDN_D), lambda c: (0, 0))],
        out_specs=[pl.BlockSpec((b, C, GDN_W), lambda c: (0, c, 0)),
                   pl.BlockSpec((b, GDN_H, GDN_D, GDN_D), lambda c: (0, 0, 0, 0))],
        out_shape=[jax.ShapeDtypeStruct((b, t, GDN_W), BF),
                   jax.ShapeDtypeStruct((b, GDN_H, GDN_D, GDN_D), F32)],
        scratch_shapes=[pltpu.VMEM((b, GDN_H, GDN_D, GDN_D), F32)],
        compiler_params=_cp("arbitrary"),
        name="gdn_chunked",
    )(qn, kn, vv, gb, betab, h3, norm_g)


def _layer_norm(y, g, b):
    mu = jnp.mean(y, -1, keepdims=True)
    yc = y - mu
    var = jnp.mean(yc * yc, -1, keepdims=True)
    return yc * lax.rsqrt(var + LN_EPS) * g + b


def _out_proj_kernel(attn, gdn, scm, x, wo, g, b, o_ref):
    y = _dot(attn[...], wo[0:ATTN_W, :])
    y = y + _dot(gdn[...], wo[ATTN_W:ATTN_W + GDN_W, :])
    y = y + _dot(scm[...], wo[ATTN_W + GDN_W:, :])
    o_ref[...] = _layer_norm(DN_ALPHA * x[...] + y, g[...], b[...])


def _out_proj(attn, gdn, scm, x, wo, layer, g, b, tm):
    m = x.shape[0]
    row = lambda w: pl.BlockSpec((tm, w), lambda i: (i, 0))
    full = lambda a: pl.BlockSpec((None,) + a.shape[1:], lambda i: (layer, 0, 0))
    return pl.pallas_call(
        _out_proj_kernel,
        grid=(m // tm,),
        in_specs=[row(ATTN_W), row(GDN_W), row(SC_W), row(D_MODEL), full(wo), full(g), full(b)],
        out_specs=row(D_MODEL),
        out_shape=jax.ShapeDtypeStruct((m, D_MODEL), F32),
        compiler_params=_cp("parallel"),
        name="out_proj_ln",
    )(attn, gdn, scm, x, wo, g, b)


def _ffn_kernel(h, wg, wu, wd, g, b, o_ref, hb, acc):
    f = pl.program_id(1)

    @pl.when(f == 0)
    def _():
        hb[...] = h[...].astype(BF)
        acc[...] = jnp.zeros_like(acc)

    x = hb[...]
    gate = _dot(x, wg[...])
    hid = gate * _sigmoid(gate) * _dot(x, wu[...])
    acc[...] += _dot(hid.astype(BF), wd[...])

    @pl.when(f == pl.num_programs(1) - 1)
    def _():
        o_ref[...] = _layer_norm(DN_ALPHA * h[...] + acc[...], g[...], b[...])


def _ffn(h, wg, wu, wd, layer, g, b, tm, tf):
    m = h.shape[0]
    full = lambda a: pl.BlockSpec((None,) + a.shape[1:], lambda i, f: (layer, 0, 0))
    return pl.pallas_call(
        _ffn_kernel,
        grid=(m // tm, FFN_DIM // tf),
        in_specs=[pl.BlockSpec((tm, D_MODEL), lambda i, f: (i, 0)),
                  pl.BlockSpec((None, D_MODEL, tf), lambda i, f: (layer, 0, f)),
                  pl.BlockSpec((None, D_MODEL, tf), lambda i, f: (layer, 0, f)),
                  pl.BlockSpec((None, tf, D_MODEL), lambda i, f: (layer, f, 0)),
                  full(g), full(b)],
        out_specs=pl.BlockSpec((tm, D_MODEL), lambda i, f: (i, 0)),
        out_shape=jax.ShapeDtypeStruct((m, D_MODEL), F32),
        scratch_shapes=[pltpu.VMEM((tm, D_MODEL), BF), pltpu.VMEM((tm, D_MODEL), F32)],
        compiler_params=_cp("parallel", "arbitrary"),
        name="ffn_ln",
    )(h, wg, wu, wd, g, b)


def _head_rows(v):
    return jnp.repeat(v.astype(F32), GDN_D)[None, :]


def _prompt_layer(xp, tabs, lw, stacks, tm_in=1024, tn_in=768, tr=256, tm_out=512, tm_ffn=512, tf=512):
    b, t, _ = xp.shape
    x2 = xp.reshape(b * t, D_MODEL)
    layer = lw["layer"]
    h = _matmul(x2, lw["w_in"], layer, min(tm_in, b * t), tn_in)
    h3 = h.reshape(b, t, IN_PACKED)
    k_st, v_st, ik_st, kb, vt, qt, iqt, ikb, iwt = _attn_prep(h3, tabs, tr, layer, *stacks)
    attn = _dsa_prompt(qt, iqt, iwt, kb, vt, ikb, tr)
    qn, kn, vv, gb, betab, scm, utail = _mix_prep(h3, lw["gdn_conv_w"], lw["sc_conv_w"],
                                                 lw["alog"], lw["dtb"], tr)
    gdn, s_new = _gdn_chunked(qn, kn, vv, gb, betab, h3, lw["norm_g"])
    h1 = _out_proj(attn.reshape(b * t, ATTN_W), gdn.reshape(b * t, GDN_W), scm.reshape(b * t, SC_W),
                   x2, lw["w_out"], layer, lw["ln1_g"], lw["ln1_b"], min(tm_out, b * t))
    y = _ffn(h1, lw["w_gate"], lw["w_up"], lw["w_down"], layer, lw["ln2_g"], lw["ln2_b"],
             min(tm_ffn, b * t), tf)
    outs = dict(
        stacks=(k_st, v_st, ik_st),
        s=s_new,
        gconv=h3[:, t - (GDN_CONV - 1):, C_GQKV:C_GQKV + GDN_CONV_DIM],
        sconv=utail[:, 8 - (SC_CONV - 1):, :],
        attn=attn, gdn=gdn, scm=scm,
    )
    return y.reshape(b, t, D_MODEL), outs


def _sample_prep_kernel(h, c128, s128, c64, s64, gst, sst, s_in, gw, sw, alog, dtb, norm_g,
                        q_o, k_o, iq_o, ik_o, iw_o, gst_o, sst_o, s_o, gdn_o, scm_o, o_buf, *, nb):
    cos = c128[...]
    sin = s128[...]
    for hh in range(N_Q):
        sl = slice(hh * HEAD_DIM, (hh + 1) * HEAD_DIM)
        q_o[:, sl] = _rope128(h[:, C_AQ + hh * HEAD_DIM:C_AQ + (hh + 1) * HEAD_DIM], cos, sin) * (HEAD_DIM ** -0.5)
    for hh in range(N_KV):
        sl = slice(hh * HEAD_DIM, (hh + 1) * HEAD_DIM)
        k_o[:, sl] = _rope128(h[:, C_AK + hh * HEAD_DIM:C_AK + (hh + 1) * HEAD_DIM], cos, sin)
    cos = c64[...]
    sin = s64[...]
    for j in range(IDX_H * IDX_D // 128):
        sl = slice(j * 128, (j + 1) * 128)
        iq_o[:, sl] = _rope64x2(h[:, C_IQ + j * 128:C_IQ + (j + 1) * 128], cos, sin) * (IDX_D ** -0.5)
    small = h[:, C_SM:C_SM + 128]
    ik_o[...] = _rope64x2(small, cos, sin)[:, SM_IK:SM_IK + IDX_D]
    iw_o[...] = small[:, SM_IW:SM_IW + IDX_H] * (IDX_H ** -0.5)

    gq = h[:, C_GQKV:C_GQKV + GDN_CONV_DIM]
    y = gq * gw[GDN_CONV - 1:GDN_CONV, :]
    for j in range(GDN_CONV - 1):
        y = y + gst[j] * gw[j:j + 1, :]
    for j in range(GDN_CONV - 2):
        gst_o[j] = gst[j + 1]
    gst_o[GDN_CONV - 2] = gq
    qs, ks, v, gs, bs = _gdn_activations(y, small, alog[...], dtb[...])
    for b in range(nb):
        for hh in range(GDN_H):
            sl = slice(hh * GDN_D, (hh + 1) * GDN_D)
            kc = jnp.broadcast_to(ks[hh][b:b + 1, :], (GDN_D, GDN_D)).T
            qc = jnp.broadcast_to(qs[hh][b:b + 1, :], (GDN_D, GDN_D)).T
            s = s_in[b, hh] * jnp.exp(gs[hh][b:b + 1, :])
            ks_row = jnp.sum(kc * s, axis=0, keepdims=True)
            delta = (v[b:b + 1, sl] - ks_row) * bs[hh][b:b + 1, :]
            s = s + kc * delta
            s_o[b, hh] = s
            o_buf[b:b + 1, sl] = jnp.sum(qc * s, axis=0, keepdims=True)
    ng = norm_g[...]
    for hh in range(GDN_H):
        sl = slice(hh * GDN_D, (hh + 1) * GDN_D)
        z = h[:, C_GZ + hh * GDN_D:C_GZ + (hh + 1) * GDN_D]
        gdn_o[:, sl] = _gated_rmsnorm(o_buf[:, sl], z, ng).astype(BF)

    u = h[:, C_SC:C_SC + SC_W] * h[:, C_SX:C_SX + SC_W]
    y = u * sw[SC_CONV - 1:SC_CONV, :]
    for j in range(SC_CONV - 1):
        y = y + sst[j] * sw[j:j + 1, :]
    for j in range(SC_CONV - 2):
        sst_o[j] = sst[j + 1]
    sst_o[SC_CONV - 2] = u
    scm_o[...] = (h[:, C_SB:C_SB + SC_W] * y).astype(BF)


def _sample_prep(h, tabs, gst, sst, s_in, gw, sw, alog, dtb, norm_g):
    nb = h.shape[0]
    sds = jax.ShapeDtypeStruct
    return pl.pallas_call(
        functools.partial(_sample_prep_kernel, nb=nb),
        out_shape=[sds((nb, ATTN_W), F32), sds((nb, KV_W), F32), sds((nb, IDX_H * IDX_D), F32),
                   sds((nb, IDX_D), F32), sds((nb, IDX_H), F32),
                   sds(gst.shape, F32), sds(sst.shape, F32), sds(s_in.shape, F32),
                   sds((nb, GDN_W), BF), sds((nb, SC_W), BF)],
        scratch_shapes=[pltpu.VMEM((nb, GDN_W), F32)],
        compiler_params=pltpu.CompilerParams(vmem_limit_bytes=VMEM_LIMIT),
        name="sample_prep",
    )(h, *tabs, gst, sst, s_in, gw, sw, alog, dtb, norm_g)


def _sample_index_kernel(pt, *refs, pps, topk):
    pages = refs[:pps]
    iq, iw, ikn, expand, bias_o, seln_o, keys = refs[pps:]
    s = pl.program_id(1)
    iqb = iq[...]
    w = jnp.broadcast_to(iw[...], (IDX_H, PAGE))
    for j in range(pps):
        sc = jnp.maximum(_dot(iqb, pages[j][...].astype(BF)), 0.0)
        keys[pl.ds(s * pps + j, 1), :] = _sortable(jnp.sum(sc * w, axis=0, keepdims=True))

    @pl.when(s == pl.num_programs(1) - 1)
    def _():
        prod = iqb.astype(F32) * ikn[...].astype(BF).astype(F32)
        sn = jnp.sum(prod, axis=1, keepdims=True)
        sn = jnp.sum(jnp.maximum(sn, 0.0) * iw[...], axis=0, keepdims=True)
        key_new = jnp.broadcast_to(_sortable(sn), (1, PAGE))
        kall = keys[...]

        def count_ge(cand):
            c = jnp.where(kall >= cand, 1.0, 0.0).sum(axis=0, keepdims=True).sum(axis=1, keepdims=True)
            return jnp.broadcast_to(c, (1, PAGE)) + jnp.where(key_new >= cand, 1.0, 0.0)

        theta = _kth_largest(count_ge, (1, PAGE), float(topk))
        gt = jnp.where(kall > theta, 1.0, 0.0)
        tie = jnp.where(kall == theta, 1.0, 0.0)
        new_gt = jnp.where(key_new > theta, 1.0, 0.0)
        n_gt = jnp.broadcast_to(gt.sum(axis=0, keepdims=True).sum(axis=1, keepdims=True), (1, PAGE)) + new_gt
        quota = float(topk) - n_gt
        ii = lax.broadcasted_iota(I32, (PAGE, PAGE), 0)
        jj = lax.broadcasted_iota(I32, (PAGE, PAGE), 1)
        in_page = _dot(tie.astype(BF), jnp.where(ii < jj, 1.0, 0.0).astype(BF))
        per_page = jnp.broadcast_to(jnp.sum(tie, axis=1, keepdims=True), (kall.shape[0], PAGE))
        npg = kall.shape[0]
        pi = lax.broadcasted_iota(I32, (npg, npg), 0)
        pj = lax.broadcasted_iota(I32, (npg, npg), 1)
        earlier = _dot(jnp.where(pj < pi, 1.0, 0.0).astype(BF), per_page.astype(BF))
        taken = jnp.where(in_page + earlier < quota, tie, 0.0)
        n_tie = jnp.broadcast_to(tie.sum(axis=0, keepdims=True).sum(axis=1, keepdims=True), (1, PAGE))
        new_sel = jnp.where(key_new > theta, 1.0, jnp.where((key_new == theta) & (n_tie < quota), 1.0, 0.0))
        sel = _dot((gt + taken).astype(BF), expand[...])
        bias_o[...] = jnp.where(sel > 0.5, 0.0, MASKED)
        seln_o[...] = jnp.where(new_sel > 0.5, 0.0, MASKED)


def _expand_matrix():
    return jnp.asarray(np.arange(PAGE)[:, None] == (np.arange(PAGE * N_KV)[None, :] // N_KV), BF)


def _sample_index(page_table, kidx_t, layer, iq3, iw3, ikn3, pps=16):
    nb, npages = page_table.shape
    topk = min(TOPK_MAX, (npages * PAGE + 1) // 4)
    page = lambda j: pl.BlockSpec((None, None, IDX_D, PAGE),
                                  lambda b, s, pt, j=j: (layer, pt[b, s * pps + j], 0, 0))
    per_b = lambda shp: pl.BlockSpec((None,) + shp, lambda b, s, pt: (b, 0, 0))
    grid_spec = pltpu.PrefetchScalarGridSpec(
        num_scalar_prefetch=1,
        grid=(nb, npages // pps),
        in_specs=[page(j) for j in range(pps)]
                 + [per_b((IDX_H, IDX_D)), per_b((IDX_H, 1)), per_b((1, IDX_D)),
                    pl.BlockSpec((PAGE, PAGE * N_KV), lambda b, s, pt: (0, 0))],
        out_specs=[per_b((npages, PAGE * N_KV)), per_b((1, PAGE))],
        scratch_shapes=[pltpu.VMEM((npages, PAGE), I32)],
    )
    return pl.pallas_call(
        functools.partial(_sample_index_kernel, pps=pps, topk=topk),
        grid_spec=grid_spec,
        out_shape=[jax.ShapeDtypeStruct((nb, npages, PAGE * N_KV), F32), jax.ShapeDtypeStruct((nb, 1, PAGE), F32)],
        compiler_params=_cp("parallel", "arbitrary"),
        name="sample_index",
    )(page_table, *([kidx_t] * pps), iq3, iw3, ikn3, _expand_matrix())


def _sample_attn_kernel(pt, *refs, pps):
    kp = refs[:pps]
    vp = refs[pps:2 * pps]
    q8, bias, seln, kn8, vn8, o_ref, m_ref, l_ref, acc_ref = refs[2 * pps:]
    s = pl.program_id(1)
    rows = PAGE * N_KV

    @pl.when(s == 0)
    def _():
        m_ref[...] = jnp.full_like(m_ref, NEG_INF)
        l_ref[...] = jnp.zeros_like(l_ref)
        acc_ref[...] = jnp.zeros_like(acc_ref)

    q = q8[...]
    lane_head = lax.broadcasted_iota(I32, (N_Q, rows), 1) % N_KV
    own = lane_head == lax.broadcasted_iota(I32, (N_Q, rows), 0) // (N_Q // N_KV)
    head_mask = jnp.where(own, 0.0, MASKED)
    lg = [_dot_nt(q, kp[j][...].astype(BF)) + (bias[pl.ds(s * pps + j, 1), :] + head_mask) for j in range(pps)]
    m_old = m_ref[...]
    m_new = m_old
    for x in lg:
        m_new = jnp.maximum(m_new, jnp.max(x, axis=1, keepdims=True))
    alpha = jnp.exp(m_old - m_new)
    l_new = l_ref[...] * alpha
    acc = acc_ref[...] * alpha
    for j in range(pps):
        p = jnp.exp(lg[j] - m_new)
        l_new = l_new + jnp.sum(p, axis=1, keepdims=True)
        acc = acc + _dot(p.astype(BF), vp[j][...].astype(BF))
    m_ref[...] = m_new
    l_ref[...] = l_new
    acc_ref[...] = acc

    @pl.when(s == pl.num_programs(1) - 1)
    def _():
        knb = kn8[...].astype(BF).astype(F32)
        vnb = vn8[...].astype(BF).astype(F32)
        x = jnp.sum(q.astype(F32) * knb, axis=1, keepdims=True) + seln[:, 0:1]
        m_fin = jnp.maximum(m_new, x)
        a = jnp.exp(m_new - m_fin)
        p = jnp.exp(x - m_fin)
        l_fin = l_new * a + p
        o_ref[...] = ((acc * a + p.astype(BF).astype(F32) * vnb) / l_fin).astype(BF)


def _sample_attn(page_table, ck2, cv2, layer, q8, bias, seln, kn8, vn8, pps=8):
    nb, npages = page_table.shape
    rows = PAGE * N_KV
    page = lambda j: pl.BlockSpec((None, None, rows, HEAD_DIM),
                                  lambda b, s, pt, j=j: (layer, pt[b, s * pps + j], 0, 0))
    per_b = lambda shp: pl.BlockSpec((None,) + shp, lambda b, s, pt: (b, 0, 0))
    grid_spec = pltpu.PrefetchScalarGridSpec(
        num_scalar_prefetch=1,
        grid=(nb, npages // pps),
        in_specs=[page(j) for j in range(pps)] * 2
                 + [per_b((N_Q, HEAD_DIM)), per_b((npages, rows)), per_b((1, PAGE)),
                    per_b((N_Q, HEAD_DIM)), per_b((N_Q, HEAD_DIM))],
        out_specs=per_b((N_Q, HEAD_DIM)),
        scratch_shapes=[pltpu.VMEM((N_Q, 1), F32), pltpu.VMEM((N_Q, 1), F32), pltpu.VMEM((N_Q, HEAD_DIM), F32)],
    )
    return pl.pallas_call(
        functools.partial(_sample_attn_kernel, pps=pps),
        grid_spec=grid_spec,
        out_shape=jax.ShapeDtypeStruct((nb, N_Q, HEAD_DIM), BF),
        compiler_params=_cp("parallel", "arbitrary"),
        name="sample_attn",
    )(page_table, *([ck2] * pps), *([cv2] * pps), q8, bias, seln, kn8, vn8)


SAMPLE_ROWS = 16


def _sample_layer(xs, tabs, lw, layer, page_table, cache_k, cache_v, cache_kidx, gst, sst, s_in):
    nb = page_table.shape[0]
    h = _matmul(xs, lw["w_in"], layer, SAMPLE_ROWS, 768)[:nb]
    q, k, iq, ik, iw, gst_n, sst_n, s_n, gdn, scm = _sample_prep(
        h, tabs, gst, sst, s_in, lw["gdn_conv_w"], lw["sc_conv_w"], lw["alog"], lw["dtb"], lw["norm_g"])
    v = h[:, C_AV:C_AV + KV_W]
    bias, seln = _sample_index(page_table, cache_kidx, layer, iq.reshape(nb, IDX_H, IDX_D).astype(BF),
                               iw.reshape(nb, IDX_H, 1), ik.reshape(nb, 1, IDX_D))
    own_kv = lambda a: jnp.repeat(a.reshape(nb, N_KV, HEAD_DIM), N_Q // N_KV, axis=1)
    attn = _sample_attn(page_table, cache_k, cache_v, layer, q.reshape(nb, N_Q, HEAD_DIM).astype(BF),
                        bias, seln, own_kv(k), own_kv(v)).reshape(nb, ATTN_W)
    padr = lambda a: jnp.pad(a, ((0, SAMPLE_ROWS - nb), (0, 0)))
    h1 = _out_proj(padr(attn), padr(gdn), padr(scm), xs, lw["w_out"], layer, lw["ln1_g"], lw["ln1_b"],
                   SAMPLE_ROWS)
    y = _ffn(h1, lw["w_gate"], lw["w_up"], lw["w_down"], layer, lw["ln2_g"], lw["ln2_b"], SAMPLE_ROWS, 512)
    outs = dict(k=k.reshape(nb, 1, N_KV, HEAD_DIM), v=v.reshape(nb, 1, N_KV, HEAD_DIM), kidx=ik.reshape(nb, 1, IDX_D),
                s=s_n, gconv=jnp.swapaxes(gst_n, 0, 1), sconv=jnp.swapaxes(sst_n, 0, 1),
                attn=attn, gdn=gdn, scm=scm)
    return y, outs


def kernel(x_prompt, x_sample, cache_k, cache_v, cache_kidx, page_table, state_gdn, state_gdn_conv,
           state_sc_conv, w_in, gdn_conv_w, gdn_a_log, gdn_dt_bias, gdn_norm_g, sc_conv_w, w_out,
           ln1_g, ln1_b, w_gate, w_up, w_down, ln2_g, ln2_b):
    bp, tp, _ = x_prompt.shape
    nb, ts, _ = x_sample.shape
    assert ts == 1
    past = page_table.shape[1] * PAGE
    tabs_p = _rope_tables(jnp.arange(tp))
    tabs_s = _rope_tables(past + jnp.arange(ts))
    w_in_p = _pack_w_in(w_in)
    w_out_b, w_gate_b, w_up_b, w_down_b = (w.astype(BF) for w in (w_out, w_gate, w_up, w_down))
    depth, pool = cache_k.shape[:2]
    cache_k = cache_k.reshape(depth, pool, PAGE * N_KV, HEAD_DIM)
    cache_v = cache_v.reshape(depth, pool, PAGE * N_KV, HEAD_DIM)
    cache_kidx = jnp.swapaxes(cache_kidx, 2, 3)

    xp = x_prompt
    xs = jnp.pad(x_sample.reshape(nb, D_MODEL), ((0, SAMPLE_ROWS - nb), (0, 0)))
    stacks = (jnp.zeros((DEPTH, bp, tp * N_KV, HEAD_DIM), F32), jnp.zeros((DEPTH, bp, tp * N_KV, HEAD_DIM), F32),
              jnp.zeros((DEPTH, bp, IDX_D, tp), F32))
    po, so = [], []
    for l in range(DEPTH):
        lw = dict(layer=l, w_in=w_in_p, gdn_conv_w=gdn_conv_w[l], sc_conv_w=sc_conv_w[l],
                  alog=_head_rows(gdn_a_log[l]), dtb=_head_rows(gdn_dt_bias[l]), norm_g=gdn_norm_g[l][None, :],
                  w_out=w_out_b, ln1_g=ln1_g[:, None, :], ln1_b=ln1_b[:, None, :],
                  w_gate=w_gate_b, w_up=w_up_b, w_down=w_down_b,
                  ln2_g=ln2_g[:, None, :], ln2_b=ln2_b[:, None, :])
        xp, o = _prompt_layer(xp, tabs_p, lw, stacks)
        stacks = o["stacks"]
        po.append(o)
        xs, o = _sample_layer(xs, tabs_s, lw, l, page_table, cache_k, cache_v, cache_kidx,
                              jnp.swapaxes(state_gdn_conv[l], 0, 1), jnp.swapaxes(state_sc_conv[l], 0, 1),
                              state_gdn[l])
        so.append(o)

    st = lambda outs, n: jnp.stack([o[n] for o in outs])
    k_st, v_st, ik_st = stacks
    return (xp, xs[:nb].reshape(nb, ts, D_MODEL),
            k_st.reshape(DEPTH, bp, tp, N_KV, HEAD_DIM), v_st.reshape(DEPTH, bp, tp, N_KV, HEAD_DIM),
            jnp.swapaxes(ik_st, 2, 3), st(so, "k"), st(so, "v"), st(so, "kidx"),
            st(po, "s"), st(so, "s"), st(po, "gconv"), st(so, "gconv"), st(po, "sconv"), st(so, "sconv"))
```

```python
import functools
import math

import numpy as np
import jax
import jax.numpy as jnp
from jax import lax
from jax.experimental import pallas as pl
from jax.experimental.pallas import tpu as pltpu

F32 = jnp.float32
BF = jnp.bfloat16
I32 = jnp.int32

D_MODEL = 2048
DEPTH = 4
PAGE = 128
HEAD_DIM = 128
N_Q = 8
N_KV = 4
ATTN_W = N_Q * HEAD_DIM
KV_W = N_KV * HEAD_DIM
IDX_H = 16
IDX_D = 64
TOPK_MAX = 256
ROPE_THETA = 10000.0
GDN_H = 4
GDN_D = 128
GDN_W = GDN_H * GDN_D
GDN_CONV = 4
GDN_CONV_DIM = 3 * GDN_W
GDN_CHUNK = 64
GDN_CPS = 4
SC_W = 512
SC_CONV = 3
FFN_DIM = 5632
DN_ALPHA = (2 * DEPTH) ** 0.25
LN_EPS = 1e-5
NORM_EPS = 1e-6
IN_SIZES = (ATTN_W, KV_W, KV_W, IDX_H * IDX_D, IDX_D, IDX_H, GDN_CONV_DIM, GDN_W, GDN_H, GDN_H,
            SC_W, SC_W, SC_W)

C_AQ, C_AK, C_AV, C_IQ, C_GQKV, C_GZ, C_SB, C_SC, C_SX, C_SM = (
    0, 1024, 1536, 2048, 3072, 4608, 5120, 5632, 6144, 6656)
SM_IK, SM_IW, SM_GB, SM_GA = 0, 64, 80, 84
IN_PACKED = 6912

NEG_INF = float("-inf")
MASKED = -1e30
Q_SCALE_LOG2 = HEAD_DIM ** -0.5 * math.log2(math.e)
VT_ROWS = HEAD_DIM + 16
SUB = 128
INT_MIN = -2 ** 31
INT_MAX = 2 ** 31 - 1

VMEM_LIMIT = 56 * 1024 * 1024


def _cp(*sem):
    return pltpu.CompilerParams(dimension_semantics=tuple(sem), vmem_limit_bytes=VMEM_LIMIT)


def _sigmoid(x):
    return 1.0 / (1.0 + jnp.exp(-x))


def _softplus(x):
    return jnp.maximum(x, 0.0) + jnp.log1p(jnp.exp(-jnp.abs(x)))


def _dot(a, b):
    return jnp.dot(a, b, preferred_element_type=F32)


def _dot_nt(a, b):
    return lax.dot_general(a, b, (((1,), (1,)), ((), ())), preferred_element_type=F32)


_IN_OFF = np.concatenate([[0], np.cumsum(IN_SIZES)]).tolist()
_O_IK, _O_GQKV, _O_GB, _O_SB, _O_END = _IN_OFF[4], _IN_OFF[6], _IN_OFF[8], _IN_OFF[10], _IN_OFF[13]


PACK_COLS = 256
_PB_G, _PB_S, _PB_SM = C_GQKV // PACK_COLS, C_SB // PACK_COLS, C_SM // PACK_COLS
_SH_G, _SH_S = _O_GQKV - C_GQKV, _O_SB - C_SB


def _pack_kernel(a, b, o):
    j = pl.program_id(1)

    def emit(src):
        o[...] = src.T.astype(BF)

    @pl.when(j < _PB_G)
    def _():
        emit(a[...])

    @pl.when((j >= _PB_G) & (j < _PB_S))
    def _():
        emit(jnp.concatenate([a[_SH_G:, :], b[:_SH_G, :]], axis=0))

    @pl.when((j >= _PB_S) & (j < _PB_SM))
    def _():
        emit(jnp.concatenate([a[_SH_S:, :], b[:_SH_S, :]], axis=0))

    @pl.when(j == _PB_SM)
    def _():
        n_idx = _O_GQKV - _O_IK
        n_gdn = _O_SB - _O_GB
        lo = _O_GB - (_O_GB // 128) * 128
        pad = jnp.zeros((PACK_COLS - n_idx - n_gdn, a.shape[1]), F32)
        emit(jnp.concatenate([a[:n_idx, :], b[lo:lo + n_gdn, :], pad], axis=0))


def _pack_w_in(w_in):
    wt = jnp.swapaxes(w_in, 1, 2)
    depth, n, d = wt.shape
    nblk = IN_PACKED // PACK_COLS
    last_b = (n - 1) // 128

    def a_map(l, j):
        return l, jnp.where(j == _PB_SM, _O_IK // PACK_COLS, j), 0

    def b_map(l, j):
        return l, jnp.where(j == _PB_SM, _O_GB // 128, jnp.minimum(2 * j + 2, last_b)), 0

    return pl.pallas_call(
        _pack_kernel,
        grid=(depth, nblk),
        in_specs=[pl.BlockSpec((None, PACK_COLS, d), a_map), pl.BlockSpec((None, 128, d), b_map)],
        out_specs=pl.BlockSpec((None, d, PACK_COLS), lambda l, j: (l, 0, j)),
        out_shape=jax.ShapeDtypeStruct((depth, d, IN_PACKED), BF),
        compiler_params=_cp("parallel", "parallel"),
        name="pack_w_in",
    )(wt, wt)


def _mm_kernel(x_ref, w_ref, o_ref):
    o_ref[...] = _dot(x_ref[...].astype(BF), w_ref[...])


def _matmul(x, w, layer, tm, tn):
    m, k = x.shape
    n = w.shape[2]
    return pl.pallas_call(
        _mm_kernel,
        grid=(m // tm, n // tn),
        in_specs=[pl.BlockSpec((tm, k), lambda i, j: (i, 0)),
                  pl.BlockSpec((None, k, tn), lambda i, j: (layer, 0, j))],
        out_specs=pl.BlockSpec((tm, tn), lambda i, j: (i, j)),
        out_shape=jax.ShapeDtypeStruct((m, n), F32),
        compiler_params=_cp("parallel", "arbitrary"),
        name="proj_in",
    )(x, w)


def _rope_tables(pos):
    def tab(half):
        inv = ROPE_THETA ** (-jnp.arange(half, dtype=F32) / half)
        ang = pos.astype(F32)[:, None] * inv
        return jnp.cos(ang), jnp.sin(ang)
    c, s = tab(HEAD_DIM // 2)
    c128 = jnp.concatenate([c, c], -1)
    s128 = jnp.concatenate([-s, s], -1)
    c, s = tab(IDX_D // 2)
    c64 = jnp.concatenate([c, c, c, c], -1)
    s64 = jnp.concatenate([-s, s, -s, s], -1)
    return c128, s128, c64, s64


def _rope128(x, cos, sin):
    return x * cos + pltpu.roll(x, HEAD_DIM // 2, 1) * sin


def _rope64x2(x, cos, sin):
    lane = lax.broadcasted_iota(I32, x.shape, 1)
    first = (lane % IDX_D) < (IDX_D // 2)
    partner = jnp.where(first, pltpu.roll(x, 128 - IDX_D // 2, 1), pltpu.roll(x, IDX_D // 2, 1))
    return x * cos + partner * sin


def _attn_prep_kernel(aq, ak, av, iq, sm, c128, s128, c64, s64, k_st, v_st, ik_st,
                      ks_o, vs_o, iks_o, kb_o, vt_o, qt_o, iqt_o, ikb_o, iwt_o, *, tr):
    del k_st, v_st, ik_st
    cos = c128[...]
    sin = s128[...]
    for h in range(N_Q):
        sl = slice(h * HEAD_DIM, (h + 1) * HEAD_DIM)
        y = _rope128(aq[:, sl], cos, sin) * Q_SCALE_LOG2
        qt_o[sl, :] = y.T.astype(BF)
    for h in range(N_KV):
        sl = slice(h * HEAD_DIM, (h + 1) * HEAD_DIM)
        y = _rope128(ak[:, sl], cos, sin)
        v = av[:, sl]
        ks_o[pl.ds(h, tr, stride=N_KV), :] = y
        vs_o[pl.ds(h, tr, stride=N_KV), :] = v
        kb_o[:, sl] = y.astype(BF)
        vt_o[h * VT_ROWS:h * VT_ROWS + HEAD_DIM, :] = v.T.astype(BF)
        vt_o[h * VT_ROWS + HEAD_DIM:(h + 1) * VT_ROWS, :] = jnp.ones((VT_ROWS - HEAD_DIM, tr), BF)
    cos = c64[...]
    sin = s64[...]
    for j in range(IDX_H * IDX_D // 128):
        sl = slice(j * 128, (j + 1) * 128)
        y = _rope64x2(iq[:, sl], cos, sin) * (IDX_D ** -0.5)
        iqt_o[sl, :] = y.T.astype(BF)
    x = sm[...]
    y = _rope64x2(x, cos, sin)
    iks_o[...] = y.T[SM_IK:SM_IK + IDX_D, :]
    ikb_o[...] = y[:, SM_IK:SM_IK + IDX_D].astype(BF)
    iwt_o[...] = x.T[SM_IW:SM_IW + IDX_H, :] * (IDX_H ** -0.5)


def _attn_prep(h3, tabs, tr, layer, k_st, v_st, ik_st):
    b, t, _ = h3.shape
    nr = t // tr
    col = lambda w, off: pl.BlockSpec((None, tr, w), lambda bi, r, o=off // w: (bi, r, o))
    tab = pl.BlockSpec((tr, 128), lambda bi, r: (r, 0))
    anyspec = pl.BlockSpec(memory_space=pl.ANY)
    sds = jax.ShapeDtypeStruct
    return pl.pallas_call(
        functools.partial(_attn_prep_kernel, tr=tr),
        grid=(b, nr),
        in_specs=[col(ATTN_W, C_AQ), col(KV_W, C_AK), col(KV_W, C_AV), col(ATTN_W, C_IQ), col(128, C_SM),
                  tab, tab, tab, tab, anyspec, anyspec, anyspec],
        out_specs=[
            pl.BlockSpec((None, None, tr * N_KV, HEAD_DIM), lambda bi, r: (layer, bi, r, 0)),
            pl.BlockSpec((None, None, tr * N_KV, HEAD_DIM), lambda bi, r: (layer, bi, r, 0)),
            pl.BlockSpec((None, None, IDX_D, tr), lambda bi, r: (layer, bi, 0, r)),
            pl.BlockSpec((None, tr, KV_W), lambda bi, r: (bi, r, 0)),
            pl.BlockSpec((None, None, N_KV * VT_ROWS, tr), lambda bi, r: (bi, r, 0, 0)),
            pl.BlockSpec((None, ATTN_W, tr), lambda bi, r: (bi, 0, r)),
            pl.BlockSpec((None, ATTN_W, tr), lambda bi, r: (bi, 0, r)),
            pl.BlockSpec((None, tr, IDX_D), lambda bi, r: (bi, r, 0)),
            pl.BlockSpec((None, IDX_H, tr), lambda bi, r: (bi, 0, r)),
        ],
        out_shape=[
            sds(k_st.shape, F32), sds(v_st.shape, F32), sds(ik_st.shape, F32),
            sds((b, t, KV_W), BF),
            sds((b, nr, N_KV * VT_ROWS, tr), BF),
            sds((b, ATTN_W, t), BF),
            sds((b, ATTN_W, t), BF),
            sds((b, t, IDX_D), BF),
            sds((b, IDX_H, t), F32),
        ],
        input_output_aliases={9: 0, 10: 1, 11: 2},
        compiler_params=_cp("parallel", "parallel"),
        name="attn_prep",
    )(h3, h3, h3, h3, h3, *tabs, k_st, v_st, ik_st)


def _sortable(x):
    bits = pltpu.bitcast(x, I32)
    return jnp.where(bits >= 0, bits, bits ^ jnp.int32(0x7FFFFFFF))


def _kth_largest(count_ge, shape, k):
    lo = jnp.where(count_ge(jnp.zeros(shape, I32)) >= k, 0, INT_MIN).astype(I32)

    def body(it, lo):
        cand = lo + (jnp.int32(1) << (30 - it))
        return jnp.where(count_ge(cand) >= k, cand, lo)

    return lax.fori_loop(0, 31, body, lo)


def _dsa_prompt_kernel(qt, iqt, iwt, kb, vt, ikb, o_ref, keys, bias, m_ref, l_ref, acc_ref, s_ref, *, tq, topk):
    i = pl.program_id(1)
    nch = i + 1
    row = lax.broadcasted_iota(I32, (tq, tq), 0)
    lane = lax.broadcasted_iota(I32, (tq, tq), 1)

    def off_of(c):
        return pl.multiple_of(c * tq, tq)

    def causal_of(c):
        return (c * tq + row) <= (i * tq + lane)

    def indexer(c, carry):
        off = off_of(c)
        ikc = ikb[pl.ds(off, tq), :]
        acc = jnp.zeros((tq, tq), F32)
        for h in range(IDX_H):
            s = _dot(ikc, iqt[h * IDX_D:(h + 1) * IDX_D, :])
            acc = acc + jnp.maximum(s, 0.0) * iwt[h:h + 1, :]
        acc = jnp.where(causal_of(c), acc, NEG_INF)
        keys[pl.ds(off, tq), :] = _sortable(acc)
        return carry

    lax.fori_loop(0, nch, indexer, 0)

    def count_ge(cand):
        def hits(c):
            hit = jnp.where(keys[pl.ds(off_of(c), tq), :] >= cand, 1, 0)
            parts = [hit[r * 8:(r + 1) * 8] for r in range(tq // 8)]
            while len(parts) > 1:
                parts = [parts[j] + parts[j + 1] for j in range(0, len(parts), 2)]
            return parts[0]

        cnt = lax.fori_loop(0, nch // 2, lambda j, cnt: cnt + (hits(2 * j) + hits(2 * j + 1)),
                            jnp.zeros((8, tq), I32))
        cnt = cnt + lax.cond(nch % 2 == 1, lambda: hits(nch - 1), lambda: jnp.zeros((8, tq), I32))
        return cnt.sum(axis=0, keepdims=True)

    theta = _kth_largest(count_ge, (1, tq), topk)

    def make_bias(c, cnt):
        off = off_of(c)
        sel = keys[pl.ds(off, tq), :] >= theta
        bias[pl.ds(off, tq), :] = jnp.where(causal_of(c), jnp.where(sel, 0.0, MASKED), MASKED)
        parts = [jnp.where(sel[r * 8:(r + 1) * 8], 1, 0) for r in range(tq // 8)]
        while len(parts) > 1:
            parts = [parts[j] + parts[j + 1] for j in range(0, len(parts), 2)]
        return cnt + parts[0]

    n_ge = lax.fori_loop(0, nch, make_bias, jnp.zeros((8, tq), I32))
    tied = jnp.max(n_ge.sum(axis=0, keepdims=True)) > topk

    @pl.when(tied)
    def _():
        n_gt = jnp.where(theta == INT_MAX, 0, count_ge(jnp.where(theta == INT_MAX, theta, theta + 1)))
        quota = (topk - n_gt).astype(F32)
        before = jnp.where(row > lane, 1.0, 0.0).astype(BF)

        def make_bias(c, seen):
            off = off_of(c)
            kc = keys[pl.ds(off, tq), :]
            tie = jnp.where(kc == theta, 1.0, 0.0)
            rank = seen + _dot(before, tie.astype(BF))
            sel = jnp.where(kc > theta, 1.0, jnp.where(rank < quota, tie, 0.0))
            bias[pl.ds(off, tq), :] = jnp.where(causal_of(c), jnp.where(sel > 0.5, 0.0, MASKED), MASKED)
            return seen + jnp.sum(tie, axis=0, keepdims=True)

        lax.fori_loop(0, nch, make_bias, jnp.zeros((1, tq), F32))

    m_ref[...] = jnp.full_like(m_ref, NEG_INF)
    l_ref[...] = jnp.zeros_like(l_ref)
    acc_ref[...] = jnp.zeros_like(acc_ref)

    def attend(c, carry):
        off = off_of(c)
        m_new = []
        for h in range(N_Q):
            gsl = slice(h // (N_Q // N_KV) * HEAD_DIM, (h // (N_Q // N_KV) + 1) * HEAD_DIM)
            qh = qt[h * HEAD_DIM:(h + 1) * HEAD_DIM, :]
            m8 = jnp.full((8, tq), NEG_INF, F32)
            for sb in range(tq // SUB):
                rows = pl.ds(pl.multiple_of(off + sb * SUB, SUB), SUB)
                s = _dot(kb[rows, gsl], qh) + bias[rows, :]
                s_ref[h, sb * SUB:(sb + 1) * SUB, :] = s
                m8 = jnp.maximum(m8, s.reshape(SUB // 8, 8, tq).max(axis=0))
            m_new.append(jnp.maximum(m_ref[h], m8.max(axis=0, keepdims=True)))
        for h in range(N_Q):
            g = h // (N_Q // N_KV)
            p = jnp.exp2(s_ref[h] - m_new[h]).astype(BF)
            alpha = jnp.exp2(m_ref[h] - m_new[h])
            pv = _dot(vt[c, g * VT_ROWS:(g + 1) * VT_ROWS, :], p)
            acc_ref[h] = acc_ref[h] * alpha + pv[0:HEAD_DIM]
            l_ref[h] = l_ref[h] * alpha + pv[HEAD_DIM:HEAD_DIM + 1]
            m_ref[h] = m_new[h]
        return carry

    lax.fori_loop(0, nch, attend, 0)

    for h in range(N_Q):
        o_ref[:, h * HEAD_DIM:(h + 1) * HEAD_DIM] = (acc_ref[h] * (1.0 / l_ref[h])).T.astype(BF)


def _dsa_prompt(qt, iqt, iwt, kb, vt, ikb, tq):
    b, _, t = qt.shape
    topk = min(TOPK_MAX, t // 4)
    kern = functools.partial(_dsa_prompt_kernel, tq=tq, topk=topk)
    return pl.pallas_call(
        kern,
        grid=(b, t // tq),
        in_specs=[
            pl.BlockSpec((None, ATTN_W, tq), lambda bi, i: (bi, 0, i)),
            pl.BlockSpec((None, ATTN_W, tq), lambda bi, i: (bi, 0, i)),
            pl.BlockSpec((None, IDX_H, tq), lambda bi, i: (bi, 0, i)),
            pl.BlockSpec((None, t, KV_W), lambda bi, i: (bi, 0, 0)),
            pl.BlockSpec((None, t // tq, N_KV * VT_ROWS, tq), lambda bi, i: (bi, 0, 0, 0)),
            pl.BlockSpec((None, t, IDX_D), lambda bi, i: (bi, 0, 0)),
        ],
        out_specs=pl.BlockSpec((None, tq, ATTN_W), lambda bi, i: (bi, i, 0)),
        out_shape=jax.ShapeDtypeStruct((b, t, ATTN_W), BF),
        scratch_shapes=[pltpu.VMEM((t, tq), I32), pltpu.VMEM((t, tq), F32),
                        pltpu.VMEM((N_Q, 1, tq), F32), pltpu.VMEM((N_Q, 1, tq), F32),
                        pltpu.VMEM((N_Q, HEAD_DIM, tq), F32), pltpu.VMEM((N_Q, tq, tq), F32)],
        compiler_params=_cp("parallel", "arbitrary"),
        name="dsa_prompt",
    )(qt, iqt, iwt, kb, vt, ikb)


def _l2norm(x):
    return x * lax.rsqrt(jnp.sum(x * x, -1, keepdims=True) + NORM_EPS)


def _gdn_activations(y, small, alog, dtb):
    y = y * _sigmoid(y)
    qs, ks, gs, bs = [], [], [], []
    r = y.shape[0]
    for h in range(GDN_H):
        sl = slice(h * GDN_D, (h + 1) * GDN_D)
        qs.append(_l2norm(y[:, sl]) * (GDN_D ** -0.5))
        ks.append(_l2norm(y[:, GDN_W + h * GDN_D:GDN_W + (h + 1) * GDN_D]))
        b_raw = jnp.broadcast_to(small[:, SM_GB + h:SM_GB + h + 1], (r, GDN_D))
        a_raw = jnp.broadcast_to(small[:, SM_GA + h:SM_GA + h + 1], (r, GDN_D))
        bs.append(_sigmoid(b_raw))
        gs.append(-jnp.exp(alog[:, sl]) * _softplus(a_raw + dtb[:, sl]))
    return qs, ks, y[:, 2 * GDN_W:], gs, bs


def _mix_prep_kernel(gq, gq_prev, sm, sb, sc, sx, sc_prev, sx_prev, gw, sw, alog, dtb,
                     qn_o, kn_o, vv_o, g_o, beta_o, scm_o, utail_o, xbuf, ubuf, *, tr):
    r = pl.program_id(1)
    first = r == 0
    xbuf[0:8, :] = jnp.where(first, 0.0, gq_prev[...])
    xbuf[8:, :] = gq[...]
    y = xbuf[5:5 + tr, :] * gw[0:1, :]
    for j in range(1, GDN_CONV):
        y = y + xbuf[5 + j:5 + j + tr, :] * gw[j:j + 1, :]
    qs, ks, v, gs, bs = _gdn_activations(y, sm[...], alog[...], dtb[...])
    for h in range(GDN_H):
        sl = slice(h * GDN_D, (h + 1) * GDN_D)
        qn_o[:, sl] = qs[h]
        kn_o[:, sl] = ks[h]
        g_o[:, sl] = gs[h]
        beta_o[:, sl] = bs[h]
    vv_o[...] = v

    u = sc[...] * sx[...]
    ubuf[0:8, :] = jnp.where(first, 0.0, sc_prev[...] * sx_prev[...])
    ubuf[8:, :] = u
    y = ubuf[6:6 + tr, :] * sw[0:1, :]
    for j in range(1, SC_CONV):
        y = y + ubuf[6 + j:6 + j + tr, :] * sw[j:j + 1, :]
    scm_o[...] = (sb[...] * y).astype(BF)

    @pl.when(r == pl.num_programs(1) - 1)
    def _():
        utail_o[...] = u[tr - 8:, :]


def _mix_prep(h3, gw, sw, alog, dtb, tr):
    b, t, _ = h3.shape
    nr = t // tr
    col = lambda w, off: pl.BlockSpec((None, tr, w), lambda bi, r, o=off // w: (bi, r, o))
    prev = lambda w, off: pl.BlockSpec(
        (None, 8, w), lambda bi, r, o=off // w: (bi, jnp.maximum(r * (tr // 8) - 1, 0), o))
    full = lambda a: pl.BlockSpec(a.shape, lambda bi, r: (0, 0))
    out = pl.BlockSpec((None, tr, GDN_W), lambda bi, r: (bi, r, 0))
    act = jax.ShapeDtypeStruct((b, t, GDN_W), F32)
    return pl.pallas_call(
        functools.partial(_mix_prep_kernel, tr=tr),
        grid=(b, nr),
        in_specs=[col(GDN_CONV_DIM, C_GQKV), prev(GDN_CONV_DIM, C_GQKV), col(128, C_SM),
                  col(SC_W, C_SB), col(SC_W, C_SC), col(SC_W, C_SX), prev(SC_W, C_SC), prev(SC_W, C_SX),
                  full(gw), full(sw), full(alog), full(dtb)],
        out_specs=[out, out, out, out, out, out,
                   pl.BlockSpec((None, 8, SC_W), lambda bi, r: (bi, 0, 0))],
        out_shape=[act, act, act, act, act,
                   jax.ShapeDtypeStruct((b, t, SC_W), BF),
                   jax.ShapeDtypeStruct((b, 8, SC_W), F32)],
        scratch_shapes=[pltpu.VMEM((tr + 8, GDN_CONV_DIM), F32), pltpu.VMEM((tr + 8, SC_W), F32)],
        compiler_params=_cp("parallel", "arbitrary"),
        name="mix_prep",
    )(h3, h3, h3, h3, h3, h3, h3, h3, gw, sw, alog, dtb)


def _cumsum_rows(x):
    row = lax.broadcasted_iota(I32, x.shape, 0)
    s = 1
    while s < x.shape[0]:
        x = x + jnp.where(row >= s, pltpu.roll(x, s, 0), 0.0)
        s *= 2
    return x


def _split_bf(x):
    hi = x.astype(BF)
    return hi, (x - hi.astype(F32)).astype(BF)


def _dot3(ah, al, bh, bl):
    m = ah.shape[0]
    r = _dot(jnp.concatenate([ah, al], axis=0), bh)
    return r[:m] + r[m:] + _dot(ah, bl)


def _gated_rmsnorm(o, z, norm_g):
    on = o * lax.rsqrt(jnp.mean(o * o, -1, keepdims=True) + NORM_EPS) * norm_g
    return on * (z * _sigmoid(z))


def _gdn_chunk_kernel(qn, kn, vv, gb, betab, z, norm_g, o_ref, s_out, s_ref, *, nb):
    c = pl.program_id(0)
    C = GDN_CHUNK
    W = GDN_H * C

    @pl.when(c == 0)
    def _():
        s_ref[...] = jnp.zeros_like(s_ref)

    ii = lax.broadcasted_iota(I32, (C, W), 0)
    jj = lax.broadcasted_iota(I32, (C, W), 1) % C
    lower = ii >= jj
    strict = ii > jj
    eye = jnp.where(ii == jj, 1.0, 0.0).astype(F32)
    blockmask = (lax.broadcasted_iota(I32, (W, W), 0) // C) == (lax.broadcasted_iota(I32, (W, W), 1) // C)
    zpad = jnp.zeros((GDN_D - C, GDN_D), F32)
    ng = norm_g[...]
    heads = [slice(h * GDN_D, (h + 1) * GDN_D) for h in range(GDN_H)]
    lanes = lambda parts: jnp.concatenate(parts, axis=1)

    blockmask = jnp.where(blockmask, 1.0, 0.0).astype(BF)

    def block_diag(m):
        return jnp.concatenate([m] * GDN_H, axis=0) * blockmask

    st = []
    for b, rows in [(b, slice(j * C, (j + 1) * C)) for b in range(nb) for j in range(GDN_CPS)]:
        q = [qn[b, rows, sl] for sl in heads]
        k = [kn[b, rows, sl] for sl in heads]
        beta = [betab[b, rows, sl] for sl in heads]
        gc = [_cumsum_rows(gb[b, rows, sl]) for sl in heads]
        kbeta = [k[h] * beta[h] for h in range(GDN_H)]
        gcol = lanes([g[:, 0:C] for g in gc])
        grow = lanes([jnp.concatenate([g, g], axis=0).T[0:C, 0:C] for g in gc])
        decay = jnp.where(lower, jnp.exp(jnp.where(lower, gcol - grow, 0.0)), 0.0)
        kk, qk = [], []
        for h in range(GDN_H):
            r = _dot_nt(jnp.concatenate([kbeta[h], q[h]], axis=0).astype(BF), k[h].astype(BF))
            kk.append(r[:C])
            qk.append(r[C:])
        a = jnp.where(strict, lanes(kk) * decay, 0.0)
        intra = lanes(qk) * decay
        rhs = jnp.concatenate(
            [lanes([vv[b, rows, heads[h]] * beta[h], kbeta[h] * jnp.exp(gc[h])]) for h in range(GDN_H)], axis=0)
        st.append(dict(b=b, rows=rows, q=q, k=k, gc=gc, intra=intra, rhs=rhs, nm=-a))

    for d in st:
        nh, nl = _split_bf(d["nm"])
        d["x"] = eye + d["nm"]
        d["p"] = _dot3(nh, nl, block_diag(nh), block_diag(nl))
    step = 4
    while step < C:
        for d in st:
            lh, ll = _split_bf(jnp.concatenate([d["x"], d["p"]], axis=0))
            r = _dot3(lh, ll, block_diag(lh[C:]), block_diag(ll[C:]))
            d["x"] = d["x"] + r[:C]
            d["p"] = r[C:]
        step *= 2
    for d in st:
        xh, xl = _split_bf(d["x"])
        p_hi, p_lo = _split_bf(d["p"])
        th, tl = _split_bf(d["x"] + _dot3(xh, xl, block_diag(p_hi), block_diag(p_lo)))
        rh, rl = _split_bf(d["rhs"])
        d["uw"] = _dot3(block_diag(th), block_diag(tl), rh, rl)

    for d in st:
        b, rows = d["b"], d["rows"]
        for h in range(GDN_H):
            u = d["uw"][h * C:(h + 1) * C, 0:GDN_D]
            w = d["uw"][h * C:(h + 1) * C, GDN_D:]
            gc = d["gc"][h]
            s = s_ref[b, h]
            r = _dot(jnp.concatenate([w, d["q"][h] * jnp.exp(gc)], axis=0).astype(BF), s.astype(BF))
            v_new = u - r[:C]
            o = r[C:] + _dot(d["intra"][:, h * C:(h + 1) * C].astype(BF), v_new.astype(BF))
            g_last = gc[C - 1:C, :]
            kd = d["k"][h] * jnp.exp(g_last - gc)
            kdt = jnp.concatenate([kd, zpad], axis=0).T.astype(BF)
            vnp = jnp.concatenate([v_new, zpad], axis=0).astype(BF)
            s_ref[b, h] = s * jnp.exp(g_last) + _dot(kdt, vnp)
            o_ref[b, rows, heads[h]] = _gated_rmsnorm(o, z[b, rows, heads[h]], ng).astype(BF)

    @pl.when(c == pl.num_programs(0) - 1)
    def _():
        s_out[...] = s_ref[...]


def _gdn_chunked(qn, kn, vv, gb, betab, h3, norm_g):
    b, t, _ = qn.shape
    C = GDN_CHUNK * GDN_CPS
    blk = pl.BlockSpec((b, C, GDN_W), lambda c: (0, c, 0))
    return pl.pallas_call(
        functools.partial(_gdn_chunk_kernel, nb=b),
        grid=(t // C,),
        in_specs=[blk, blk, blk, blk, blk,
                  pl.BlockSpec((b, C, GDN_W), lambda c: (0, c, C_GZ // GDN_W)),
                  pl.BlockSpec((1, GDN_D), lambda c: (0, 0))],
        out_specs=[pl.BlockSpec((b, C, GDN_W), lambda c: (0, c, 0)),
                   pl.BlockSpec((b, GDN_H, GDN_D, GDN_D), lambda c: (0, 0, 0, 0))],
        out_shape=[jax.ShapeDtypeStruct((b, t, GDN_W), BF),
                   jax.ShapeDtypeStruct((b, GDN_H, GDN_D, GDN_D), F32)],
        scratch_shapes=[pltpu.VMEM((b, GDN_H, GDN_D, GDN_D), F32)],
        compiler_params=_cp("arbitrary"),
        name="gdn_chunked",
    )(qn, kn, vv, gb, betab, h3, norm_g)


def _layer_norm(y, g, b):
    mu = jnp.mean(y, -1, keepdims=True)
    yc = y - mu
    var = jnp.mean(yc * yc, -1, keepdims=True)
    return yc * lax.rsqrt(var + LN_EPS) * g + b


def _out_proj_kernel(attn, gdn, scm, x, wo, g, b, o_ref):
    y = _dot(attn[...], wo[0:ATTN_W, :])
    y = y + _dot(gdn[...], wo[ATTN_W:ATTN_W + GDN_W, :])
    y = y + _dot(scm[...], wo[ATTN_W + GDN_W:, :])
    o_ref[...] = _layer_norm(DN_ALPHA * x[...] + y, g[...], b[...])


def _out_proj(attn, gdn, scm, x, wo, layer, g, b, tm):
    m = x.shape[0]
    row = lambda w: pl.BlockSpec((tm, w), lambda i: (i, 0))
    full = lambda a: pl.BlockSpec((None,) + a.shape[1:], lambda i: (layer, 0, 0))
    return pl.pallas_call(
        _out_proj_kernel,
        grid=(m // tm,),
        in_specs=[row(ATTN_W), row(GDN_W), row(SC_W), row(D_MODEL), full(wo), full(g), full(b)],
        out_specs=row(D_MODEL),
        out_shape=jax.ShapeDtypeStruct((m, D_MODEL), F32),
        compiler_params=_cp("parallel"),
        name="out_proj_ln",
    )(attn, gdn, scm, x, wo, g, b)


def _ffn_kernel(h, wg, wu, wd, g, b, o_ref, hb, acc):
    f = pl.program_id(1)

    @pl.when(f == 0)
    def _():
        hb[...] = h[...].astype(BF)
        acc[...] = jnp.zeros_like(acc)

    x = hb[...]
    gate = _dot(x, wg[...])
    hid = gate * _sigmoid(gate) * _dot(x, wu[...])
    acc[...] += _dot(hid.astype(BF), wd[...])

    @pl.when(f == pl.num_programs(1) - 1)
    def _():
        o_ref[...] = _layer_norm(DN_ALPHA * h[...] + acc[...], g[...], b[...])


def _ffn(h, wg, wu, wd, layer, g, b, tm, tf):
    m = h.shape[0]
    full = lambda a: pl.BlockSpec((None,) + a.shape[1:], lambda i, f: (layer, 0, 0))
    return pl.pallas_call(
        _ffn_kernel,
        grid=(m // tm, FFN_DIM // tf),
        in_specs=[pl.BlockSpec((tm, D_MODEL), lambda i, f: (i, 0)),
                  pl.BlockSpec((None, D_MODEL, tf), lambda i, f: (layer, 0, f)),
                  pl.BlockSpec((None, D_MODEL, tf), lambda i, f: (layer, 0, f)),
                  pl.BlockSpec((None, tf, D_MODEL), lambda i, f: (layer, f, 0)),
                  full(g), full(b)],
        out_specs=pl.BlockSpec((tm, D_MODEL), lambda i, f: (i, 0)),
        out_shape=jax.ShapeDtypeStruct((m, D_MODEL), F32),
        scratch_shapes=[pltpu.VMEM((tm, D_MODEL), BF), pltpu.VMEM((tm, D_MODEL), F32)],
        compiler_params=_cp("parallel", "arbitrary"),
        name="ffn_ln",
    )(h, wg, wu, wd, g, b)


def _head_rows(v):
    return jnp.repeat(v.astype(F32), GDN_D)[None, :]


def _prompt_layer(xp, tabs, lw, stacks, tm_in=1024, tn_in=768, tr=256, tm_out=512, tm_ffn=512, tf=512):
    b, t, _ = xp.shape
    x2 = xp.reshape(b * t, D_MODEL)
    layer = lw["layer"]
    h = _matmul(x2, lw["w_in"], layer, min(tm_in, b * t), tn_in)
    h3 = h.reshape(b, t, IN_PACKED)
    k_st, v_st, ik_st, kb, vt, qt, iqt, ikb, iwt = _attn_prep(h3, tabs, tr, layer, *stacks)
    attn = _dsa_prompt(qt, iqt, iwt, kb, vt, ikb, tr)
    qn, kn, vv, gb, betab, scm, utail = _mix_prep(h3, lw["gdn_conv_w"], lw["sc_conv_w"],
                                                 lw["alog"], lw["dtb"], tr)
    gdn, s_new = _gdn_chunked(qn, kn, vv, gb, betab, h3, lw["norm_g"])
    h1 = _out_proj(attn.reshape(b * t, ATTN_W), gdn.reshape(b * t, GDN_W), scm.reshape(b * t, SC_W),
                   x2, lw["w_out"], layer, lw["ln1_g"], lw["ln1_b"], min(tm_out, b * t))
    y = _ffn(h1, lw["w_gate"], lw["w_up"], lw["w_down"], layer, lw["ln2_g"], lw["ln2_b"],
             min(tm_ffn, b * t), tf)
    outs = dict(
        stacks=(k_st, v_st, ik_st),
        s=s_new,
        gconv=h3[:, t - (GDN_CONV - 1):, C_GQKV:C_GQKV + GDN_CONV_DIM],
        sconv=utail[:, 8 - (SC_CONV - 1):, :],
        attn=attn, gdn=gdn, scm=scm,
    )
    return y.reshape(b, t, D_MODEL), outs


def _sample_prep_kernel(h, c128, s128, c64, s64, gst, sst, s_in, gw, sw, alog, dtb, norm_g,
                        q_o, k_o, iq_o, ik_o, iw_o, gst_o, sst_o, s_o, gdn_o, scm_o, o_buf, *, nb):
    cos = c128[...]
    sin = s128[...]
    for hh in range(N_Q):
        sl = slice(hh * HEAD_DIM, (hh + 1) * HEAD_DIM)
        q_o[:, sl] = _rope128(h[:, C_AQ + hh * HEAD_DIM:C_AQ + (hh + 1) * HEAD_DIM], cos, sin) * (HEAD_DIM ** -0.5)
    for hh in range(N_KV):
        sl = slice(hh * HEAD_DIM, (hh + 1) * HEAD_DIM)
        k_o[:, sl] = _rope128(h[:, C_AK + hh * HEAD_DIM:C_AK + (hh + 1) * HEAD_DIM], cos, sin)
    cos = c64[...]
    sin = s64[...]
    for j in range(IDX_H * IDX_D // 128):
        sl = slice(j * 128, (j + 1) * 128)
        iq_o[:, sl] = _rope64x2(h[:, C_IQ + j * 128:C_IQ + (j + 1) * 128], cos, sin) * (IDX_D ** -0.5)
    small = h[:, C_SM:C_SM + 128]
    ik_o[...] = _rope64x2(small, cos, sin)[:, SM_IK:SM_IK + IDX_D]
    iw_o[...] = small[:, SM_IW:SM_IW + IDX_H] * (IDX_H ** -0.5)

    gq = h[:, C_GQKV:C_GQKV + GDN_CONV_DIM]
    y = gq * gw[GDN_CONV - 1:GDN_CONV, :]
    for j in range(GDN_CONV - 1):
        y = y + gst[j] * gw[j:j + 1, :]
    for j in range(GDN_CONV - 2):
        gst_o[j] = gst[j + 1]
    gst_o[GDN_CONV - 2] = gq
    qs, ks, v, gs, bs = _gdn_activations(y, small, alog[...], dtb[...])
    for b in range(nb):
        for hh in range(GDN_H):
            sl = slice(hh * GDN_D, (hh + 1) * GDN_D)
            kc = jnp.broadcast_to(ks[hh][b:b + 1, :], (GDN_D, GDN_D)).T
            qc = jnp.broadcast_to(qs[hh][b:b + 1, :], (GDN_D, GDN_D)).T
            s = s_in[b, hh] * jnp.exp(gs[hh][b:b + 1, :])
            ks_row = jnp.sum(kc * s, axis=0, keepdims=True)
            delta = (v[b:b + 1, sl] - ks_row) * bs[hh][b:b + 1, :]
            s = s + kc * delta
            s_o[b, hh] = s
            o_buf[b:b + 1, sl] = jnp.sum(qc * s, axis=0, keepdims=True)
    ng = norm_g[...]
    for hh in range(GDN_H):
        sl = slice(hh * GDN_D, (hh + 1) * GDN_D)
        z = h[:, C_GZ + hh * GDN_D:C_GZ + (hh + 1) * GDN_D]
        gdn_o[:, sl] = _gated_rmsnorm(o_buf[:, sl], z, ng).astype(BF)

    u = h[:, C_SC:C_SC + SC_W] * h[:, C_SX:C_SX + SC_W]
    y = u * sw[SC_CONV - 1:SC_CONV, :]
    for j in range(SC_CONV - 1):
        y = y + sst[j] * sw[j:j + 1, :]
    for j in range(SC_CONV - 2):
        sst_o[j] = sst[j + 1]
    sst_o[SC_CONV - 2] = u
    scm_o[...] = (h[:, C_SB:C_SB + SC_W] * y).astype(BF)


def _sample_prep(h, tabs, gst, sst, s_in, gw, sw, alog, dtb, norm_g):
    nb = h.shape[0]
    sds = jax.ShapeDtypeStruct
    return pl.pallas_call(
        functools.partial(_sample_prep_kernel, nb=nb),
        out_shape=[sds((nb, ATTN_W), F32), sds((nb, KV_W), F32), sds((nb, IDX_H * IDX_D), F32),
                   sds((nb, IDX_D), F32), sds((nb, IDX_H), F32),
                   sds(gst.shape, F32), sds(sst.shape, F32), sds(s_in.shape, F32),
                   sds((nb, GDN_W), BF), sds((nb, SC_W), BF)],
        scratch_shapes=[pltpu.VMEM((nb, GDN_W), F32)],
        compiler_params=pltpu.CompilerParams(vmem_limit_bytes=VMEM_LIMIT),
        name="sample_prep",
    )(h, *tabs, gst, sst, s_in, gw, sw, alog, dtb, norm_g)


def _sample_index_kernel(pt, *refs, pps, topk):
    pages = refs[:pps]
    iq, iw, ikn, expand, bias_o, seln_o, keys = refs[pps:]
    s = pl.program_id(1)
    iqb = iq[...]
    w = jnp.broadcast_to(iw[...], (IDX_H, PAGE))
    for j in range(pps):
        sc = jnp.maximum(_dot(iqb, pages[j][...].astype(BF)), 0.0)
        keys[pl.ds(s * pps + j, 1), :] = _sortable(jnp.sum(sc * w, axis=0, keepdims=True))

    @pl.when(s == pl.num_programs(1) - 1)
    def _():
        prod = iqb.astype(F32) * ikn[...].astype(BF).astype(F32)
        sn = jnp.sum(prod, axis=1, keepdims=True)
        sn = jnp.sum(jnp.maximum(sn, 0.0) * iw[...], axis=0, keepdims=True)
        key_new = jnp.broadcast_to(_sortable(sn), (1, PAGE))
        kall = keys[...]

        def count_ge(cand):
            c = jnp.where(kall >= cand, 1.0, 0.0).sum(axis=0, keepdims=True).sum(axis=1, keepdims=True)
            return jnp.broadcast_to(c, (1, PAGE)) + jnp.where(key_new >= cand, 1.0, 0.0)

        theta = _kth_largest(count_ge, (1, PAGE), float(topk))
        gt = jnp.where(kall > theta, 1.0, 0.0)
        tie = jnp.where(kall == theta, 1.0, 0.0)
        new_gt = jnp.where(key_new > theta, 1.0, 0.0)
        n_gt = jnp.broadcast_to(gt.sum(axis=0, keepdims=True).sum(axis=1, keepdims=True), (1, PAGE)) + new_gt
        quota = float(topk) - n_gt
        ii = lax.broadcasted_iota(I32, (PAGE, PAGE), 0)
        jj = lax.broadcasted_iota(I32, (PAGE, PAGE), 1)
        in_page = _dot(tie.astype(BF), jnp.where(ii < jj, 1.0, 0.0).astype(BF))
        per_page = jnp.broadcast_to(jnp.sum(tie, axis=1, keepdims=True), (kall.shape[0], PAGE))
        npg = kall.shape[0]
        pi = lax.broadcasted_iota(I32, (npg, npg), 0)
        pj = lax.broadcasted_iota(I32, (npg, npg), 1)
        earlier = _dot(jnp.where(pj < pi, 1.0, 0.0).astype(BF), per_page.astype(BF))
        taken = jnp.where(in_page + earlier < quota, tie, 0.0)
        n_tie = jnp.broadcast_to(tie.sum(axis=0, keepdims=True).sum(axis=1, keepdims=True), (1, PAGE))
        new_sel = jnp.where(key_new > theta, 1.0, jnp.where((key_new == theta) & (n_tie < quota), 1.0, 0.0))
        sel = _dot((gt + taken).astype(BF), expand[...])
        bias_o[...] = jnp.where(sel > 0.5, 0.0, MASKED)
        seln_o[...] = jnp.where(new_sel > 0.5, 0.0, MASKED)


def _expand_matrix():
    return jnp.asarray(np.arange(PAGE)[:, None] == (np.arange(PAGE * N_KV)[None, :] // N_KV), BF)


def _sample_index(page_table, kidx_t, layer, iq3, iw3, ikn3, pps=16):
    nb, npages = page_table.shape
    topk = min(TOPK_MAX, (npages * PAGE + 1) // 4)
    page = lambda j: pl.BlockSpec((None, None, IDX_D, PAGE),
                                  lambda b, s, pt, j=j: (layer, pt[b, s * pps + j], 0, 0))
    per_b = lambda shp: pl.BlockSpec((None,) + shp, lambda b, s, pt: (b, 0, 0))
    grid_spec = pltpu.PrefetchScalarGridSpec(
        num_scalar_prefetch=1,
        grid=(nb, npages // pps),
        in_specs=[page(j) for j in range(pps)]
                 + [per_b((IDX_H, IDX_D)), per_b((IDX_H, 1)), per_b((1, IDX_D)),
                    pl.BlockSpec((PAGE, PAGE * N_KV), lambda b, s, pt: (0, 0))],
        out_specs=[per_b((npages, PAGE * N_KV)), per_b((1, PAGE))],
        scratch_shapes=[pltpu.VMEM((npages, PAGE), I32)],
    )
    return pl.pallas_call(
        functools.partial(_sample_index_kernel, pps=pps, topk=topk),
        grid_spec=grid_spec,
        out_shape=[jax.ShapeDtypeStruct((nb, npages, PAGE * N_KV), F32), jax.ShapeDtypeStruct((nb, 1, PAGE), F32)],
        compiler_params=_cp("parallel", "arbitrary"),
        name="sample_index",
    )(page_table, *([kidx_t] * pps), iq3, iw3, ikn3, _expand_matrix())


def _sample_attn_kernel(pt, *refs, pps):
    kp = refs[:pps]
    vp = refs[pps:2 * pps]
    q8, bias, seln, kn8, vn8, o_ref, m_ref, l_ref, acc_ref = refs[2 * pps:]
    s = pl.program_id(1)
    rows = PAGE * N_KV

    @pl.when(s == 0)
    def _():
        m_ref[...] = jnp.full_like(m_ref, NEG_INF)
        l_ref[...] = jnp.zeros_like(l_ref)
        acc_ref[...] = jnp.zeros_like(acc_ref)

    q = q8[...]
    lane_head = lax.broadcasted_iota(I32, (N_Q, rows), 1) % N_KV
    own = lane_head == lax.broadcasted_iota(I32, (N_Q, rows), 0) // (N_Q // N_KV)
    head_mask = jnp.where(own, 0.0, MASKED)
    lg = [_dot_nt(q, kp[j][...].astype(BF)) + (bias[pl.ds(s * pps + j, 1), :] + head_mask) for j in range(pps)]
    m_old = m_ref[...]
    m_new = m_old
    for x in lg:
        m_new = jnp.maximum(m_new, jnp.max(x, axis=1, keepdims=True))
    alpha = jnp.exp(m_old - m_new)
    l_new = l_ref[...] * alpha
    acc = acc_ref[...] * alpha
    for j in range(pps):
        p = jnp.exp(lg[j] - m_new)
        l_new = l_new + jnp.sum(p, axis=1, keepdims=True)
        acc = acc + _dot(p.astype(BF), vp[j][...].astype(BF))
    m_ref[...] = m_new
    l_ref[...] = l_new
    acc_ref[...] = acc

    @pl.when(s == pl.num_programs(1) - 1)
    def _():
        knb = kn8[...].astype(BF).astype(F32)
        vnb = vn8[...].astype(BF).astype(F32)
        x = jnp.sum(q.astype(F32) * knb, axis=1, keepdims=True) + seln[:, 0:1]
        m_fin = jnp.maximum(m_new, x)
        a = jnp.exp(m_new - m_fin)
        p = jnp.exp(x - m_fin)
        l_fin = l_new * a + p
        o_ref[...] = ((acc * a + p.astype(BF).astype(F32) * vnb) / l_fin).astype(BF)


def _sample_attn(page_table, ck2, cv2, layer, q8, bias, seln, kn8, vn8, pps=8):
    nb, npages = page_table.shape
    rows = PAGE * N_KV
    page = lambda j: pl.BlockSpec((None, None, rows, HEAD_DIM),
                                  lambda b, s, pt, j=j: (layer, pt[b, s * pps + j], 0, 0))
    per_b = lambda shp: pl.BlockSpec((None,) + shp, lambda b, s, pt: (b, 0, 0))
    grid_spec = pltpu.PrefetchScalarGridSpec(
        num_scalar_prefetch=1,
        grid=(nb, npages // pps),
        in_specs=[page(j) for j in range(pps)] * 2
                 + [per_b((N_Q, HEAD_DIM)), per_b((npages, rows)), per_b((1, PAGE)),
                    per_b((N_Q, HEAD_DIM)), per_b((N_Q, HEAD_DIM))],
        out_specs=per_b((N_Q, HEAD_DIM)),
        scratch_shapes=[pltpu.VMEM((N_Q, 1), F32), pltpu.VMEM((N_Q, 1), F32), pltpu.VMEM((N_Q, HEAD_DIM), F32)],
    )
    return pl.pallas_call(
        functools.partial(_sample_attn_kernel, pps=pps),
        grid_spec=grid_spec,
        out_shape=jax.ShapeDtypeStruct((nb, N_Q, HEAD_DIM), BF),
        compiler_params=_cp("parallel", "arbitrary"),
        name="sample_attn",
    )(page_table, *([ck2] * pps), *([cv2] * pps), q8, bias, seln, kn8, vn8)


SAMPLE_ROWS = 16


def _sample_layer(xs, tabs, lw, layer, page_table, cache_k, cache_v, cache_kidx, gst, sst, s_in):
    nb = page_table.shape[0]
    h = _matmul(xs, lw["w_in"], layer, SAMPLE_ROWS, 768)[:nb]
    q, k, iq, ik, iw, gst_n, sst_n, s_n, gdn, scm = _sample_prep(
        h, tabs, gst, sst, s_in, lw["gdn_conv_w"], lw["sc_conv_w"], lw["alog"], lw["dtb"], lw["norm_g"])
    v = h[:, C_AV:C_AV + KV_W]
    bias, seln = _sample_index(page_table, cache_kidx, layer, iq.reshape(nb, IDX_H, IDX_D).astype(BF),
                               iw.reshape(nb, IDX_H, 1), ik.reshape(nb, 1, IDX_D))
    own_kv = lambda a: jnp.repeat(a.reshape(nb, N_KV, HEAD_DIM), N_Q // N_KV, axis=1)
    attn = _sample_attn(page_table, cache_k, cache_v, layer, q.reshape(nb, N_Q, HEAD_DIM).astype(BF),
                        bias, seln, own_kv(k), own_kv(v)).reshape(nb, ATTN_W)
    padr = lambda a: jnp.pad(a, ((0, SAMPLE_ROWS - nb), (0, 0)))
    h1 = _out_proj(padr(attn), padr(gdn), padr(scm), xs, lw["w_out"], layer, lw["ln1_g"], lw["ln1_b"],
                   SAMPLE_ROWS)
    y = _ffn(h1, lw["w_gate"], lw["w_up"], lw["w_down"], layer, lw["ln2_g"], lw["ln2_b"], SAMPLE_ROWS, 512)
    outs = dict(k=k.reshape(nb, 1, N_KV, HEAD_DIM), v=v.reshape(nb, 1, N_KV, HEAD_DIM), kidx=ik.reshape(nb, 1, IDX_D),
                s=s_n, gconv=jnp.swapaxes(gst_n, 0, 1), sconv=jnp.swapaxes(sst_n, 0, 1),
                attn=attn, gdn=gdn, scm=scm)
    return y, outs


def kernel(x_prompt, x_sample, cache_k, cache_v, cache_kidx, page_table, state_gdn, state_gdn_conv,
           state_sc_conv, w_in, gdn_conv_w, gdn_a_log, gdn_dt_bias, gdn_norm_g, sc_conv_w, w_out,
           ln1_g, ln1_b, w_gate, w_up, w_down, ln2_g, ln2_b):
    bp, tp, _ = x_prompt.shape
    nb, ts, _ = x_sample.shape
    assert ts == 1
    past = page_table.shape[1] * PAGE
    tabs_p = _rope_tables(jnp.arange(tp))
    tabs_s = _rope_tables(past + jnp.arange(ts))
    w_in_p = _pack_w_in(w_in)
    w_out_b, w_gate_b, w_up_b, w_down_b = (w.astype(BF) for w in (w_out, w_gate, w_up, w_down))
    depth, pool = cache_k.shape[:2]
    cache_k = cache_k.reshape(depth, pool, PAGE * N_KV, HEAD_DIM)
    cache_v = cache_v.reshape(depth, pool, PAGE * N_KV, HEAD_DIM)
    cache_kidx = jnp.swapaxes(cache_kidx, 2, 3)

    xp = x_prompt
    xs = jnp.pad(x_sample.reshape(nb, D_MODEL), ((0, SAMPLE_ROWS - nb), (0, 0)))
    stacks = (jnp.zeros((DEPTH, bp, tp * N_KV, HEAD_DIM), F32), jnp.zeros((DEPTH, bp, tp * N_KV, HEAD_DIM), F32),
              jnp.zeros((DEPTH, bp, IDX_D, tp), F32))
    po, so = [], []
    for l in range(DEPTH):
        lw = dict(layer=l, w_in=w_in_p, gdn_conv_w=gdn_conv_w[l], sc_conv_w=sc_conv_w[l],
                  alog=_head_rows(gdn_a_log[l]), dtb=_head_rows(gdn_dt_bias[l]), norm_g=gdn_norm_g[l][None, :],
                  w_out=w_out_b, ln1_g=ln1_g[:, None, :], ln1_b=ln1_b[:, None, :],
                  w_gate=w_gate_b, w_up=w_up_b, w_down=w_down_b,
                  ln2_g=ln2_g[:, None, :], ln2_b=ln2_b[:, None, :])
        xp, o = _prompt_layer(xp, tabs_p, lw, stacks)
        stacks = o["stacks"]
        po.append(o)
        xs, o = _sample_layer(xs, tabs_s, lw, l, page_table, cache_k, cache_v, cache_kidx,
                              jnp.swapaxes(state_gdn_conv[l], 0, 1), jnp.swapaxes(state_sc_conv[l], 0, 1),
                              state_gdn[l])
        so.append(o)

    st = lambda outs, n: jnp.stack([o[n] for o in outs])
    k_st, v_st, ik_st = stacks
    return (xp, xs[:nb].reshape(nb, ts, D_MODEL),
            k_st.reshape(DEPTH, bp, tp, N_KV, HEAD_DIM), v_st.reshape(DEPTH, bp, tp, N_KV, HEAD_DIM),
            jnp.swapaxes(ik_st, 2, 3), st(so, "k"), st(so, "v"), st(so, "kidx"),
            st(po, "s"), st(so, "s"), st(po, "gconv"), st(so, "gconv"), st(po, "sconv"), st(so, "sconv"))
```

```python
import functools
import math

import numpy as np
import jax
import jax.numpy as jnp
from jax import lax
from jax.experimental import pallas as pl
from jax.experimental.pallas import tpu as pltpu

F32 = jnp.float32
BF = jnp.bfloat16
I32 = jnp.int32

D_MODEL = 2048
DEPTH = 4
PAGE = 128
HEAD_DIM = 128
N_Q = 8
N_KV = 4
ATTN_W = N_Q * HEAD_DIM
KV_W = N_KV * HEAD_DIM
IDX_H = 16
IDX_D = 64
TOPK_MAX = 256
ROPE_THETA = 10000.0
GDN_H = 4
GDN_D = 128
GDN_W = GDN_H * GDN_D
GDN_CONV = 4
GDN_CONV_DIM = 3 * GDN_W
GDN_CHUNK = 64
GDN_CPS = 4
SC_W = 512
SC_CONV = 3
FFN_DIM = 5632
DN_ALPHA = (2 * DEPTH) ** 0.25
LN_EPS = 1e-5
NORM_EPS = 1e-6
IN_SIZES = (ATTN_W, KV_W, KV_W, IDX_H * IDX_D, IDX_D, IDX_H, GDN_CONV_DIM, GDN_W, GDN_H, GDN_H,
            SC_W, SC_W, SC_W)

C_AQ, C_AK, C_AV, C_IQ, C_GQKV, C_GZ, C_SB, C_SC, C_SX, C_SM = (
    0, 1024, 1536, 2048, 3072, 4608, 5120, 5632, 6144, 6656)
SM_IK, SM_IW, SM_GB, SM_GA = 0, 64, 80, 84
IN_PACKED = 6912

NEG_INF = float("-inf")
MASKED = -1e30
Q_SCALE_LOG2 = HEAD_DIM ** -0.5 * math.log2(math.e)
VT_ROWS = HEAD_DIM + 16
SUB = 128
LN_SUB = 128
INT_MIN = -2 ** 31
INT_MAX = 2 ** 31 - 1

VMEM_LIMIT = 60 * 1024 * 1024


def _cp(*sem):
    return pltpu.CompilerParams(dimension_semantics=tuple(sem), vmem_limit_bytes=VMEM_LIMIT)


def _sigmoid(x):
    return 1.0 / (1.0 + jnp.exp(-x))


def _softplus(x):
    return jnp.maximum(x, 0.0) + jnp.log1p(jnp.exp(-jnp.abs(x)))


def _dot(a, b):
    return jnp.dot(a, b, preferred_element_type=F32)


def _dot_nt(a, b):
    return lax.dot_general(a, b, (((1,), (1,)), ((), ())), preferred_element_type=F32)


_IN_OFF = np.concatenate([[0], np.cumsum(IN_SIZES)]).tolist()
_O_IK, _O_GQKV, _O_GB, _O_SB, _O_END = _IN_OFF[4], _IN_OFF[6], _IN_OFF[8], _IN_OFF[10], _IN_OFF[13]


PACK_COLS = 256
_PB_G, _PB_S, _PB_SM = C_GQKV // PACK_COLS, C_SB // PACK_COLS, C_SM // PACK_COLS
_SH_G, _SH_S = _O_GQKV - C_GQKV, _O_SB - C_SB


def _pack_kernel(a, b, o):
    j = pl.program_id(1)

    def emit(src):
        o[...] = src.T.astype(BF)

    @pl.when(j < _PB_G)
    def _():
        emit(a[...])

    @pl.when((j >= _PB_G) & (j < _PB_S))
    def _():
        emit(jnp.concatenate([a[_SH_G:, :], b[:_SH_G, :]], axis=0))

    @pl.when((j >= _PB_S) & (j < _PB_SM))
    def _():
        emit(jnp.concatenate([a[_SH_S:, :], b[:_SH_S, :]], axis=0))

    @pl.when(j == _PB_SM)
    def _():
        n_idx = _O_GQKV - _O_IK
        n_gdn = _O_SB - _O_GB
        lo = _O_GB - (_O_GB // 128) * 128
        pad = jnp.zeros((PACK_COLS - n_idx - n_gdn, a.shape[1]), F32)
        emit(jnp.concatenate([a[:n_idx, :], b[lo:lo + n_gdn, :], pad], axis=0))


def _pack_w_in(w_in):
    wt = jnp.swapaxes(w_in, 1, 2)
    depth, n, d = wt.shape
    nblk = IN_PACKED // PACK_COLS
    last_b = (n - 1) // 128

    def a_map(l, j):
        return l, jnp.where(j == _PB_SM, _O_IK // PACK_COLS, j), 0

    def b_map(l, j):
        return l, jnp.where(j == _PB_SM, _O_GB // 128, jnp.minimum(2 * j + 2, last_b)), 0

    return pl.pallas_call(
        _pack_kernel,
        grid=(depth, nblk),
        in_specs=[pl.BlockSpec((None, PACK_COLS, d), a_map), pl.BlockSpec((None, 128, d), b_map)],
        out_specs=pl.BlockSpec((None, d, PACK_COLS), lambda l, j: (l, 0, j)),
        out_shape=jax.ShapeDtypeStruct((depth, d, IN_PACKED), BF),
        compiler_params=_cp("parallel", "parallel"),
        name="pack_w_in",
    )(wt, wt)


def _mm_kernel(x_ref, w_ref, o_ref):
    o_ref[...] = _dot(x_ref[...].astype(BF), w_ref[...])


def _matmul(x, w, layer, tm, tn):
    m, k = x.shape
    n = w.shape[2]
    return pl.pallas_call(
        _mm_kernel,
        grid=(m // tm, n // tn),
        in_specs=[pl.BlockSpec((tm, k), lambda i, j: (i, 0)),
                  pl.BlockSpec((None, k, tn), lambda i, j: (layer, 0, j))],
        out_specs=pl.BlockSpec((tm, tn), lambda i, j: (i, j)),
        out_shape=jax.ShapeDtypeStruct((m, n), F32),
        compiler_params=_cp("parallel", "arbitrary"),
        name="proj_in",
    )(x, w)


def _rope_tables(pos):
    def tab(half):
        inv = ROPE_THETA ** (-jnp.arange(half, dtype=F32) / half)
        ang = pos.astype(F32)[:, None] * inv
        return jnp.cos(ang), jnp.sin(ang)
    c, s = tab(HEAD_DIM // 2)
    c128 = jnp.concatenate([c, c], -1)
    s128 = jnp.concatenate([-s, s], -1)
    c, s = tab(IDX_D // 2)
    c64 = jnp.concatenate([c, c, c, c], -1)
    s64 = jnp.concatenate([-s, s, -s, s], -1)
    return c128, s128, c64, s64


def _rope128(x, cos, sin):
    return x * cos + pltpu.roll(x, HEAD_DIM // 2, 1) * sin


def _rope64x2(x, cos, sin):
    lane = lax.broadcasted_iota(I32, x.shape, 1)
    first = (lane % IDX_D) < (IDX_D // 2)
    partner = jnp.where(first, pltpu.roll(x, 128 - IDX_D // 2, 1), pltpu.roll(x, IDX_D // 2, 1))
    return x * cos + partner * sin


def _attn_prep_kernel(aq, ak, av, iq, sm, c128, s128, c64, s64, k_st, v_st, ik_st,
                      ks_o, vs_o, iks_o, kb_o, vt_o, qt_o, iqt_o, ikb_o, iwt_o, *, tr):
    del k_st, v_st, ik_st
    cos = c128[...]
    sin = s128[...]
    for h in range(N_Q):
        sl = slice(h * HEAD_DIM, (h + 1) * HEAD_DIM)
        y = _rope128(aq[:, sl], cos, sin) * Q_SCALE_LOG2
        qt_o[sl, :] = y.T.astype(BF)
    for h in range(N_KV):
        sl = slice(h * HEAD_DIM, (h + 1) * HEAD_DIM)
        y = _rope128(ak[:, sl], cos, sin)
        v = av[:, sl]
        ks_o[pl.ds(h, tr, stride=N_KV), :] = y
        vs_o[pl.ds(h, tr, stride=N_KV), :] = v
        kb_o[:, sl] = y.astype(BF)
        vt_o[h * VT_ROWS:h * VT_ROWS + HEAD_DIM, :] = v.T.astype(BF)
        vt_o[h * VT_ROWS + HEAD_DIM:(h + 1) * VT_ROWS, :] = jnp.ones((VT_ROWS - HEAD_DIM, tr), BF)
    cos = c64[...]
    sin = s64[...]
    for j in range(IDX_H * IDX_D // 128):
        sl = slice(j * 128, (j + 1) * 128)
        y = _rope64x2(iq[:, sl], cos, sin) * (IDX_D ** -0.5)
        iqt_o[sl, :] = y.T.astype(BF)
    x = sm[...]
    y = _rope64x2(x, cos, sin)
    iks_o[...] = y.T[SM_IK:SM_IK + IDX_D, :]
    ikb_o[...] = y[:, SM_IK:SM_IK + IDX_D].astype(BF)
    iwt_o[...] = x.T[SM_IW:SM_IW + IDX_H, :] * (IDX_H ** -0.5)


def _attn_prep(h3, tabs, tr, layer, k_st, v_st, ik_st):
    b, t, _ = h3.shape
    nr = t // tr
    col = lambda w, off: pl.BlockSpec((None, tr, w), lambda bi, r, o=off // w: (bi, r, o))
    tab = pl.BlockSpec((tr, 128), lambda bi, r: (r, 0))
    anyspec = pl.BlockSpec(memory_space=pl.ANY)
    sds = jax.ShapeDtypeStruct
    return pl.pallas_call(
        functools.partial(_attn_prep_kernel, tr=tr),
        grid=(b, nr),
        in_specs=[col(ATTN_W, C_AQ), col(KV_W, C_AK), col(KV_W, C_AV), col(ATTN_W, C_IQ), col(128, C_SM),
                  tab, tab, tab, tab, anyspec, anyspec, anyspec],
        out_specs=[
            pl.BlockSpec((None, None, tr * N_KV, HEAD_DIM), lambda bi, r: (layer, bi, r, 0)),
            pl.BlockSpec((None, None, tr * N_KV, HEAD_DIM), lambda bi, r: (layer, bi, r, 0)),
            pl.BlockSpec((None, None, IDX_D, tr), lambda bi, r: (layer, bi, 0, r)),
            pl.BlockSpec((None, tr, KV_W), lambda bi, r: (bi, r, 0)),
            pl.BlockSpec((None, None, N_KV * VT_ROWS, tr), lambda bi, r: (bi, r, 0, 0)),
            pl.BlockSpec((None, ATTN_W, tr), lambda bi, r: (bi, 0, r)),
            pl.BlockSpec((None, ATTN_W, tr), lambda bi, r: (bi, 0, r)),
            pl.BlockSpec((None, tr, IDX_D), lambda bi, r: (bi, r, 0)),
            pl.BlockSpec((None, IDX_H, tr), lambda bi, r: (bi, 0, r)),
        ],
        out_shape=[
            sds(k_st.shape, F32), sds(v_st.shape, F32), sds(ik_st.shape, F32),
            sds((b, t, KV_W), BF),
            sds((b, nr, N_KV * VT_ROWS, tr), BF),
            sds((b, ATTN_W, t), BF),
            sds((b, ATTN_W, t), BF),
            sds((b, t, IDX_D), BF),
            sds((b, IDX_H, t), F32),
        ],
        input_output_aliases={9: 0, 10: 1, 11: 2},
        compiler_params=_cp("parallel", "parallel"),
        name="attn_prep",
    )(h3, h3, h3, h3, h3, *tabs, k_st, v_st, ik_st)


def _sortable(x):
    bits = pltpu.bitcast(x, I32)
    return jnp.where(bits >= 0, bits, bits ^ jnp.int32(0x7FFFFFFF))


def _kth_largest(count_ge, shape, k):
    lo = jnp.where(count_ge(jnp.zeros(shape, I32)) >= k, 0, INT_MIN).astype(I32)

    def body(it, lo):
        cand = lo + (jnp.int32(1) << (30 - it))
        return jnp.where(count_ge(cand) >= k, cand, lo)

    return lax.fori_loop(0, 31, body, lo)


def _dsa_prompt_kernel(qt, iqt, iwt, kb, vt, ikb, o_ref, keys, bias, m_ref, l_ref, acc_ref, s_ref, *, tq, topk):
    i = pl.program_id(1)
    nch = i + 1
    row = lax.broadcasted_iota(I32, (tq, tq), 0)
    lane = lax.broadcasted_iota(I32, (tq, tq), 1)

    def off_of(c):
        return pl.multiple_of(c * tq, tq)

    def causal_of(c):
        return (c * tq + row) <= (i * tq + lane)

    def indexer(c, carry):
        off = off_of(c)
        ikc = ikb[pl.ds(off, tq), :]
        acc = jnp.zeros((tq, tq), F32)
        for h in range(IDX_H):
            s = _dot(ikc, iqt[h * IDX_D:(h + 1) * IDX_D, :])
            acc = acc + jnp.maximum(s, 0.0) * iwt[h:h + 1, :]
        acc = jnp.where(causal_of(c), acc, NEG_INF)
        keys[pl.ds(off, tq), :] = _sortable(acc)
        return carry

    lax.fori_loop(0, nch, indexer, 0)

    def count_ge(cand):
        def hits(c):
            hit = jnp.where(keys[pl.ds(off_of(c), tq), :] >= cand, 1, 0)
            parts = [hit[r * 8:(r + 1) * 8] for r in range(tq // 8)]
            while len(parts) > 1:
                parts = [parts[j] + parts[j + 1] for j in range(0, len(parts), 2)]
            return parts[0]

        cnt = lax.fori_loop(0, nch // 2, lambda j, cnt: cnt + (hits(2 * j) + hits(2 * j + 1)),
                            jnp.zeros((8, tq), I32))
        cnt = cnt + lax.cond(nch % 2 == 1, lambda: hits(nch - 1), lambda: jnp.zeros((8, tq), I32))
        return cnt.sum(axis=0, keepdims=True)

    theta = _kth_largest(count_ge, (1, tq), topk)

    def make_bias(c, cnt):
        off = off_of(c)
        sel = keys[pl.ds(off, tq), :] >= theta
        bias[pl.ds(off, tq), :] = jnp.where(causal_of(c), jnp.where(sel, 0.0, MASKED), MASKED)
        parts = [jnp.where(sel[r * 8:(r + 1) * 8], 1, 0) for r in range(tq // 8)]
        while len(parts) > 1:
            parts = [parts[j] + parts[j + 1] for j in range(0, len(parts), 2)]
        return cnt + parts[0]

    n_ge = lax.fori_loop(0, nch, make_bias, jnp.zeros((8, tq), I32))
    tied = jnp.max(n_ge.sum(axis=0, keepdims=True)) > topk

    @pl.when(tied)
    def _():
        n_gt = jnp.where(theta == INT_MAX, 0, count_ge(jnp.where(theta == INT_MAX, theta, theta + 1)))
        quota = (topk - n_gt).astype(F32)
        before = jnp.where(row > lane, 1.0, 0.0).astype(BF)

        def make_bias(c, seen):
            off = off_of(c)
            kc = keys[pl.ds(off, tq), :]
            tie = jnp.where(kc == theta, 1.0, 0.0)
            rank = seen + _dot(before, tie.astype(BF))
            sel = jnp.where(kc > theta, 1.0, jnp.where(rank < quota, tie, 0.0))
            bias[pl.ds(off, tq), :] = jnp.where(causal_of(c), jnp.where(sel > 0.5, 0.0, MASKED), MASKED)
            return seen + jnp.sum(tie, axis=0, keepdims=True)

        lax.fori_loop(0, nch, make_bias, jnp.zeros((1, tq), F32))

    m_ref[...] = jnp.full_like(m_ref, NEG_INF)
    l_ref[...] = jnp.zeros_like(l_ref)
    acc_ref[...] = jnp.zeros_like(acc_ref)

    def attend(c, carry):
        off = off_of(c)
        m_new = []
        for h in range(N_Q):
            gsl = slice(h // (N_Q // N_KV) * HEAD_DIM, (h // (N_Q // N_KV) + 1) * HEAD_DIM)
            qh = qt[h * HEAD_DIM:(h + 1) * HEAD_DIM, :]
            m8 = jnp.full((8, tq), NEG_INF, F32)
            for sb in range(tq // SUB):
                rows = pl.ds(pl.multiple_of(off + sb * SUB, SUB), SUB)
                s = _dot(kb[rows, gsl], qh) + bias[rows, :]
                s_ref[h, sb * SUB:(sb + 1) * SUB, :] = s
                m8 = jnp.maximum(m8, s.reshape(SUB // 8, 8, tq).max(axis=0))
            m_new.append(jnp.maximum(m_ref[h], m8.max(axis=0, keepdims=True)))
        for h in range(N_Q):
            g = h // (N_Q // N_KV)
            p = jnp.exp2(s_ref[h] - m_new[h]).astype(BF)
            alpha = jnp.exp2(m_ref[h] - m_new[h])
            pv = _dot(vt[c, g * VT_ROWS:(g + 1) * VT_ROWS, :], p)
            acc_ref[h] = acc_ref[h] * alpha + pv[0:HEAD_DIM]
            l_ref[h] = l_ref[h] * alpha + pv[HEAD_DIM:HEAD_DIM + 1]
            m_ref[h] = m_new[h]
        return carry

    lax.fori_loop(0, nch, attend, 0)

    for h in range(N_Q):
        o_ref[:, h * HEAD_DIM:(h + 1) * HEAD_DIM] = (acc_ref[h] * (1.0 / l_ref[h])).T.astype(BF)


def _dsa_prompt(qt, iqt, iwt, kb, vt, ikb, tq):
    b, _, t = qt.shape
    topk = min(TOPK_MAX, t // 4)
    kern = functools.partial(_dsa_prompt_kernel, tq=tq, topk=topk)
    return pl.pallas_call(
        kern,
        grid=(b, t // tq),
        in_specs=[
            pl.BlockSpec((None, ATTN_W, tq), lambda bi, i: (bi, 0, i)),
            pl.BlockSpec((None, ATTN_W, tq), lambda bi, i: (bi, 0, i)),
            pl.BlockSpec((None, IDX_H, tq), lambda bi, i: (bi, 0, i)),
            pl.BlockSpec((None, t, KV_W), lambda bi, i: (bi, 0, 0)),
            pl.BlockSpec((None, t // tq, N_KV * VT_ROWS, tq), lambda bi, i: (bi, 0, 0, 0)),
            pl.BlockSpec((None, t, IDX_D), lambda bi, i: (bi, 0, 0)),
        ],
        out_specs=pl.BlockSpec((None, tq, ATTN_W), lambda bi, i: (bi, i, 0)),
        out_shape=jax.ShapeDtypeStruct((b, t, ATTN_W), BF),
        scratch_shapes=[pltpu.VMEM((t, tq), I32), pltpu.VMEM((t, tq), F32),
                        pltpu.VMEM((N_Q, 1, tq), F32), pltpu.VMEM((N_Q, 1, tq), F32),
                        pltpu.VMEM((N_Q, HEAD_DIM, tq), F32), pltpu.VMEM((N_Q, tq, tq), F32)],
        compiler_params=_cp("parallel", "arbitrary"),
        name="dsa_prompt",
    )(qt, iqt, iwt, kb, vt, ikb)


def _l2norm(x):
    return x * lax.rsqrt(jnp.sum(x * x, -1, keepdims=True) + NORM_EPS)


def _gdn_activations(y, small, alog, dtb):
    y = y * _sigmoid(y)
    qs, ks, gs, bs = [], [], [], []
    r = y.shape[0]
    for h in range(GDN_H):
        sl = slice(h * GDN_D, (h + 1) * GDN_D)
        qs.append(_l2norm(y[:, sl]) * (GDN_D ** -0.5))
        ks.append(_l2norm(y[:, GDN_W + h * GDN_D:GDN_W + (h + 1) * GDN_D]))
        b_raw = jnp.broadcast_to(small[:, SM_GB + h:SM_GB + h + 1], (r, GDN_D))
        a_raw = jnp.broadcast_to(small[:, SM_GA + h:SM_GA + h + 1], (r, GDN_D))
        bs.append(_sigmoid(b_raw))
        gs.append(-jnp.exp(alog[:, sl]) * _softplus(a_raw + dtb[:, sl]))
    return qs, ks, y[:, 2 * GDN_W:], gs, bs


def _mix_prep_kernel(gq, gq_prev, sm, sb, sc, sx, sc_prev, sx_prev, gw, sw, alog, dtb,
                     qn_o, kn_o, vv_o, g_o, beta_o, scm_o, utail_o, xbuf, ubuf, *, tr):
    r = pl.program_id(1)
    first = r == 0
    xbuf[0:8, :] = jnp.where(first, 0.0, gq_prev[...])
    xbuf[8:, :] = gq[...]
    y = xbuf[5:5 + tr, :] * gw[0:1, :]
    for j in range(1, GDN_CONV):
        y = y + xbuf[5 + j:5 + j + tr, :] * gw[j:j + 1, :]
    qs, ks, v, gs, bs = _gdn_activations(y, sm[...], alog[...], dtb[...])
    for h in range(GDN_H):
        sl = slice(h * GDN_D, (h + 1) * GDN_D)
        qn_o[:, sl] = qs[h]
        kn_o[:, sl] = ks[h]
        g_o[:, sl] = gs[h]
        beta_o[:, sl] = bs[h]
    vv_o[...] = v

    u = sc[...] * sx[...]
    ubuf[0:8, :] = jnp.where(first, 0.0, sc_prev[...] * sx_prev[...])
    ubuf[8:, :] = u
    y = ubuf[6:6 + tr, :] * sw[0:1, :]
    for j in range(1, SC_CONV):
        y = y + ubuf[6 + j:6 + j + tr, :] * sw[j:j + 1, :]
    scm_o[...] = (sb[...] * y).astype(BF)

    @pl.when(r == pl.num_programs(1) - 1)
    def _():
        utail_o[...] = u[tr - 8:, :]


def _mix_prep(h3, gw, sw, alog, dtb, tr):
    b, t, _ = h3.shape
    nr = t // tr
    col = lambda w, off: pl.BlockSpec((None, tr, w), lambda bi, r, o=off // w: (bi, r, o))
    prev = lambda w, off: pl.BlockSpec(
        (None, 8, w), lambda bi, r, o=off // w: (bi, jnp.maximum(r * (tr // 8) - 1, 0), o))
    full = lambda a: pl.BlockSpec(a.shape, lambda bi, r: (0, 0))
    out = pl.BlockSpec((None, tr, GDN_W), lambda bi, r: (bi, r, 0))
    act = jax.ShapeDtypeStruct((b, t, GDN_W), F32)
    return pl.pallas_call(
        functools.partial(_mix_prep_kernel, tr=tr),
        grid=(b, nr),
        in_specs=[col(GDN_CONV_DIM, C_GQKV), prev(GDN_CONV_DIM, C_GQKV), col(128, C_SM),
                  col(SC_W, C_SB), col(SC_W, C_SC), col(SC_W, C_SX), prev(SC_W, C_SC), prev(SC_W, C_SX),
                  full(gw), full(sw), full(alog), full(dtb)],
        out_specs=[out, out, out, out, out, out,
                   pl.BlockSpec((None, 8, SC_W), lambda bi, r: (bi, 0, 0))],
        out_shape=[act, act, act, act, act,
                   jax.ShapeDtypeStruct((b, t, SC_W), BF),
                   jax.ShapeDtypeStruct((b, 8, SC_W), F32)],
        scratch_shapes=[pltpu.VMEM((tr + 8, GDN_CONV_DIM), F32), pltpu.VMEM((tr + 8, SC_W), F32)],
        compiler_params=_cp("parallel", "arbitrary"),
        name="mix_prep",
    )(h3, h3, h3, h3, h3, h3, h3, h3, gw, sw, alog, dtb)


def _cumsum_rows(x):
    row = lax.broadcasted_iota(I32, x.shape, 0)
    s = 1
    while s < x.shape[0]:
        x = x + jnp.where(row >= s, pltpu.roll(x, s, 0), 0.0)
        s *= 2
    return x


def _split_bf(x):
    hi = x.astype(BF)
    return hi, (x - hi.astype(F32)).astype(BF)


def _dot3(ah, al, bh, bl):
    m = ah.shape[0]
    r = _dot(jnp.concatenate([ah, al], axis=0), bh)
    return r[:m] + r[m:] + _dot(ah, bl)


def _gated_rmsnorm(o, z, norm_g):
    on = o * lax.rsqrt(jnp.mean(o * o, -1, keepdims=True) + NORM_EPS) * norm_g
    return on * (z * _sigmoid(z))


def _gdn_chunk_kernel(qn, kn, vv, gb, betab, z, norm_g, o_ref, s_out, s_ref, *, nb):
    c = pl.program_id(0)
    C = GDN_CHUNK
    W = GDN_H * C

    @pl.when(c == 0)
    def _():
        s_ref[...] = jnp.zeros_like(s_ref)

    ii = lax.broadcasted_iota(I32, (C, W), 0)
    jj = lax.broadcasted_iota(I32, (C, W), 1) % C
    lower = ii >= jj
    strict = ii > jj
    eye = jnp.where(ii == jj, 1.0, 0.0).astype(F32)
    blockmask = (lax.broadcasted_iota(I32, (W, W), 0) // C) == (lax.broadcasted_iota(I32, (W, W), 1) // C)
    zpad = jnp.zeros((GDN_D - C, GDN_D), F32)
    ng = norm_g[...]
    heads = [slice(h * GDN_D, (h + 1) * GDN_D) for h in range(GDN_H)]
    lanes = lambda parts: jnp.concatenate(parts, axis=1)

    blockmask = jnp.where(blockmask, 1.0, 0.0).astype(BF)

    def block_diag(m):
        return jnp.concatenate([m] * GDN_H, axis=0) * blockmask

    st = []
    for b, rows in [(b, slice(j * C, (j + 1) * C)) for b in range(nb) for j in range(GDN_CPS)]:
        q = [qn[b, rows, sl] for sl in heads]
        k = [kn[b, rows, sl] for sl in heads]
        beta = [betab[b, rows, sl] for sl in heads]
        gc = [_cumsum_rows(gb[b, rows, sl]) for sl in heads]
        kbeta = [k[h] * beta[h] for h in range(GDN_H)]
        gcol = lanes([g[:, 0:C] for g in gc])
        grow = lanes([jnp.concatenate([g, g], axis=0).T[0:C, 0:C] for g in gc])
        decay = jnp.where(lower, jnp.exp(jnp.where(lower, gcol - grow, 0.0)), 0.0)
        kk, qk = [], []
        for h in range(GDN_H):
            r = _dot_nt(jnp.concatenate([kbeta[h], q[h]], axis=0).astype(BF), k[h].astype(BF))
            kk.append(r[:C])
            qk.append(r[C:])
        a = jnp.where(strict, lanes(kk) * decay, 0.0)
        intra = lanes(qk) * decay
        rhs = jnp.concatenate(
            [lanes([vv[b, rows, heads[h]] * beta[h], kbeta[h] * jnp.exp(gc[h])]) for h in range(GDN_H)], axis=0)
        st.append(dict(b=b, rows=rows, q=q, k=k, gc=gc, intra=intra, rhs=rhs, nm=-a))

    for d in st:
        nh, nl = _split_bf(d["nm"])
        d["x"] = eye + d["nm"]
        d["p"] = _dot3(nh, nl, block_diag(nh), block_diag(nl))
    step = 4
    while step < C:
        for d in st:
            lh, ll = _split_bf(jnp.concatenate([d["x"], d["p"]], axis=0))
            r = _dot3(lh, ll, block_diag(lh[C:]), block_diag(ll[C:]))
            d["x"] = d["x"] + r[:C]
            d["p"] = r[C:]
        step *= 2
    for d in st:
        xh, xl = _split_bf(d["x"])
        p_hi, p_lo = _split_bf(d["p"])
        th, tl = _split_bf(d["x"] + _dot3(xh, xl, block_diag(p_hi), block_diag(p_lo)))
        rh, rl = _split_bf(d["rhs"])
        d["uw"] = _dot3(block_diag(th), block_diag(tl), rh, rl)

    for d in st:
        b, rows = d["b"], d["rows"]
        for h in range(GDN_H):
            u = d["uw"][h * C:(h + 1) * C, 0:GDN_D]
            w = d["uw"][h * C:(h + 1) * C, GDN_D:]
            gc = d["gc"][h]
            s = s_ref[b, h]
            r = _dot(jnp.concatenate([w, d["q"][h] * jnp.exp(gc)], axis=0).astype(BF), s.astype(BF))
            v_new = u - r[:C]
            o = r[C:] + _dot(d["intra"][:, h * C:(h + 1) * C].astype(BF), v_new.astype(BF))
            g_last = gc[C - 1:C, :]
            kd = d["k"][h] * jnp.exp(g_last - gc)
            kdt = jnp.concatenate([kd, zpad], axis=0).T.astype(BF)
            vnp = jnp.concatenate([v_new, zpad], axis=0).astype(BF)
            s_ref[b, h] = s * jnp.exp(g_last) + _dot(kdt, vnp)
            o_ref[b, rows, heads[h]] = _gated_rmsnorm(o, z[b, rows, heads[h]], ng).astype(BF)

    @pl.when(c == pl.num_programs(0) - 1)
    def _():
        s_out[...] = s_ref[...]


def _gdn_chunked(qn, kn, vv, gb, betab, h3, norm_g):
    b, t, _ = qn.shape
    C = GDN_CHUNK * GDN_CPS
    blk = pl.BlockSpec((b, C, GDN_W), lambda c: (0, c, 0))
    return pl.pallas_call(
        functools.partial(_gdn_chunk_kernel, nb=b),
        grid=(t // C,),
        in_specs=[blk, blk, blk, blk, blk,
                  pl.BlockSpec((b, C, GDN_W), lambda c: (0, c, C_GZ // GDN_W)),
                  pl.BlockSpec((1, GDN_D), lambda c: (0, 0))],
        out_specs=[pl.BlockSpec((b, C, GDN_W), lambda c: (0, c, 0)),
                   pl.BlockSpec((b, GDN_H, GDN_D, GDN_D), lambda c: (0, 0, 0, 0))],
        out_shape=[jax.ShapeDtypeStruct((b, t, GDN_W), BF),
                   jax.ShapeDtypeStruct((b, GDN_H, GDN_D, GDN_D), F32)],
        scratch_shapes=[pltpu.VMEM((b, GDN_H, GDN_D, GDN_D), F32)],
        compiler_params=_cp("arbitrary"),
        name="gdn_chunked",
    )(qn, kn, vv, gb, betab, h3, norm_g)


def _layer_norm(y, g, b):
    mu = jnp.mean(y, -1, keepdims=True)
    yc = y - mu
    var = jnp.mean(yc * yc, -1, keepdims=True)
    return yc * lax.rsqrt(var + LN_EPS) * g + b


def _out_proj_kernel(attn, gdn, scm, x, wo, g, b, o_ref):
    tm = x.shape[0]
    sub = min(tm, LN_SUB)
    for r in range(tm // sub):
        rows = slice(r * sub, (r + 1) * sub)
        y = _dot(attn[rows, :], wo[0:ATTN_W, :])
        y = y + _dot(gdn[rows, :], wo[ATTN_W:ATTN_W + GDN_W, :])
        y = y + _dot(scm[rows, :], wo[ATTN_W + GDN_W:, :])
        o_ref[rows, :] = _layer_norm(DN_ALPHA * x[rows, :] + y, g[...], b[...])


def _out_proj(attn, gdn, scm, x, wo, layer, g, b, tm):
    m = x.shape[0]
    row = lambda w: pl.BlockSpec((tm, w), lambda i: (i, 0))
    full = lambda a: pl.BlockSpec((None,) + a.shape[1:], lambda i: (layer, 0, 0))
    return pl.pallas_call(
        _out_proj_kernel,
        grid=(m // tm,),
        in_specs=[row(ATTN_W), row(GDN_W), row(SC_W), row(D_MODEL), full(wo), full(g), full(b)],
        out_specs=row(D_MODEL),
        out_shape=jax.ShapeDtypeStruct((m, D_MODEL), F32),
        compiler_params=_cp("parallel"),
        name="out_proj_ln",
    )(attn, gdn, scm, x, wo, g, b)


def _ffn_kernel(h, wg, wu, wd, g, b, o_ref, hb, acc):
    f = pl.program_id(1)

    @pl.when(f == 0)
    def _():
        hb[...] = h[...].astype(BF)
        acc[...] = jnp.zeros_like(acc)

    x = hb[...]
    gate = _dot(x, wg[...])
    hid = gate * _sigmoid(gate) * _dot(x, wu[...])
    acc[...] += _dot(hid.astype(BF), wd[...])

    @pl.when(f == pl.num_programs(1) - 1)
    def _():
        o_ref[...] = _layer_norm(DN_ALPHA * h[...] + acc[...], g[...], b[...])


def _ffn(h, wg, wu, wd, layer, g, b, tm, tf):
    m = h.shape[0]
    full = lambda a: pl.BlockSpec((None,) + a.shape[1:], lambda i, f: (layer, 0, 0))
    return pl.pallas_call(
        _ffn_kernel,
        grid=(m // tm, FFN_DIM // tf),
        in_specs=[pl.BlockSpec((tm, D_MODEL), lambda i, f: (i, 0)),
                  pl.BlockSpec((None, D_MODEL, tf), lambda i, f: (layer, 0, f)),
                  pl.BlockSpec((None, D_MODEL, tf), lambda i, f: (layer, 0, f)),
                  pl.BlockSpec((None, tf, D_MODEL), lambda i, f: (layer, f, 0)),
                  full(g), full(b)],
        out_specs=pl.BlockSpec((tm, D_MODEL), lambda i, f: (i, 0)),
        out_shape=jax.ShapeDtypeStruct((m, D_MODEL), F32),
        scratch_shapes=[pltpu.VMEM((tm, D_MODEL), BF), pltpu.VMEM((tm, D_MODEL), F32)],
        compiler_params=_cp("parallel", "arbitrary"),
        name="ffn_ln",
    )(h, wg, wu, wd, g, b)


def _head_rows(v):
    return jnp.repeat(v.astype(F32), GDN_D)[None, :]


def _prompt_layer(xp, tabs, lw, stacks, tm_in=1024, tn_in=768, tr=256, tm_out=512, tm_ffn=1024, tf=256):
    b, t, _ = xp.shape
    x2 = xp.reshape(b * t, D_MODEL)
    layer = lw["layer"]
    h = _matmul(x2, lw["w_in"], layer, min(tm_in, b * t), tn_in)
    h3 = h.reshape(b, t, IN_PACKED)
    k_st, v_st, ik_st, kb, vt, qt, iqt, ikb, iwt = _attn_prep(h3, tabs, tr, layer, *stacks)
    attn = _dsa_prompt(qt, iqt, iwt, kb, vt, ikb, tr)
    qn, kn, vv, gb, betab, scm, utail = _mix_prep(h3, lw["gdn_conv_w"], lw["sc_conv_w"],
                                                 lw["alog"], lw["dtb"], tr)
    gdn, s_new = _gdn_chunked(qn, kn, vv, gb, betab, h3, lw["norm_g"])
    h1 = _out_proj(attn.reshape(b * t, ATTN_W), gdn.reshape(b * t, GDN_W), scm.reshape(b * t, SC_W),
                   x2, lw["w_out"], layer, lw["ln1_g"], lw["ln1_b"], min(tm_out, b * t))
    y = _ffn(h1, lw["w_gate"], lw["w_up"], lw["w_down"], layer, lw["ln2_g"], lw["ln2_b"],
             min(tm_ffn, b * t), tf)
    outs = dict(
        stacks=(k_st, v_st, ik_st),
        s=s_new,
        gconv=h3[:, t - (GDN_CONV - 1):, C_GQKV:C_GQKV + GDN_CONV_DIM],
        sconv=utail[:, 8 - (SC_CONV - 1):, :],
        attn=attn, gdn=gdn, scm=scm,
    )
    return y.reshape(b, t, D_MODEL), outs


def _sample_prep_kernel(h, c128, s128, c64, s64, gst, sst, s_in, gw, sw, alog, dtb, norm_g,
                        q_o, k_o, iq_o, ik_o, iw_o, gst_o, sst_o, s_o, gdn_o, scm_o, o_buf, *, nb):
    cos = c128[...]
    sin = s128[...]
    for hh in range(N_Q):
        sl = slice(hh * HEAD_DIM, (hh + 1) * HEAD_DIM)
        q_o[:, sl] = _rope128(h[:, C_AQ + hh * HEAD_DIM:C_AQ + (hh + 1) * HEAD_DIM], cos, sin) * (HEAD_DIM ** -0.5)
    for hh in range(N_KV):
        sl = slice(hh * HEAD_DIM, (hh + 1) * HEAD_DIM)
        k_o[:, sl] = _rope128(h[:, C_AK + hh * HEAD_DIM:C_AK + (hh + 1) * HEAD_DIM], cos, sin)
    cos = c64[...]
    sin = s64[...]
    for j in range(IDX_H * IDX_D // 128):
        sl = slice(j * 128, (j + 1) * 128)
        iq_o[:, sl] = _rope64x2(h[:, C_IQ + j * 128:C_IQ + (j + 1) * 128], cos, sin) * (IDX_D ** -0.5)
    small = h[:, C_SM:C_SM + 128]
    ik_o[...] = _rope64x2(small, cos, sin)[:, SM_IK:SM_IK + IDX_D]
    iw_o[...] = small[:, SM_IW:SM_IW + IDX_H] * (IDX_H ** -0.5)

    gq = h[:, C_GQKV:C_GQKV + GDN_CONV_DIM]
    y = gq * gw[GDN_CONV - 1:GDN_CONV, :]
    for j in range(GDN_CONV - 1):
        y = y + gst[j] * gw[j:j + 1, :]
    for j in range(GDN_CONV - 2):
        gst_o[j] = gst[j + 1]
    gst_o[GDN_CONV - 2] = gq
    qs, ks, v, gs, bs = _gdn_activations(y, small, alog[...], dtb[...])
    for b in range(nb):
        for hh in range(GDN_H):
            sl = slice(hh * GDN_D, (hh + 1) * GDN_D)
            kc = jnp.broadcast_to(ks[hh][b:b + 1, :], (GDN_D, GDN_D)).T
            qc = jnp.broadcast_to(qs[hh][b:b + 1, :], (GDN_D, GDN_D)).T
            s = s_in[b, hh] * jnp.exp(gs[hh][b:b + 1, :])
            ks_row = jnp.sum(kc * s, axis=0, keepdims=True)
            delta = (v[b:b + 1, sl] - ks_row) * bs[hh][b:b + 1, :]
            s = s + kc * delta
            s_o[b, hh] = s
            o_buf[b:b + 1, sl] = jnp.sum(qc * s, axis=0, keepdims=True)
    ng = norm_g[...]
    for hh in range(GDN_H):
        sl = slice(hh * GDN_D, (hh + 1) * GDN_D)
        z = h[:, C_GZ + hh * GDN_D:C_GZ + (hh + 1) * GDN_D]
        gdn_o[:, sl] = _gated_rmsnorm(o_buf[:, sl], z, ng).astype(BF)

    u = h[:, C_SC:C_SC + SC_W] * h[:, C_SX:C_SX + SC_W]
    y = u * sw[SC_CONV - 1:SC_CONV, :]
    for j in range(SC_CONV - 1):
        y = y + sst[j] * sw[j:j + 1, :]
    for j in range(SC_CONV - 2):
        sst_o[j] = sst[j + 1]
    sst_o[SC_CONV - 2] = u
    scm_o[...] = (h[:, C_SB:C_SB + SC_W] * y).astype(BF)


def _sample_prep(h, tabs, gst, sst, s_in, gw, sw, alog, dtb, norm_g):
    nb = h.shape[0]
    sds = jax.ShapeDtypeStruct
    return pl.pallas_call(
        functools.partial(_sample_prep_kernel, nb=nb),
        out_shape=[sds((nb, ATTN_W), F32), sds((nb, KV_W), F32), sds((nb, IDX_H * IDX_D), F32),
                   sds((nb, IDX_D), F32), sds((nb, IDX_H), F32),
                   sds(gst.shape, F32), sds(sst.shape, F32), sds(s_in.shape, F32),
                   sds((nb, GDN_W), BF), sds((nb, SC_W), BF)],
        scratch_shapes=[pltpu.VMEM((nb, GDN_W), F32)],
        compiler_params=pltpu.CompilerParams(vmem_limit_bytes=VMEM_LIMIT),
        name="sample_prep",
    )(h, *tabs, gst, sst, s_in, gw, sw, alog, dtb, norm_g)


def _sample_index_kernel(pt, *refs, pps, topk):
    pages = refs[:pps]
    iq, iw, ikn, expand, bias_o, seln_o, keys = refs[pps:]
    s = pl.program_id(1)
    iqb = iq[...]
    w = jnp.broadcast_to(iw[...], (IDX_H, PAGE))
    for j in range(pps):
        sc = jnp.maximum(_dot(iqb, pages[j][...].astype(BF)), 0.0)
        keys[pl.ds(s * pps + j, 1), :] = _sortable(jnp.sum(sc * w, axis=0, keepdims=True))

    @pl.when(s == pl.num_programs(1) - 1)
    def _():
        prod = iqb.astype(F32) * ikn[...].astype(BF).astype(F32)
        sn = jnp.sum(prod, axis=1, keepdims=True)
        sn = jnp.sum(jnp.maximum(sn, 0.0) * iw[...], axis=0, keepdims=True)
        key_new = jnp.broadcast_to(_sortable(sn), (1, PAGE))
        kall = keys[...]

        def count_ge(cand):
            c = jnp.where(kall >= cand, 1.0, 0.0).sum(axis=0, keepdims=True).sum(axis=1, keepdims=True)
            return jnp.broadcast_to(c, (1, PAGE)) + jnp.where(key_new >= cand, 1.0, 0.0)

        theta = _kth_largest(count_ge, (1, PAGE), float(topk))
        gt = jnp.where(kall > theta, 1.0, 0.0)
        tie = jnp.where(kall == theta, 1.0, 0.0)
        new_gt = jnp.where(key_new > theta, 1.0, 0.0)
        n_gt = jnp.broadcast_to(gt.sum(axis=0, keepdims=True).sum(axis=1, keepdims=True), (1, PAGE)) + new_gt
        quota = float(topk) - n_gt
        ii = lax.broadcasted_iota(I32, (PAGE, PAGE), 0)
        jj = lax.broadcasted_iota(I32, (PAGE, PAGE), 1)
        in_page = _dot(tie.astype(BF), jnp.where(ii < jj, 1.0, 0.0).astype(BF))
        per_page = jnp.broadcast_to(jnp.sum(tie, axis=1, keepdims=True), (kall.shape[0], PAGE))
        npg = kall.shape[0]
        pi = lax.broadcasted_iota(I32, (npg, npg), 0)
        pj = lax.broadcasted_iota(I32, (npg, npg), 1)
        earlier = _dot(jnp.where(pj < pi, 1.0, 0.0).astype(BF), per_page.astype(BF))
        taken = jnp.where(in_page + earlier < quota, tie, 0.0)
        n_tie = jnp.broadcast_to(tie.sum(axis=0, keepdims=True).sum(axis=1, keepdims=True), (1, PAGE))
        new_sel = jnp.where(key_new > theta, 1.0, jnp.where((key_new == theta) & (n_tie < quota), 1.0, 0.0))
        sel = _dot((gt + taken).astype(BF), expand[...])
        bias_o[...] = jnp.where(sel > 0.5, 0.0, MASKED)
        seln_o[...] = jnp.where(new_sel > 0.5, 0.0, MASKED)


def _expand_matrix():
    return jnp.asarray(np.arange(PAGE)[:, None] == (np.arange(PAGE * N_KV)[None, :] // N_KV), BF)


def _sample_index(page_table, kidx_t, layer, iq3, iw3, ikn3, pps=16):
    nb, npages = page_table.shape
    topk = min(TOPK_MAX, (npages * PAGE + 1) // 4)
    page = lambda j: pl.BlockSpec((None, None, IDX_D, PAGE),
                                  lambda b, s, pt, j=j: (layer, pt[b, s * pps + j], 0, 0))
    per_b = lambda shp: pl.BlockSpec((None,) + shp, lambda b, s, pt: (b, 0, 0))
    grid_spec = pltpu.PrefetchScalarGridSpec(
        num_scalar_prefetch=1,
        grid=(nb, npages // pps),
        in_specs=[page(j) for j in range(pps)]
                 + [per_b((IDX_H, IDX_D)), per_b((IDX_H, 1)), per_b((1, IDX_D)),
                    pl.BlockSpec((PAGE, PAGE * N_KV), lambda b, s, pt: (0, 0))],
        out_specs=[per_b((npages, PAGE * N_KV)), per_b((1, PAGE))],
        scratch_shapes=[pltpu.VMEM((npages, PAGE), I32)],
    )
    return pl.pallas_call(
        functools.partial(_sample_index_kernel, pps=pps, topk=topk),
        grid_spec=grid_spec,
        out_shape=[jax.ShapeDtypeStruct((nb, npages, PAGE * N_KV), F32), jax.ShapeDtypeStruct((nb, 1, PAGE), F32)],
        compiler_params=_cp("parallel", "arbitrary"),
        name="sample_index",
    )(page_table, *([kidx_t] * pps), iq3, iw3, ikn3, _expand_matrix())


def _sample_attn_kernel(pt, *refs, pps):
    kp = refs[:pps]
    vp = refs[pps:2 * pps]
    q8, bias, seln, kn8, vn8, o_ref, m_ref, l_ref, acc_ref = refs[2 * pps:]
    s = pl.program_id(1)
    rows = PAGE * N_KV

    @pl.when(s == 0)
    def _():
        m_ref[...] = jnp.full_like(m_ref, NEG_INF)
        l_ref[...] = jnp.zeros_like(l_ref)
        acc_ref[...] = jnp.zeros_like(acc_ref)

    q = q8[...]
    lane_head = lax.broadcasted_iota(I32, (N_Q, rows), 1) % N_KV
    own = lane_head == lax.broadcasted_iota(I32, (N_Q, rows), 0) // (N_Q // N_KV)
    head_mask = jnp.where(own, 0.0, MASKED)
    lg = [_dot_nt(q, kp[j][...].astype(BF)) + (bias[pl.ds(s * pps + j, 1), :] + head_mask) for j in range(pps)]
    m_old = m_ref[...]
    m_new = m_old
    for x in lg:
        m_new = jnp.maximum(m_new, jnp.max(x, axis=1, keepdims=True))
    alpha = jnp.exp(m_old - m_new)
    l_new = l_ref[...] * alpha
    acc = acc_ref[...] * alpha
    for j in range(pps):
        p = jnp.exp(lg[j] - m_new)
        l_new = l_new + jnp.sum(p, axis=1, keepdims=True)
        acc = acc + _dot(p.astype(BF), vp[j][...].astype(BF))
    m_ref[...] = m_new
    l_ref[...] = l_new
    acc_ref[...] = acc

    @pl.when(s == pl.num_programs(1) - 1)
    def _():
        knb = kn8[...].astype(BF).astype(F32)
        vnb = vn8[...].astype(BF).astype(F32)
        x = jnp.sum(q.astype(F32) * knb, axis=1, keepdims=True) + seln[:, 0:1]
        m_fin = jnp.maximum(m_new, x)
        a = jnp.exp(m_new - m_fin)
        p = jnp.exp(x - m_fin)
        l_fin = l_new * a + p
        o_ref[...] = ((acc * a + p.astype(BF).astype(F32) * vnb) / l_fin).astype(BF)


def _sample_attn(page_table, ck2, cv2, layer, q8, bias, seln, kn8, vn8, pps=8):
    nb, npages = page_table.shape
    rows = PAGE * N_KV
    page = lambda j: pl.BlockSpec((None, None, rows, HEAD_DIM),
                                  lambda b, s, pt, j=j: (layer, pt[b, s * pps + j], 0, 0))
    per_b = lambda shp: pl.BlockSpec((None,) + shp, lambda b, s, pt: (b, 0, 0))
    grid_spec = pltpu.PrefetchScalarGridSpec(
        num_scalar_prefetch=1,
        grid=(nb, npages // pps),
        in_specs=[page(j) for j in range(pps)] * 2
                 + [per_b((N_Q, HEAD_DIM)), per_b((npages, rows)), per_b((1, PAGE)),
                    per_b((N_Q, HEAD_DIM)), per_b((N_Q, HEAD_DIM))],
        out_specs=per_b((N_Q, HEAD_DIM)),
        scratch_shapes=[pltpu.VMEM((N_Q, 1), F32), pltpu.VMEM((N_Q, 1), F32), pltpu.VMEM((N_Q, HEAD_DIM), F32)],
    )
    return pl.pallas_call(
        functools.partial(_sample_attn_kernel, pps=pps),
        grid_spec=grid_spec,
        out_shape=jax.ShapeDtypeStruct((nb, N_Q, HEAD_DIM), BF),
        compiler_params=_cp("parallel", "arbitrary"),
        name="sample_attn",
    )(page_table, *([ck2] * pps), *([cv2] * pps), q8, bias, seln, kn8, vn8)


SAMPLE_ROWS = 16


def _sample_layer(xs, tabs, lw, layer, page_table, cache_k, cache_v, cache_kidx, gst, sst, s_in):
    nb = page_table.shape[0]
    h = _matmul(xs, lw["w_in"], layer, SAMPLE_ROWS, 768)[:nb]
    q, k, iq, ik, iw, gst_n, sst_n, s_n, gdn, scm = _sample_prep(
        h, tabs, gst, sst, s_in, lw["gdn_conv_w"], lw["sc_conv_w"], lw["alog"], lw["dtb"], lw["norm_g"])
    v = h[:, C_AV:C_AV + KV_W]
    bias, seln = _sample_index(page_table, cache_kidx, layer, iq.reshape(nb, IDX_H, IDX_D).astype(BF),
                               iw.reshape(nb, IDX_H, 1), ik.reshape(nb, 1, IDX_D))
    own_kv = lambda a: jnp.repeat(a.reshape(nb, N_KV, HEAD_DIM), N_Q // N_KV, axis=1)
    attn = _sample_attn(page_table, cache_k, cache_v, layer, q.reshape(nb, N_Q, HEAD_DIM).astype(BF),
                        bias, seln, own_kv(k), own_kv(v)).reshape(nb, ATTN_W)
    padr = lambda a: jnp.pad(a, ((0, SAMPLE_ROWS - nb), (0, 0)))
    h1 = _out_proj(padr(attn), padr(gdn), padr(scm), xs, lw["w_out"], layer, lw["ln1_g"], lw["ln1_b"],
                   SAMPLE_ROWS)
    y = _ffn(h1, lw["w_gate"], lw["w_up"], lw["w_down"], layer, lw["ln2_g"], lw["ln2_b"], SAMPLE_ROWS, 512)
    outs = dict(k=k.reshape(nb, 1, N_KV, HEAD_DIM), v=v.reshape(nb, 1, N_KV, HEAD_DIM), kidx=ik.reshape(nb, 1, IDX_D),
                s=s_n, gconv=jnp.swapaxes(gst_n, 0, 1), sconv=jnp.swapaxes(sst_n, 0, 1),
                attn=attn, gdn=gdn, scm=scm)
    return y, outs


def kernel(x_prompt, x_sample, cache_k, cache_v, cache_kidx, page_table, state_gdn, state_gdn_conv,
           state_sc_conv, w_in, gdn_conv_w, gdn_a_log, gdn_dt_bias, gdn_norm_g, sc_conv_w, w_out,
           ln1_g, ln1_b, w_gate, w_up, w_down, ln2_g, ln2_b):
    bp, tp, _ = x_prompt.shape
    nb, ts, _ = x_sample.shape
    assert ts == 1
    past = page_table.shape[1] * PAGE
    tabs_p = _rope_tables(jnp.arange(tp))
    tabs_s = _rope_tables(past + jnp.arange(ts))
    w_in_p = _pack_w_in(w_in)
    w_out_b, w_gate_b, w_up_b, w_down_b = (w.astype(BF) for w in (w_out, w_gate, w_up, w_down))
    depth, pool = cache_k.shape[:2]
    cache_k = cache_k.reshape(depth, pool, PAGE * N_KV, HEAD_DIM)
    cache_v = cache_v.reshape(depth, pool, PAGE * N_KV, HEAD_DIM)
    cache_kidx = jnp.swapaxes(cache_kidx, 2, 3)

    xp = x_prompt
    xs = jnp.pad(x_sample.reshape(nb, D_MODEL), ((0, SAMPLE_ROWS - nb), (0, 0)))
    stacks = (jnp.zeros((DEPTH, bp, tp * N_KV, HEAD_DIM), F32), jnp.zeros((DEPTH, bp, tp * N_KV, HEAD_DIM), F32),
              jnp.zeros((DEPTH, bp, IDX_D, tp), F32))
    po, so = [], []
    for l in range(DEPTH):
        lw = dict(layer=l, w_in=w_in_p, gdn_conv_w=gdn_conv_w[l], sc_conv_w=sc_conv_w[l],
                  alog=_head_rows(gdn_a_log[l]), dtb=_head_rows(gdn_dt_bias[l]), norm_g=gdn_norm_g[l][None, :],
                  w_out=w_out_b, ln1_g=ln1_g[:, None, :], ln1_b=ln1_b[:, None, :],
                  w_gate=w_gate_b, w_up=w_up_b, w_down=w_down_b,
                  ln2_g=ln2_g[:, None, :], ln2_b=ln2_b[:, None, :])
        xp, o = _prompt_layer(xp, tabs_p, lw, stacks)
        stacks = o["stacks"]
        po.append(o)
        xs, o = _sample_layer(xs, tabs_s, lw, l, page_table, cache_k, cache_v, cache_kidx,
                              jnp.swapaxes(state_gdn_conv[l], 0, 1), jnp.swapaxes(state_sc_conv[l], 0, 1),
                              state_gdn[l])
        so.append(o)

    st = lambda outs, n: jnp.stack([o[n] for o in outs])
    k_st, v_st, ik_st = stacks
    return (xp, xs[:nb].reshape(nb, ts, D_MODEL),
            k_st.reshape(DEPTH, bp, tp, N_KV, HEAD_DIM), v_st.reshape(DEPTH, bp, tp, N_KV, HEAD_DIM),
            jnp.swapaxes(ik_st, 2, 3), st(so, "k"), st(so, "v"), st(so, "kidx"),
            st(po, "s"), st(so, "s"), st(po, "gconv"), st(so, "gconv"), st(po, "sconv"), st(so, "sconv"))
```

```python
import functools
import math

import numpy as np
import jax
import jax.numpy as jnp
from jax import lax
from jax.experimental import pallas as pl
from jax.experimental.pallas import tpu as pltpu

F32 = jnp.float32
BF = jnp.bfloat16
I32 = jnp.int32

D_MODEL = 2048
DEPTH = 4
PAGE = 128
HEAD_DIM = 128
N_Q = 8
N_KV = 4
ATTN_W = N_Q * HEAD_DIM
KV_W = N_KV * HEAD_DIM
IDX_H = 16
IDX_D = 64
TOPK_MAX = 256
ROPE_THETA = 10000.0
GDN_H = 4
GDN_D = 128
GDN_W = GDN_H * GDN_D
GDN_CONV = 4
GDN_CONV_DIM = 3 * GDN_W
GDN_CHUNK = 64
GDN_CPS = 4
SC_W = 512
SC_CONV = 3
FFN_DIM = 5632
DN_ALPHA = (2 * DEPTH) ** 0.25
LN_EPS = 1e-5
NORM_EPS = 1e-6
IN_SIZES = (ATTN_W, KV_W, KV_W, IDX_H * IDX_D, IDX_D, IDX_H, GDN_CONV_DIM, GDN_W, GDN_H, GDN_H,
            SC_W, SC_W, SC_W)

C_AQ, C_AK, C_AV, C_IQ, C_GQKV, C_GZ, C_SB, C_SC, C_SX, C_SM = (
    0, 1024, 1536, 2048, 3072, 4608, 5120, 5632, 6144, 6656)
SM_IK, SM_IW, SM_GB, SM_GA = 0, 64, 80, 84
IN_PACKED = 6912

NEG_INF = float("-inf")
MASKED = -1e30
Q_SCALE_LOG2 = HEAD_DIM ** -0.5 * math.log2(math.e)
VT_ROWS = HEAD_DIM + 16
SUB = 128
INT_MIN = -2 ** 31
INT_MAX = 2 ** 31 - 1

VMEM_LIMIT = 56 * 1024 * 1024


def _cp(*sem):
    return pltpu.CompilerParams(dimension_semantics=tuple(sem), vmem_limit_bytes=VMEM_LIMIT)


def _sigmoid(x):
    return 1.0 / (1.0 + jnp.exp(-x))


def _softplus(x):
    return jnp.maximum(x, 0.0) + jnp.log1p(jnp.exp(-jnp.abs(x)))


def _dot(a, b):
    return jnp.dot(a, b, preferred_element_type=F32)


def _dot_nt(a, b):
    return lax.dot_general(a, b, (((1,), (1,)), ((), ())), preferred_element_type=F32)


_IN_OFF = np.concatenate([[0], np.cumsum(IN_SIZES)]).tolist()
_O_IK, _O_GQKV, _O_GB, _O_SB, _O_END = _IN_OFF[4], _IN_OFF[6], _IN_OFF[8], _IN_OFF[10], _IN_OFF[13]


PACK_COLS = 256
_PB_G, _PB_S, _PB_SM = C_GQKV // PACK_COLS, C_SB // PACK_COLS, C_SM // PACK_COLS
_SH_G, _SH_S = _O_GQKV - C_GQKV, _O_SB - C_SB


def _pack_kernel(a, b, o):
    j = pl.program_id(1)

    def emit(src):
        o[...] = src.T.astype(BF)

    @pl.when(j < _PB_G)
    def _():
        emit(a[...])

    @pl.when((j >= _PB_G) & (j < _PB_S))
    def _():
        emit(jnp.concatenate([a[_SH_G:, :], b[:_SH_G, :]], axis=0))

    @pl.when((j >= _PB_S) & (j < _PB_SM))
    def _():
        emit(jnp.concatenate([a[_SH_S:, :], b[:_SH_S, :]], axis=0))

    @pl.when(j == _PB_SM)
    def _():
        n_idx = _O_GQKV - _O_IK
        n_gdn = _O_SB - _O_GB
        lo = _O_GB - (_O_GB // 128) * 128
        pad = jnp.zeros((PACK_COLS - n_idx - n_gdn, a.shape[1]), F32)
        emit(jnp.concatenate([a[:n_idx, :], b[lo:lo + n_gdn, :], pad], axis=0))


def _pack_w_in(w_in):
    wt = jnp.swapaxes(w_in, 1, 2)
    depth, n, d = wt.shape
    nblk = IN_PACKED // PACK_COLS
    last_b = (n - 1) // 128

    def a_map(l, j):
        return l, jnp.where(j == _PB_SM, _O_IK // PACK_COLS, j), 0

    def b_map(l, j):
        return l, jnp.where(j == _PB_SM, _O_GB // 128, jnp.minimum(2 * j + 2, last_b)), 0

    return pl.pallas_call(
        _pack_kernel,
        grid=(depth, nblk),
        in_specs=[pl.BlockSpec((None, PACK_COLS, d), a_map), pl.BlockSpec((None, 128, d), b_map)],
        out_specs=pl.BlockSpec((None, d, PACK_COLS), lambda l, j: (l, 0, j)),
        out_shape=jax.ShapeDtypeStruct((depth, d, IN_PACKED), BF),
        compiler_params=_cp("parallel", "parallel"),
        name="pack_w_in",
    )(wt, wt)


def _mm_kernel(x_ref, w_ref, o_ref):
    o_ref[...] = _dot(x_ref[...].astype(BF), w_ref[...])


def _matmul(x, w, layer, tm, tn):
    m, k = x.shape
    n = w.shape[2]
    return pl.pallas_call(
        _mm_kernel,
        grid=(m // tm, n // tn),
        in_specs=[pl.BlockSpec((tm, k), lambda i, j: (i, 0)),
                  pl.BlockSpec((None, k, tn), lambda i, j: (layer, 0, j))],
        out_specs=pl.BlockSpec((tm, tn), lambda i, j: (i, j)),
        out_shape=jax.ShapeDtypeStruct((m, n), F32),
        compiler_params=_cp("parallel", "arbitrary"),
        name="proj_in",
    )(x, w)


def _rope_tables(pos):
    def tab(half):
        inv = ROPE_THETA ** (-jnp.arange(half, dtype=F32) / half)
        ang = pos.astype(F32)[:, None] * inv
        return jnp.cos(ang), jnp.sin(ang)
    c, s = tab(HEAD_DIM // 2)
    c128 = jnp.concatenate([c, c], -1)
    s128 = jnp.concatenate([-s, s], -1)
    c, s = tab(IDX_D // 2)
    c64 = jnp.concatenate([c, c, c, c], -1)
    s64 = jnp.concatenate([-s, s, -s, s], -1)
    return c128, s128, c64, s64


def _rope128(x, cos, sin):
    return x * cos + pltpu.roll(x, HEAD_DIM // 2, 1) * sin


def _rope64x2(x, cos, sin):
    lane = lax.broadcasted_iota(I32, x.shape, 1)
    first = (lane % IDX_D) < (IDX_D // 2)
    partner = jnp.where(first, pltpu.roll(x, 128 - IDX_D // 2, 1), pltpu.roll(x, IDX_D // 2, 1))
    return x * cos + partner * sin


def _attn_prep_kernel(aq, ak, av, iq, sm, c128, s128, c64, s64, k_st, v_st, ik_st,
                      ks_o, vs_o, iks_o, kb_o, vt_o, qt_o, iqt_o, ikb_o, iwt_o, *, tr):
    del k_st, v_st, ik_st
    cos = c128[...]
    sin = s128[...]
    for h in range(N_Q):
        sl = slice(h * HEAD_DIM, (h + 1) * HEAD_DIM)
        y = _rope128(aq[:, sl], cos, sin) * Q_SCALE_LOG2
        qt_o[sl, :] = y.T.astype(BF)
    for h in range(N_KV):
        sl = slice(h * HEAD_DIM, (h + 1) * HEAD_DIM)
        y = _rope128(ak[:, sl], cos, sin)
        v = av[:, sl]
        ks_o[pl.ds(h, tr, stride=N_KV), :] = y
        vs_o[pl.ds(h, tr, stride=N_KV), :] = v
        kb_o[:, sl] = y.astype(BF)
        vt_o[h * VT_ROWS:h * VT_ROWS + HEAD_DIM, :] = v.T.astype(BF)
        vt_o[h * VT_ROWS + HEAD_DIM:(h + 1) * VT_ROWS, :] = jnp.ones((VT_ROWS - HEAD_DIM, tr), BF)
    cos = c64[...]
    sin = s64[...]
    for j in range(IDX_H * IDX_D // 128):
        sl = slice(j * 128, (j + 1) * 128)
        y = _rope64x2(iq[:, sl], cos, sin) * (IDX_D ** -0.5)
        iqt_o[sl, :] = y.T.astype(BF)
    x = sm[...]
    y = _rope64x2(x, cos, sin)
    iks_o[...] = y.T[SM_IK:SM_IK + IDX_D, :]
    ikb_o[...] = y[:, SM_IK:SM_IK + IDX_D].astype(BF)
    iwt_o[...] = x.T[SM_IW:SM_IW + IDX_H, :] * (IDX_H ** -0.5)


def _attn_prep(h3, tabs, tr, layer, k_st, v_st, ik_st):
    b, t, _ = h3.shape
    nr = t // tr
    col = lambda w, off: pl.BlockSpec((None, tr, w), lambda bi, r, o=off // w: (bi, r, o))
    tab = pl.BlockSpec((tr, 128), lambda bi, r: (r, 0))
    anyspec = pl.BlockSpec(memory_space=pl.ANY)
    sds = jax.ShapeDtypeStruct
    return pl.pallas_call(
        functools.partial(_attn_prep_kernel, tr=tr),
        grid=(b, nr),
        in_specs=[col(ATTN_W, C_AQ), col(KV_W, C_AK), col(KV_W, C_AV), col(ATTN_W, C_IQ), col(128, C_SM),
                  tab, tab, tab, tab, anyspec, anyspec, anyspec],
        out_specs=[
            pl.BlockSpec((None, None, tr * N_KV, HEAD_DIM), lambda bi, r: (layer, bi, r, 0)),
            pl.BlockSpec((None, None, tr * N_KV, HEAD_DIM), lambda bi, r: (layer, bi, r, 0)),
            pl.BlockSpec((None, None, IDX_D, tr), lambda bi, r: (layer, bi, 0, r)),
            pl.BlockSpec((None, tr, KV_W), lambda bi, r: (bi, r, 0)),
            pl.BlockSpec((None, None, N_KV * VT_ROWS, tr), lambda bi, r: (bi, r, 0, 0)),
            pl.BlockSpec((None, ATTN_W, tr), lambda bi, r: (bi, 0, r)),
            pl.BlockSpec((None, ATTN_W, tr), lambda bi, r: (bi, 0, r)),
            pl.BlockSpec((None, tr, IDX_D), lambda bi, r: (bi, r, 0)),
            pl.BlockSpec((None, IDX_H, tr), lambda bi, r: (bi, 0, r)),
        ],
        out_shape=[
            sds(k_st.shape, F32), sds(v_st.shape, F32), sds(ik_st.shape, F32),
            sds((b, t, KV_W), BF),
            sds((b, nr, N_KV * VT_ROWS, tr), BF),
            sds((b, ATTN_W, t), BF),
            sds((b, ATTN_W, t), BF),
            sds((b, t, IDX_D), BF),
            sds((b, IDX_H, t), F32),
        ],
        input_output_aliases={9: 0, 10: 1, 11: 2},
        compiler_params=_cp("parallel", "parallel"),
        name="attn_prep",
    )(h3, h3, h3, h3, h3, *tabs, k_st, v_st, ik_st)


def _sortable(x):
    bits = pltpu.bitcast(x, I32)
    return jnp.where(bits >= 0, bits, bits ^ jnp.int32(0x7FFFFFFF))


def _kth_largest(count_ge, shape, k):
    lo = jnp.where(count_ge(jnp.zeros(shape, I32)) >= k, 0, INT_MIN).astype(I32)

    def body(it, lo):
        cand = lo + (jnp.int32(1) << (30 - it))
        return jnp.where(count_ge(cand) >= k, cand, lo)

    return lax.fori_loop(0, 31, body, lo)


def _dsa_prompt_kernel(qt, iqt, iwt, kb, vt, ikb, o_ref, keys, bias, m_ref, l_ref, acc_ref, s_ref, *, tq, topk):
    i = pl.program_id(1)
    nch = i + 1
    row = lax.broadcasted_iota(I32, (tq, tq), 0)
    lane = lax.broadcasted_iota(I32, (tq, tq), 1)

    def off_of(c):
        return pl.multiple_of(c * tq, tq)

    def causal_of(c):
        return (c * tq + row) <= (i * tq + lane)

    def index_rows(c, n):
        off = off_of(c)
        ikc = ikb[pl.ds(off, n * tq), :]
        acc = jnp.zeros((n * tq, tq), F32)
        for h in range(IDX_H):
            s = _dot(ikc, iqt[h * IDX_D:(h + 1) * IDX_D, :])
            acc = acc + jnp.maximum(s, 0.0) * iwt[h:h + 1, :]
        for j in range(n):
            part = jnp.where(causal_of(c + j), acc[j * tq:(j + 1) * tq], NEG_INF)
            keys[pl.ds(pl.multiple_of(off + j * tq, tq), tq), :] = _sortable(part)

    def indexer2(j, carry):
        index_rows(2 * j, 2)
        return carry

    lax.fori_loop(0, nch // 2, indexer2, 0)

    @pl.when(nch % 2 == 1)
    def _():
        index_rows(nch - 1, 1)

    def count_ge(cand):
        def hits(c):
            hit = jnp.where(keys[pl.ds(off_of(c), tq), :] >= cand, 1, 0)
            parts = [hit[r * 8:(r + 1) * 8] for r in range(tq // 8)]
            while len(parts) > 1:
                parts = [parts[j] + parts[j + 1] for j in range(0, len(parts), 2)]
            return parts[0]

        cnt = lax.fori_loop(0, nch // 2, lambda j, cnt: cnt + (hits(2 * j) + hits(2 * j + 1)),
                            jnp.zeros((8, tq), I32))
        cnt = cnt + lax.cond(nch % 2 == 1, lambda: hits(nch - 1), lambda: jnp.zeros((8, tq), I32))
        return cnt.sum(axis=0, keepdims=True)

    theta = _kth_largest(count_ge, (1, tq), topk)

    def make_bias(c, cnt):
        off = off_of(c)
        sel = keys[pl.ds(off, tq), :] >= theta
        bias[pl.ds(off, tq), :] = jnp.where(causal_of(c), jnp.where(sel, 0.0, MASKED), MASKED)
        parts = [jnp.where(sel[r * 8:(r + 1) * 8], 1, 0) for r in range(tq // 8)]
        while len(parts) > 1:
            parts = [parts[j] + parts[j + 1] for j in range(0, len(parts), 2)]
        return cnt + parts[0]

    n_ge = lax.fori_loop(0, nch, make_bias, jnp.zeros((8, tq), I32))
    tied = jnp.max(n_ge.sum(axis=0, keepdims=True)) > topk

    @pl.when(tied)
    def _():
        n_gt = jnp.where(theta == INT_MAX, 0, count_ge(jnp.where(theta == INT_MAX, theta, theta + 1)))
        quota = (topk - n_gt).astype(F32)
        before = jnp.where(row > lane, 1.0, 0.0).astype(BF)

        def make_bias(c, seen):
            off = off_of(c)
            kc = keys[pl.ds(off, tq), :]
            tie = jnp.where(kc == theta, 1.0, 0.0)
            rank = seen + _dot(before, tie.astype(BF))
            sel = jnp.where(kc > theta, 1.0, jnp.where(rank < quota, tie, 0.0))
            bias[pl.ds(off, tq), :] = jnp.where(causal_of(c), jnp.where(sel > 0.5, 0.0, MASKED), MASKED)
            return seen + jnp.sum(tie, axis=0, keepdims=True)

        lax.fori_loop(0, nch, make_bias, jnp.zeros((1, tq), F32))

    m_ref[...] = jnp.full_like(m_ref, NEG_INF)
    l_ref[...] = jnp.zeros_like(l_ref)
    acc_ref[...] = jnp.zeros_like(acc_ref)

    def attend(c, carry):
        off = off_of(c)
        m_new = []
        for h in range(N_Q):
            gsl = slice(h // (N_Q // N_KV) * HEAD_DIM, (h // (N_Q // N_KV) + 1) * HEAD_DIM)
            qh = qt[h * HEAD_DIM:(h + 1) * HEAD_DIM, :]
            m8 = jnp.full((8, tq), NEG_INF, F32)
            for sb in range(tq // SUB):
                rows = pl.ds(pl.multiple_of(off + sb * SUB, SUB), SUB)
                s = _dot(kb[rows, gsl], qh) + bias[rows, :]
                s_ref[h, sb * SUB:(sb + 1) * SUB, :] = s
                m8 = jnp.maximum(m8, s.reshape(SUB // 8, 8, tq).max(axis=0))
            m_new.append(jnp.maximum(m_ref[h], m8.max(axis=0, keepdims=True)))
        for h in range(N_Q):
            g = h // (N_Q // N_KV)
            p = jnp.exp2(s_ref[h] - m_new[h]).astype(BF)
            alpha = jnp.exp2(m_ref[h] - m_new[h])
            pv = _dot(vt[c, g * VT_ROWS:(g + 1) * VT_ROWS, :], p)
            acc_ref[h] = acc_ref[h] * alpha + pv[0:HEAD_DIM]
            l_ref[h] = l_ref[h] * alpha + pv[HEAD_DIM:HEAD_DIM + 1]
            m_ref[h] = m_new[h]
        return carry

    lax.fori_loop(0, nch, attend, 0)

    for h in range(N_Q):
        o_ref[:, h * HEAD_DIM:(h + 1) * HEAD_DIM] = (acc_ref[h] * (1.0 / l_ref[h])).T.astype(BF)


def _dsa_prompt(qt, iqt, iwt, kb, vt, ikb, tq):
    b, _, t = qt.shape
    topk = min(TOPK_MAX, t // 4)
    kern = functools.partial(_dsa_prompt_kernel, tq=tq, topk=topk)
    return pl.pallas_call(
        kern,
        grid=(b, t // tq),
        in_specs=[
            pl.BlockSpec((None, ATTN_W, tq), lambda bi, i: (bi, 0, i)),
            pl.BlockSpec((None, ATTN_W, tq), lambda bi, i: (bi, 0, i)),
            pl.BlockSpec((None, IDX_H, tq), lambda bi, i: (bi, 0, i)),
            pl.BlockSpec((None, t, KV_W), lambda bi, i: (bi, 0, 0)),
            pl.BlockSpec((None, t // tq, N_KV * VT_ROWS, tq), lambda bi, i: (bi, 0, 0, 0)),
            pl.BlockSpec((None, t, IDX_D), lambda bi, i: (bi, 0, 0)),
        ],
        out_specs=pl.BlockSpec((None, tq, ATTN_W), lambda bi, i: (bi, i, 0)),
        out_shape=jax.ShapeDtypeStruct((b, t, ATTN_W), BF),
        scratch_shapes=[pltpu.VMEM((t, tq), I32), pltpu.VMEM((t, tq), F32),
                        pltpu.VMEM((N_Q, 1, tq), F32), pltpu.VMEM((N_Q, 1, tq), F32),
                        pltpu.VMEM((N_Q, HEAD_DIM, tq), F32), pltpu.VMEM((N_Q, tq, tq), F32)],
        compiler_params=_cp("parallel", "arbitrary"),
        name="dsa_prompt",
    )(qt, iqt, iwt, kb, vt, ikb)


def _l2norm(x):
    return x * lax.rsqrt(jnp.sum(x * x, -1, keepdims=True) + NORM_EPS)


def _gdn_activations(y, small, alog, dtb):
    y = y * _sigmoid(y)
    qs, ks, gs, bs = [], [], [], []
    r = y.shape[0]
    for h in range(GDN_H):
        sl = slice(h * GDN_D, (h + 1) * GDN_D)
        qs.append(_l2norm(y[:, sl]) * (GDN_D ** -0.5))
        ks.append(_l2norm(y[:, GDN_W + h * GDN_D:GDN_W + (h + 1) * GDN_D]))
        b_raw = jnp.broadcast_to(small[:, SM_GB + h:SM_GB + h + 1], (r, GDN_D))
        a_raw = jnp.broadcast_to(small[:, SM_GA + h:SM_GA + h + 1], (r, GDN_D))
        bs.append(_sigmoid(b_raw))
        gs.append(-jnp.exp(alog[:, sl]) * _softplus(a_raw + dtb[:, sl]))
    return qs, ks, y[:, 2 * GDN_W:], gs, bs


def _mix_prep_kernel(gq, gq_prev, sm, sb, sc, sx, sc_prev, sx_prev, gw, sw, alog, dtb,
                     qn_o, kn_o, vv_o, g_o, beta_o, scm_o, utail_o, xbuf, ubuf, *, tr):
    r = pl.program_id(1)
    first = r == 0
    xbuf[0:8, :] = jnp.where(first, 0.0, gq_prev[...])
    xbuf[8:, :] = gq[...]
    y = xbuf[5:5 + tr, :] * gw[0:1, :]
    for j in range(1, GDN_CONV):
        y = y + xbuf[5 + j:5 + j + tr, :] * gw[j:j + 1, :]
    qs, ks, v, gs, bs = _gdn_activations(y, sm[...], alog[...], dtb[...])
    for h in range(GDN_H):
        sl = slice(h * GDN_D, (h + 1) * GDN_D)
        qn_o[:, sl] = qs[h]
        kn_o[:, sl] = ks[h]
        g_o[:, sl] = gs[h]
        beta_o[:, sl] = bs[h]
    vv_o[...] = v

    u = sc[...] * sx[...]
    ubuf[0:8, :] = jnp.where(first, 0.0, sc_prev[...] * sx_prev[...])
    ubuf[8:, :] = u
    y = ubuf[6:6 + tr, :] * sw[0:1, :]
    for j in range(1, SC_CONV):
        y = y + ubuf[6 + j:6 + j + tr, :] * sw[j:j + 1, :]
    scm_o[...] = (sb[...] * y).astype(BF)

    @pl.when(r == pl.num_programs(1) - 1)
    def _():
        utail_o[...] = u[tr - 8:, :]


def _mix_prep(h3, gw, sw, alog, dtb, tr):
    b, t, _ = h3.shape
    nr = t // tr
    col = lambda w, off: pl.BlockSpec((None, tr, w), lambda bi, r, o=off // w: (bi, r, o))
    prev = lambda w, off: pl.BlockSpec(
        (None, 8, w), lambda bi, r, o=off // w: (bi, jnp.maximum(r * (tr // 8) - 1, 0), o))
    full = lambda a: pl.BlockSpec(a.shape, lambda bi, r: (0, 0))
    out = pl.BlockSpec((None, tr, GDN_W), lambda bi, r: (bi, r, 0))
    act = jax.ShapeDtypeStruct((b, t, GDN_W), F32)
    return pl.pallas_call(
        functools.partial(_mix_prep_kernel, tr=tr),
        grid=(b, nr),
        in_specs=[col(GDN_CONV_DIM, C_GQKV), prev(GDN_CONV_DIM, C_GQKV), col(128, C_SM),
                  col(SC_W, C_SB), col(SC_W, C_SC), col(SC_W, C_SX), prev(SC_W, C_SC), prev(SC_W, C_SX),
                  full(gw), full(sw), full(alog), full(dtb)],
        out_specs=[out, out, out, out, out, out,
                   pl.BlockSpec((None, 8, SC_W), lambda bi, r: (bi, 0, 0))],
        out_shape=[act, act, act, act, act,
                   jax.ShapeDtypeStruct((b, t, SC_W), BF),
                   jax.ShapeDtypeStruct((b, 8, SC_W), F32)],
        scratch_shapes=[pltpu.VMEM((tr + 8, GDN_CONV_DIM), F32), pltpu.VMEM((tr + 8, SC_W), F32)],
        compiler_params=_cp("parallel", "arbitrary"),
        name="mix_prep",
    )(h3, h3, h3, h3, h3, h3, h3, h3, gw, sw, alog, dtb)


def _cumsum_rows(x):
    row = lax.broadcasted_iota(I32, x.shape, 0)
    s = 1
    while s < x.shape[0]:
        x = x + jnp.where(row >= s, pltpu.roll(x, s, 0), 0.0)
        s *= 2
    return x


def _split_bf(x):
    hi = x.astype(BF)
    return hi, (x - hi.astype(F32)).astype(BF)


def _dot3(ah, al, bh, bl):
    m = ah.shape[0]
    r = _dot(jnp.concatenate([ah, al], axis=0), bh)
    return r[:m] + r[m:] + _dot(ah, bl)


def _gated_rmsnorm(o, z, norm_g):
    on = o * lax.rsqrt(jnp.mean(o * o, -1, keepdims=True) + NORM_EPS) * norm_g
    return on * (z * _sigmoid(z))


def _gdn_chunk_kernel(qn, kn, vv, gb, betab, z, norm_g, o_ref, s_out, s_ref, *, nb):
    c = pl.program_id(0)
    C = GDN_CHUNK
    W = GDN_H * C

    @pl.when(c == 0)
    def _():
        s_ref[...] = jnp.zeros_like(s_ref)

    ii = lax.broadcasted_iota(I32, (C, W), 0)
    jj = lax.broadcasted_iota(I32, (C, W), 1) % C
    lower = ii >= jj
    strict = ii > jj
    eye = jnp.where(ii == jj, 1.0, 0.0).astype(F32)
    blockmask = (lax.broadcasted_iota(I32, (W, W), 0) // C) == (lax.broadcasted_iota(I32, (W, W), 1) // C)
    zpad = jnp.zeros((GDN_D - C, GDN_D), F32)
    ng = norm_g[...]
    heads = [slice(h * GDN_D, (h + 1) * GDN_D) for h in range(GDN_H)]
    lanes = lambda parts: jnp.concatenate(parts, axis=1)

    blockmask = jnp.where(blockmask, 1.0, 0.0).astype(BF)

    def block_diag(m):
        return jnp.concatenate([m] * GDN_H, axis=0) * blockmask

    st = []
    for b, rows in [(b, slice(j * C, (j + 1) * C)) for b in range(nb) for j in range(GDN_CPS)]:
        q = [qn[b, rows, sl] for sl in heads]
        k = [kn[b, rows, sl] for sl in heads]
        beta = [betab[b, rows, sl] for sl in heads]
        gc = [_cumsum_rows(gb[b, rows, sl]) for sl in heads]
        kbeta = [k[h] * beta[h] for h in range(GDN_H)]
        gcol = lanes([g[:, 0:C] for g in gc])
        grow = lanes([jnp.concatenate([g, g], axis=0).T[0:C, 0:C] for g in gc])
        decay = jnp.where(lower, jnp.exp(jnp.where(lower, gcol - grow, 0.0)), 0.0)
        kk, qk = [], []
        for h in range(GDN_H):
            r = _dot_nt(jnp.concatenate([kbeta[h], q[h]], axis=0).astype(BF), k[h].astype(BF))
            kk.append(r[:C])
            qk.append(r[C:])
        a = jnp.where(strict, lanes(kk) * decay, 0.0)
        intra = lanes(qk) * decay
        rhs = jnp.concatenate(
            [lanes([vv[b, rows, heads[h]] * beta[h], kbeta[h] * jnp.exp(gc[h])]) for h in range(GDN_H)], axis=0)
        st.append(dict(b=b, rows=rows, q=q, k=k, gc=gc, intra=intra, rhs=rhs, nm=-a))

    for d in st:
        nh, nl = _split_bf(d["nm"])
        d["x"] = eye + d["nm"]
        d["p"] = _dot3(nh, nl, block_diag(nh), block_diag(nl))
    step = 4
    while step < C:
        for d in st:
            lh, ll = _split_bf(jnp.concatenate([d["x"], d["p"]], axis=0))
            r = _dot3(lh, ll, block_diag(lh[C:]), block_diag(ll[C:]))
            d["x"] = d["x"] + r[:C]
            d["p"] = r[C:]
        step *= 2
    for d in st:
        xh, xl = _split_bf(d["x"])
        p_hi, p_lo = _split_bf(d["p"])
        th, tl = _split_bf(d["x"] + _dot3(xh, xl, block_diag(p_hi), block_diag(p_lo)))
        rh, rl = _split_bf(d["rhs"])
        d["uw"] = _dot3(block_diag(th), block_diag(tl), rh, rl)

    for d in st:
        b, rows = d["b"], d["rows"]
        for h in range(GDN_H):
            u = d["uw"][h * C:(h + 1) * C, 0:GDN_D]
            w = d["uw"][h * C:(h + 1) * C, GDN_D:]
            gc = d["gc"][h]
            s = s_ref[b, h]
            r = _dot(jnp.concatenate([w, d["q"][h] * jnp.exp(gc)], axis=0).astype(BF), s.astype(BF))
            v_new = u - r[:C]
            o = r[C:] + _dot(d["intra"][:, h * C:(h + 1) * C].astype(BF), v_new.astype(BF))
            g_last = gc[C - 1:C, :]
            kd = d["k"][h] * jnp.exp(g_last - gc)
            kdt = jnp.concatenate([kd, zpad], axis=0).T.astype(BF)
            vnp = jnp.concatenate([v_new, zpad], axis=0).astype(BF)
            s_ref[b, h] = s * jnp.exp(g_last) + _dot(kdt, vnp)
            o_ref[b, rows, heads[h]] = _gated_rmsnorm(o, z[b, rows, heads[h]], ng).astype(BF)

    @pl.when(c == pl.num_programs(0) - 1)
    def _():
        s_out[...] = s_ref[...]


def _gdn_chunked(qn, kn, vv, gb, betab, h3, norm_g):
    b, t, _ = qn.shape
    C = GDN_CHUNK * GDN_CPS
    blk = pl.BlockSpec((b, C, GDN_W), lambda c: (0, c, 0))
    return pl.pallas_call(
        functools.partial(_gdn_chunk_kernel, nb=b),
        grid=(t // C,),
        in_specs=[blk, blk, blk, blk, blk,
                  pl.BlockSpec((b, C, GDN_W), lambda c: (0, c, C_GZ // GDN_W)),
                  pl.BlockSpec((1, GDN_D), lambda c: (0, 0))],
        out_specs=[pl.BlockSpec((b, C, GDN_W), lambda c: (0, c, 0)),
                   pl.BlockSpec((b, GDN_H, GDN_D, GDN_D), lambda c: (0, 0, 0, 0))],
        out_shape=[jax.ShapeDtypeStruct((b, t, GDN_W), BF),
                   jax.ShapeDtypeStruct((b, GDN_H, GDN_D, GDN_D), F32)],
        scratch_shapes=[pltpu.VMEM((b, GDN_H, GDN_D, GDN_D), F32)],
        compiler_params=_cp("arbitrary"),
        name="gdn_chunked",
    )(qn, kn, vv, gb, betab, h3, norm_g)


def _layer_norm(y, g, b):
    mu = jnp.mean(y, -1, keepdims=True)
    yc = y - mu
    var = jnp.mean(yc * yc, -1, keepdims=True)
    return yc * lax.rsqrt(var + LN_EPS) * g + b


def _out_proj_kernel(attn, gdn, scm, x, wo, g, b, o_ref):
    y = _dot(attn[...], wo[0:ATTN_W, :])
    y = y + _dot(gdn[...], wo[ATTN_W:ATTN_W + GDN_W, :])
    y = y + _dot(scm[...], wo[ATTN_W + GDN_W:, :])
    o_ref[...] = _layer_norm(DN_ALPHA * x[...] + y, g[...], b[...])


def _out_proj(attn, gdn, scm, x, wo, layer, g, b, tm):
    m = x.shape[0]
    row = lambda w: pl.BlockSpec((tm, w), lambda i: (i, 0))
    full = lambda a: pl.BlockSpec((None,) + a.shape[1:], lambda i: (layer, 0, 0))
    return pl.pallas_call(
        _out_proj_kernel,
        grid=(m // tm,),
        in_specs=[row(ATTN_W), row(GDN_W), row(SC_W), row(D_MODEL), full(wo), full(g), full(b)],
        out_specs=row(D_MODEL),
        out_shape=jax.ShapeDtypeStruct((m, D_MODEL), F32),
        compiler_params=_cp("parallel"),
        name="out_proj_ln",
    )(attn, gdn, scm, x, wo, g, b)


def _ffn_kernel(h, wg, wu, wd, g, b, o_ref, hb, acc):
    f = pl.program_id(1)

    @pl.when(f == 0)
    def _():
        hb[...] = h[...].astype(BF)
        acc[...] = jnp.zeros_like(acc)

    x = hb[...]
    gate = _dot(x, wg[...])
    hid = gate * _sigmoid(gate) * _dot(x, wu[...])
    acc[...] += _dot(hid.astype(BF), wd[...])

    @pl.when(f == pl.num_programs(1) - 1)
    def _():
        o_ref[...] = _layer_norm(DN_ALPHA * h[...] + acc[...], g[...], b[...])


def _ffn(h, wg, wu, wd, layer, g, b, tm, tf):
    m = h.shape[0]
    full = lambda a: pl.BlockSpec((None,) + a.shape[1:], lambda i, f: (layer, 0, 0))
    return pl.pallas_call(
        _ffn_kernel,
        grid=(m // tm, FFN_DIM // tf),
        in_specs=[pl.BlockSpec((tm, D_MODEL), lambda i, f: (i, 0)),
                  pl.BlockSpec((None, D_MODEL, tf), lambda i, f: (layer, 0, f)),
                  pl.BlockSpec((None, D_MODEL, tf), lambda i, f: (layer, 0, f)),
                  pl.BlockSpec((None, tf, D_MODEL), lambda i, f: (layer, f, 0)),
                  full(g), full(b)],
        out_specs=pl.BlockSpec((tm, D_MODEL), lambda i, f: (i, 0)),
        out_shape=jax.ShapeDtypeStruct((m, D_MODEL), F32),
        scratch_shapes=[pltpu.VMEM((tm, D_MODEL), BF), pltpu.VMEM((tm, D_MODEL), F32)],
        compiler_params=_cp("parallel", "arbitrary"),
        name="ffn_ln",
    )(h, wg, wu, wd, g, b)


def _head_rows(v):
    return jnp.repeat(v.astype(F32), GDN_D)[None, :]


def _prompt_layer(xp, tabs, lw, stacks, tm_in=1024, tn_in=768, tr=256, tm_out=512, tm_ffn=512, tf=512):
    b, t, _ = xp.shape
    x2 = xp.reshape(b * t, D_MODEL)
    layer = lw["layer"]
    h = _matmul(x2, lw["w_in"], layer, min(tm_in, b * t), tn_in)
    h3 = h.reshape(b, t, IN_PACKED)
    k_st, v_st, ik_st, kb, vt, qt, iqt, ikb, iwt = _attn_prep(h3, tabs, tr, layer, *stacks)
    attn = _dsa_prompt(qt, iqt, iwt, kb, vt, ikb, tr)
    qn, kn, vv, gb, betab, scm, utail = _mix_prep(h3, lw["gdn_conv_w"], lw["sc_conv_w"],
                                                 lw["alog"], lw["dtb"], tr)
    gdn, s_new = _gdn_chunked(qn, kn, vv, gb, betab, h3, lw["norm_g"])
    h1 = _out_proj(attn.reshape(b * t, ATTN_W), gdn.reshape(b * t, GDN_W), scm.reshape(b * t, SC_W),
                   x2, lw["w_out"], layer, lw["ln1_g"], lw["ln1_b"], min(tm_out, b * t))
    y = _ffn(h1, lw["w_gate"], lw["w_up"], lw["w_down"], layer, lw["ln2_g"], lw["ln2_b"],
             min(tm_ffn, b * t), tf)
    outs = dict(
        stacks=(k_st, v_st, ik_st),
        s=s_new,
        gconv=h3[:, t - (GDN_CONV - 1):, C_GQKV:C_GQKV + GDN_CONV_DIM],
        sconv=utail[:, 8 - (SC_CONV - 1):, :],
        attn=attn, gdn=gdn, scm=scm,
    )
    return y.reshape(b, t, D_MODEL), outs


def _sample_prep_kernel(h, c128, s128, c64, s64, gst, sst, s_in, gw, sw, alog, dtb, norm_g,
                        q_o, k_o, iq_o, ik_o, iw_o, gst_o, sst_o, s_o, gdn_o, scm_o, o_buf, *, nb):
    cos = c128[...]
    sin = s128[...]
    for hh in range(N_Q):
        sl = slice(hh * HEAD_DIM, (hh + 1) * HEAD_DIM)
        q_o[:, sl] = _rope128(h[:, C_AQ + hh * HEAD_DIM:C_AQ + (hh + 1) * HEAD_DIM], cos, sin) * (HEAD_DIM ** -0.5)
    for hh in range(N_KV):
        sl = slice(hh * HEAD_DIM, (hh + 1) * HEAD_DIM)
        k_o[:, sl] = _rope128(h[:, C_AK + hh * HEAD_DIM:C_AK + (hh + 1) * HEAD_DIM], cos, sin)
    cos = c64[...]
    sin = s64[...]
    for j in range(IDX_H * IDX_D // 128):
        sl = slice(j * 128, (j + 1) * 128)
        iq_o[:, sl] = _rope64x2(h[:, C_IQ + j * 128:C_IQ + (j + 1) * 128], cos, sin) * (IDX_D ** -0.5)
    small = h[:, C_SM:C_SM + 128]
    ik_o[...] = _rope64x2(small, cos, sin)[:, SM_IK:SM_IK + IDX_D]
    iw_o[...] = small[:, SM_IW:SM_IW + IDX_H] * (IDX_H ** -0.5)

    gq = h[:, C_GQKV:C_GQKV + GDN_CONV_DIM]
    y = gq * gw[GDN_CONV - 1:GDN_CONV, :]
    for j in range(GDN_CONV - 1):
        y = y + gst[j] * gw[j:j + 1, :]
    for j in range(GDN_CONV - 2):
        gst_o[j] = gst[j + 1]
    gst_o[GDN_CONV - 2] = gq
    qs, ks, v, gs, bs = _gdn_activations(y, small, alog[...], dtb[...])
    for b in range(nb):
        for hh in range(GDN_H):
            sl = slice(hh * GDN_D, (hh + 1) * GDN_D)
            kc = jnp.broadcast_to(ks[hh][b:b + 1, :], (GDN_D, GDN_D)).T
            qc = jnp.broadcast_to(qs[hh][b:b + 1, :], (GDN_D, GDN_D)).T
            s = s_in[b, hh] * jnp.exp(gs[hh][b:b + 1, :])
            ks_row = jnp.sum(kc * s, axis=0, keepdims=True)
            delta = (v[b:b + 1, sl] - ks_row) * bs[hh][b:b + 1, :]
            s = s + kc * delta
            s_o[b, hh] = s
            o_buf[b:b + 1, sl] = jnp.sum(qc * s, axis=0, keepdims=True)
    ng = norm_g[...]
    for hh in range(GDN_H):
        sl = slice(hh * GDN_D, (hh + 1) * GDN_D)
        z = h[:, C_GZ + hh * GDN_D:C_GZ + (hh + 1) * GDN_D]
        gdn_o[:, sl] = _gated_rmsnorm(o_buf[:, sl], z, ng).astype(BF)

    u = h[:, C_SC:C_SC + SC_W] * h[:, C_SX:C_SX + SC_W]
    y = u * sw[SC_CONV - 1:SC_CONV, :]
    for j in range(SC_CONV - 1):
        y = y + sst[j] * sw[j:j + 1, :]
    for j in range(SC_CONV - 2):
        sst_o[j] = sst[j + 1]
    sst_o[SC_CONV - 2] = u
    scm_o[...] = (h[:, C_SB:C_SB + SC_W] * y).astype(BF)


def _sample_prep(h, tabs, gst, sst, s_in, gw, sw, alog, dtb, norm_g):
    nb = h.shape[0]
    sds = jax.ShapeDtypeStruct
    return pl.pallas_call(
        functools.partial(_sample_prep_kernel, nb=nb),
        out_shape=[sds((nb, ATTN_W), F32), sds((nb, KV_W), F32), sds((nb, IDX_H * IDX_D), F32),
                   sds((nb, IDX_D), F32), sds((nb, IDX_H), F32),
                   sds(gst.shape, F32), sds(sst.shape, F32), sds(s_in.shape, F32),
                   sds((nb, GDN_W), BF), sds((nb, SC_W), BF)],
        scratch_shapes=[pltpu.VMEM((nb, GDN_W), F32)],
        compiler_params=pltpu.CompilerParams(vmem_limit_bytes=VMEM_LIMIT),
        name="sample_prep",
    )(h, *tabs, gst, sst, s_in, gw, sw, alog, dtb, norm_g)


def _sample_index_kernel(pt, *refs, pps, topk):
    pages = refs[:pps]
    iq, iw, ikn, expand, bias_o, seln_o, keys = refs[pps:]
    s = pl.program_id(1)
    iqb = iq[...]
    w = jnp.broadcast_to(iw[...], (IDX_H, PAGE))
    for j in range(pps):
        sc = jnp.maximum(_dot(iqb, pages[j][...].astype(BF)), 0.0)
        keys[pl.ds(s * pps + j, 1), :] = _sortable(jnp.sum(sc * w, axis=0, keepdims=True))

    @pl.when(s == pl.num_programs(1) - 1)
    def _():
        prod = iqb.astype(F32) * ikn[...].astype(BF).astype(F32)
        sn = jnp.sum(prod, axis=1, keepdims=True)
        sn = jnp.sum(jnp.maximum(sn, 0.0) * iw[...], axis=0, keepdims=True)
        key_new = jnp.broadcast_to(_sortable(sn), (1, PAGE))
        kall = keys[...]

        def count_ge(cand):
            c = jnp.where(kall >= cand, 1.0, 0.0).sum(axis=0, keepdims=True).sum(axis=1, keepdims=True)
            return jnp.broadcast_to(c, (1, PAGE)) + jnp.where(key_new >= cand, 1.0, 0.0)

        theta = _kth_largest(count_ge, (1, PAGE), float(topk))
        gt = jnp.where(kall > theta, 1.0, 0.0)
        tie = jnp.where(kall == theta, 1.0, 0.0)
        new_gt = jnp.where(key_new > theta, 1.0, 0.0)
        n_gt = jnp.broadcast_to(gt.sum(axis=0, keepdims=True).sum(axis=1, keepdims=True), (1, PAGE)) + new_gt
        quota = float(topk) - n_gt
        ii = lax.broadcasted_iota(I32, (PAGE, PAGE), 0)
        jj = lax.broadcasted_iota(I32, (PAGE, PAGE), 1)
        in_page = _dot(tie.astype(BF), jnp.where(ii < jj, 1.0, 0.0).astype(BF))
        per_page = jnp.broadcast_to(jnp.sum(tie, axis=1, keepdims=True), (kall.shape[0], PAGE))
        npg = kall.shape[0]
        pi = lax.broadcasted_iota(I32, (npg, npg), 0)
        pj = lax.broadcasted_iota(I32, (npg, npg), 1)
        earlier = _dot(jnp.where(pj < pi, 1.0, 0.0).astype(BF), per_page.astype(BF))
        taken = jnp.where(in_page + earlier < quota, tie, 0.0)
        n_tie = jnp.broadcast_to(tie.sum(axis=0, keepdims=True).sum(axis=1, keepdims=True), (1, PAGE))
        new_sel = jnp.where(key_new > theta, 1.0, jnp.where((key_new == theta) & (n_tie < quota), 1.0, 0.0))
        sel = _dot((gt + taken).astype(BF), expand[...])
        bias_o[...] = jnp.where(sel > 0.5, 0.0, MASKED)
        seln_o[...] = jnp.where(new_sel > 0.5, 0.0, MASKED)


def _expand_matrix():
    return jnp.asarray(np.arange(PAGE)[:, None] == (np.arange(PAGE * N_KV)[None, :] // N_KV), BF)


def _sample_index(page_table, kidx_t, layer, iq3, iw3, ikn3, pps=16):
    nb, npages = page_table.shape
    topk = min(TOPK_MAX, (npages * PAGE + 1) // 4)
    page = lambda j: pl.BlockSpec((None, None, IDX_D, PAGE),
                                  lambda b, s, pt, j=j: (layer, pt[b, s * pps + j], 0, 0))
    per_b = lambda shp: pl.BlockSpec((None,) + shp, lambda b, s, pt: (b, 0, 0))
    grid_spec = pltpu.PrefetchScalarGridSpec(
        num_scalar_prefetch=1,
        grid=(nb, npages // pps),
        in_specs=[page(j) for j in range(pps)]
                 + [per_b((IDX_H, IDX_D)), per_b((IDX_H, 1)), per_b((1, IDX_D)),
                    pl.BlockSpec((PAGE, PAGE * N_KV), lambda b, s, pt: (0, 0))],
        out_specs=[per_b((npages, PAGE * N_KV)), per_b((1, PAGE))],
        scratch_shapes=[pltpu.VMEM((npages, PAGE), I32)],
    )
    return pl.pallas_call(
        functools.partial(_sample_index_kernel, pps=pps, topk=topk),
        grid_spec=grid_spec,
        out_shape=[jax.ShapeDtypeStruct((nb, npages, PAGE * N_KV), F32), jax.ShapeDtypeStruct((nb, 1, PAGE), F32)],
        compiler_params=_cp("parallel", "arbitrary"),
        name="sample_index",
    )(page_table, *([kidx_t] * pps), iq3, iw3, ikn3, _expand_matrix())


def _sample_attn_kernel(pt, *refs, pps):
    kp = refs[:pps]
    vp = refs[pps:2 * pps]
    q8, bias, seln, kn8, vn8, o_ref, m_ref, l_ref, acc_ref = refs[2 * pps:]
    s = pl.program_id(1)
    rows = PAGE * N_KV

    @pl.when(s == 0)
    def _():
        m_ref[...] = jnp.full_like(m_ref, NEG_INF)
        l_ref[...] = jnp.zeros_like(l_ref)
        acc_ref[...] = jnp.zeros_like(acc_ref)

    q = q8[...]
    lane_head = lax.broadcasted_iota(I32, (N_Q, rows), 1) % N_KV
    own = lane_head == lax.broadcasted_iota(I32, (N_Q, rows), 0) // (N_Q // N_KV)
    head_mask = jnp.where(own, 0.0, MASKED)
    lg = [_dot_nt(q, kp[j][...].astype(BF)) + (bias[pl.ds(s * pps + j, 1), :] + head_mask) for j in range(pps)]
    m_old = m_ref[...]
    m_new = m_old
    for x in lg:
        m_new = jnp.maximum(m_new, jnp.max(x, axis=1, keepdims=True))
    alpha = jnp.exp(m_old - m_new)
    l_new = l_ref[...] * alpha
    acc = acc_ref[...] * alpha
    for j in range(pps):
        p = jnp.exp(lg[j] - m_new)
        l_new = l_new + jnp.sum(p, axis=1, keepdims=True)
        acc = acc + _dot(p.astype(BF), vp[j][...].astype(BF))
    m_ref[...] = m_new
    l_ref[...] = l_new
    acc_ref[...] = acc

    @pl.when(s == pl.num_programs(1) - 1)
    def _():
        knb = kn8[...].astype(BF).astype(F32)
        vnb = vn8[...].astype(BF).astype(F32)
        x = jnp.sum(q.astype(F32) * knb, axis=1, keepdims=True) + seln[:, 0:1]
        m_fin = jnp.maximum(m_new, x)
        a = jnp.exp(m_new - m_fin)
        p = jnp.exp(x - m_fin)
        l_fin = l_new * a + p
        o_ref[...] = ((acc * a + p.astype(BF).astype(F32) * vnb) / l_fin).astype(BF)


def _sample_attn(page_table, ck2, cv2, layer, q8, bias, seln, kn8, vn8, pps=8):
    nb, npages = page_table.shape
    rows = PAGE * N_KV
    page = lambda j: pl.BlockSpec((None, None, rows, HEAD_DIM),
                                  lambda b, s, pt, j=j: (layer, pt[b, s * pps + j], 0, 0))
    per_b = lambda shp: pl.BlockSpec((None,) + shp, lambda b, s, pt: (b, 0, 0))
    grid_spec = pltpu.PrefetchScalarGridSpec(
        num_scalar_prefetch=1,
        grid=(nb, npages // pps),
        in_specs=[page(j) for j in range(pps)] * 2
                 + [per_b((N_Q, HEAD_DIM)), per_b((npages, rows)), per_b((1, PAGE)),
                    per_b((N_Q, HEAD_DIM)), per_b((N_Q, HEAD_DIM))],
        out_specs=per_b((N_Q, HEAD_DIM)),
        scratch_shapes=[pltpu.VMEM((N_Q, 1), F32), pltpu.VMEM((N_Q, 1), F32), pltpu.VMEM((N_Q, HEAD_DIM), F32)],
    )
    return pl.pallas_call(
        functools.partial(_sample_attn_kernel, pps=pps),
        grid_spec=grid_spec,
        out_shape=jax.ShapeDtypeStruct((nb, N_Q, HEAD_DIM), BF),
        compiler_params=_cp("parallel", "arbitrary"),
        name="sample_attn",
    )(page_table, *([ck2] * pps), *([cv2] * pps), q8, bias, seln, kn8, vn8)


SAMPLE_ROWS = 16


def _sample_layer(xs, tabs, lw, layer, page_table, cache_k, cache_v, cache_kidx, gst, sst, s_in):
    nb = page_table.shape[0]
    h = _matmul(xs, lw["w_in"], layer, SAMPLE_ROWS, 768)[:nb]
    q, k, iq, ik, iw, gst_n, sst_n, s_n, gdn, scm = _sample_prep(
        h, tabs, gst, sst, s_in, lw["gdn_conv_w"], lw["sc_conv_w"], lw["alog"], lw["dtb"], lw["norm_g"])
    v = h[:, C_AV:C_AV + KV_W]
    bias, seln = _sample_index(page_table, cache_kidx, layer, iq.reshape(nb, IDX_H, IDX_D).astype(BF),
                               iw.reshape(nb, IDX_H, 1), ik.reshape(nb, 1, IDX_D))
    own_kv = lambda a: jnp.repeat(a.reshape(nb, N_KV, HEAD_DIM), N_Q // N_KV, axis=1)
    attn = _sample_attn(page_table, cache_k, cache_v, layer, q.reshape(nb, N_Q, HEAD_DIM).astype(BF),
                        bias, seln, own_kv(k), own_kv(v)).reshape(nb, ATTN_W)
    padr = lambda a: jnp.pad(a, ((0, SAMPLE_ROWS - nb), (0, 0)))
    h1 = _out_proj(padr(attn), padr(gdn), padr(scm), xs, lw["w_out"], layer, lw["ln1_g"], lw["ln1_b"],
                   SAMPLE_ROWS)
    y = _ffn(h1, lw["w_gate"], lw["w_up"], lw["w_down"], layer, lw["ln2_g"], lw["ln2_b"], SAMPLE_ROWS, 512)
    outs = dict(k=k.reshape(nb, 1, N_KV, HEAD_DIM), v=v.reshape(nb, 1, N_KV, HEAD_DIM), kidx=ik.reshape(nb, 1, IDX_D),
                s=s_n, gconv=jnp.swapaxes(gst_n, 0, 1), sconv=jnp.swapaxes(sst_n, 0, 1),
                attn=attn, gdn=gdn, scm=scm)
    return y, outs


def kernel(x_prompt, x_sample, cache_k, cache_v, cache_kidx, page_table, state_gdn, state_gdn_conv,
           state_sc_conv, w_in, gdn_conv_w, gdn_a_log, gdn_dt_bias, gdn_norm_g, sc_conv_w, w_out,
           ln1_g, ln1_b, w_gate, w_up, w_down, ln2_g, ln2_b):
    bp, tp, _ = x_prompt.shape
    nb, ts, _ = x_sample.shape
    assert ts == 1
    past = page_table.shape[1] * PAGE
    tabs_p = _rope_tables(jnp.arange(tp))
    tabs_s = _rope_tables(past + jnp.arange(ts))
    w_in_p = _pack_w_in(w_in)
    w_out_b, w_gate_b, w_up_b, w_down_b = (w.astype(BF) for w in (w_out, w_gate, w_up, w_down))
    depth, pool = cache_k.shape[:2]
    cache_k = cache_k.reshape(depth, pool, PAGE * N_KV, HEAD_DIM)
    cache_v = cache_v.reshape(depth, pool, PAGE * N_KV, HEAD_DIM)
    cache_kidx = jnp.swapaxes(cache_kidx, 2, 3)

    xp = x_prompt
    xs = jnp.pad(x_sample.reshape(nb, D_MODEL), ((0, SAMPLE_ROWS - nb), (0, 0)))
    stacks = (jnp.zeros((DEPTH, bp, tp * N_KV, HEAD_DIM), F32), jnp.zeros((DEPTH, bp, tp * N_KV, HEAD_DIM), F32),
              jnp.zeros((DEPTH, bp, IDX_D, tp), F32))
    po, so = [], []
    for l in range(DEPTH):
        lw = dict(layer=l, w_in=w_in_p, gdn_conv_w=gdn_conv_w[l], sc_conv_w=sc_conv_w[l],
                  alog=_head_rows(gdn_a_log[l]), dtb=_head_rows(gdn_dt_bias[l]), norm_g=gdn_norm_g[l][None, :],
                  w_out=w_out_b, ln1_g=ln1_g[:, None, :], ln1_b=ln1_b[:, None, :],
                  w_gate=w_gate_b, w_up=w_up_b, w_down=w_down_b,
                  ln2_g=ln2_g[:, None, :], ln2_b=ln2_b[:, None, :])
        xp, o = _prompt_layer(xp, tabs_p, lw, stacks)
        stacks = o["stacks"]
        po.append(o)
        xs, o = _sample_layer(xs, tabs_s, lw, l, page_table, cache_k, cache_v, cache_kidx,
                              jnp.swapaxes(state_gdn_conv[l], 0, 1), jnp.swapaxes(state_sc_conv[l], 0, 1),
                              state_gdn[l])
        so.append(o)

    st = lambda outs, n: jnp.stack([o[n] for o in outs])
    k_st, v_st, ik_st = stacks
    return (xp, xs[:nb].reshape(nb, ts, D_MODEL),
            k_st.reshape(DEPTH, bp, tp, N_KV, HEAD_DIM), v_st.reshape(DEPTH, bp, tp, N_KV, HEAD_DIM),
            jnp.swapaxes(ik_st, 2, 3), st(so, "k"), st(so, "v"), st(so, "kidx"),
            st(po, "s"), st(so, "s"), st(po, "gconv"), st(so, "gconv"), st(po, "sconv"), st(so, "sconv"))
```

```python
import functools
import math

import numpy as np
import jax
import jax.numpy as jnp
from jax import lax
from jax.experimental import pallas as pl
from jax.experimental.pallas import tpu as pltpu

F32 = jnp.float32
BF = jnp.bfloat16
I32 = jnp.int32

D_MODEL = 2048
DEPTH = 4
PAGE = 128
HEAD_DIM = 128
N_Q = 8
N_KV = 4
ATTN_W = N_Q * HEAD_DIM
KV_W = N_KV * HEAD_DIM
IDX_H = 16
IDX_D = 64
TOPK_MAX = 256
ROPE_THETA = 10000.0
GDN_H = 4
GDN_D = 128
GDN_W = GDN_H * GDN_D
GDN_CONV = 4
GDN_CONV_DIM = 3 * GDN_W
GDN_CHUNK = 64
GDN_CPS = 4
SC_W = 512
SC_CONV = 3
FFN_DIM = 5632
DN_ALPHA = (2 * DEPTH) ** 0.25
LN_EPS = 1e-5
NORM_EPS = 1e-6
IN_SIZES = (ATTN_W, KV_W, KV_W, IDX_H * IDX_D, IDX_D, IDX_H, GDN_CONV_DIM, GDN_W, GDN_H, GDN_H,
            SC_W, SC_W, SC_W)

C_AQ, C_AK, C_AV, C_IQ, C_GQKV, C_GZ, C_SB, C_SC, C_SX, C_SM = (
    0, 1024, 1536, 2048, 3072, 4608, 5120, 5632, 6144, 6656)
SM_IK, SM_IW, SM_GB, SM_GA = 0, 64, 80, 84
IN_PACKED = 6912

NEG_INF = float("-inf")
MASKED = -1e30
Q_SCALE_LOG2 = HEAD_DIM ** -0.5 * math.log2(math.e)
VT_ROWS = HEAD_DIM + 16
SUB = 128
INT_MIN = -2 ** 31
INT_MAX = 2 ** 31 - 1

VMEM_LIMIT = 56 * 1024 * 1024


def _cp(*sem):
    return pltpu.CompilerParams(dimension_semantics=tuple(sem), vmem_limit_bytes=VMEM_LIMIT)


def _sigmoid(x):
    return 1.0 / (1.0 + jnp.exp(-x))


def _softplus(x):
    return jnp.maximum(x, 0.0) + jnp.log1p(jnp.exp(-jnp.abs(x)))


def _dot(a, b):
    return jnp.dot(a, b, preferred_element_type=F32)


def _dot_nt(a, b):
    return lax.dot_general(a, b, (((1,), (1,)), ((), ())), preferred_element_type=F32)


_IN_OFF = np.concatenate([[0], np.cumsum(IN_SIZES)]).tolist()
_O_IK, _O_GQKV, _O_GB, _O_SB, _O_END = _IN_OFF[4], _IN_OFF[6], _IN_OFF[8], _IN_OFF[10], _IN_OFF[13]


PACK_COLS = 256
_PB_G, _PB_S, _PB_SM = C_GQKV // PACK_COLS, C_SB // PACK_COLS, C_SM // PACK_COLS
_SH_G, _SH_S = _O_GQKV - C_GQKV, _O_SB - C_SB


def _pack_kernel(a, b, o):
    j = pl.program_id(1)

    def emit(src):
        o[...] = src.T.astype(BF)

    @pl.when(j < _PB_G)
    def _():
        emit(a[...])

    @pl.when((j >= _PB_G) & (j < _PB_S))
    def _():
        emit(jnp.concatenate([a[_SH_G:, :], b[:_SH_G, :]], axis=0))

    @pl.when((j >= _PB_S) & (j < _PB_SM))
    def _():
        emit(jnp.concatenate([a[_SH_S:, :], b[:_SH_S, :]], axis=0))

    @pl.when(j == _PB_SM)
    def _():
        n_idx = _O_GQKV - _O_IK
        n_gdn = _O_SB - _O_GB
        lo = _O_GB - (_O_GB // 128) * 128
        pad = jnp.zeros((PACK_COLS - n_idx - n_gdn, a.shape[1]), F32)
        emit(jnp.concatenate([a[:n_idx, :], b[lo:lo + n_gdn, :], pad], axis=0))


def _pack_w_in(w_in):
    wt = jnp.swapaxes(w_in, 1, 2)
    depth, n, d = wt.shape
    nblk = IN_PACKED // PACK_COLS
    last_b = (n - 1) // 128

    def a_map(l, j):
        return l, jnp.where(j == _PB_SM, _O_IK // PACK_COLS, j), 0

    def b_map(l, j):
        return l, jnp.where(j == _PB_SM, _O_GB // 128, jnp.minimum(2 * j + 2, last_b)), 0

    return pl.pallas_call(
        _pack_kernel,
        grid=(depth, nblk),
        in_specs=[pl.BlockSpec((None, PACK_COLS, d), a_map), pl.BlockSpec((None, 128, d), b_map)],
        out_specs=pl.BlockSpec((None, d, PACK_COLS), lambda l, j: (l, 0, j)),
        out_shape=jax.ShapeDtypeStruct((depth, d, IN_PACKED), BF),
        compiler_params=_cp("parallel", "parallel"),
        name="pack_w_in",
    )(wt, wt)


def _mm_kernel(x_ref, w_ref, o_ref):
    o_ref[...] = _dot(x_ref[...].astype(BF), w_ref[...])


def _matmul(x, w, layer, tm, tn):
    m, k = x.shape
    n = w.shape[2]
    return pl.pallas_call(
        _mm_kernel,
        grid=(m // tm, n // tn),
        in_specs=[pl.BlockSpec((tm, k), lambda i, j: (i, 0)),
                  pl.BlockSpec((None, k, tn), lambda i, j: (layer, 0, j))],
        out_specs=pl.BlockSpec((tm, tn), lambda i, j: (i, j)),
        out_shape=jax.ShapeDtypeStruct((m, n), F32),
        compiler_params=_cp("parallel", "arbitrary"),
        name="proj_in",
    )(x, w)


def _rope_tables(pos):
    def tab(half):
        inv = ROPE_THETA ** (-jnp.arange(half, dtype=F32) / half)
        ang = pos.astype(F32)[:, None] * inv
        return jnp.cos(ang), jnp.sin(ang)
    c, s = tab(HEAD_DIM // 2)
    c128 = jnp.concatenate([c, c], -1)
    s128 = jnp.concatenate([-s, s], -1)
    c, s = tab(IDX_D // 2)
    c64 = jnp.concatenate([c, c, c, c], -1)
    s64 = jnp.concatenate([-s, s, -s, s], -1)
    return c128, s128, c64, s64


def _rope128(x, cos, sin):
    return x * cos + pltpu.roll(x, HEAD_DIM // 2, 1) * sin


def _rope64x2(x, cos, sin):
    lane = lax.broadcasted_iota(I32, x.shape, 1)
    first = (lane % IDX_D) < (IDX_D // 2)
    partner = jnp.where(first, pltpu.roll(x, 128 - IDX_D // 2, 1), pltpu.roll(x, IDX_D // 2, 1))
    return x * cos + partner * sin


def _attn_prep_kernel(aq, ak, av, iq, sm, c128, s128, c64, s64, k_st, v_st, ik_st,
                      ks_o, vs_o, iks_o, kb_o, vt_o, qt_o, iqt_o, ikb_o, iwt_o, *, tr):
    del k_st, v_st, ik_st
    cos = c128[...]
    sin = s128[...]
    for h in range(N_Q):
        sl = slice(h * HEAD_DIM, (h + 1) * HEAD_DIM)
        y = _rope128(aq[:, sl], cos, sin) * Q_SCALE_LOG2
        qt_o[sl, :] = y.T.astype(BF)
    for h in range(N_KV):
        sl = slice(h * HEAD_DIM, (h + 1) * HEAD_DIM)
        y = _rope128(ak[:, sl], cos, sin)
        v = av[:, sl]
        ks_o[pl.ds(h, tr, stride=N_KV), :] = y
        vs_o[pl.ds(h, tr, stride=N_KV), :] = v
        kb_o[:, sl] = y.astype(BF)
        vt_o[h * VT_ROWS:h * VT_ROWS + HEAD_DIM, :] = v.T.astype(BF)
        vt_o[h * VT_ROWS + HEAD_DIM:(h + 1) * VT_ROWS, :] = jnp.ones((VT_ROWS - HEAD_DIM, tr), BF)
    cos = c64[...]
    sin = s64[...]
    for j in range(IDX_H * IDX_D // 128):
        sl = slice(j * 128, (j + 1) * 128)
        y = _rope64x2(iq[:, sl], cos, sin) * (IDX_D ** -0.5)
        iqt_o[sl, :] = y.T.astype(BF)
    x = sm[...]
    y = _rope64x2(x, cos, sin)
    iks_o[...] = y.T[SM_IK:SM_IK + IDX_D, :]
    ikb_o[...] = y[:, SM_IK:SM_IK + IDX_D].astype(BF)
    iwt_o[...] = x.T[SM_IW:SM_IW + IDX_H, :] * (IDX_H ** -0.5)


def _attn_prep(h3, tabs, tr, layer, k_st, v_st, ik_st):
    b, t, _ = h3.shape
    nr = t // tr
    col = lambda w, off: pl.BlockSpec((None, tr, w), lambda bi, r, o=off // w: (bi, r, o))
    tab = pl.BlockSpec((tr, 128), lambda bi, r: (r, 0))
    anyspec = pl.BlockSpec(memory_space=pl.ANY)
    sds = jax.ShapeDtypeStruct
    return pl.pallas_call(
        functools.partial(_attn_prep_kernel, tr=tr),
        grid=(b, nr),
        in_specs=[col(ATTN_W, C_AQ), col(KV_W, C_AK), col(KV_W, C_AV), col(ATTN_W, C_IQ), col(128, C_SM),
                  tab, tab, tab, tab, anyspec, anyspec, anyspec],
        out_specs=[
            pl.BlockSpec((None, None, tr * N_KV, HEAD_DIM), lambda bi, r: (layer, bi, r, 0)),
            pl.BlockSpec((None, None, tr * N_KV, HEAD_DIM), lambda bi, r: (layer, bi, r, 0)),
            pl.BlockSpec((None, None, IDX_D, tr), lambda bi, r: (layer, bi, 0, r)),
            pl.BlockSpec((None, tr, KV_W), lambda bi, r: (bi, r, 0)),
            pl.BlockSpec((None, None, N_KV * VT_ROWS, tr), lambda bi, r: (bi, r, 0, 0)),
            pl.BlockSpec((None, ATTN_W, tr), lambda bi, r: (bi, 0, r)),
            pl.BlockSpec((None, ATTN_W, tr), lambda bi, r: (bi, 0, r)),
            pl.BlockSpec((None, tr, IDX_D), lambda bi, r: (bi, r, 0)),
            pl.BlockSpec((None, IDX_H, tr), lambda bi, r: (bi, 0, r)),
        ],
        out_shape=[
            sds(k_st.shape, F32), sds(v_st.shape, F32), sds(ik_st.shape, F32),
            sds((b, t, KV_W), BF),
            sds((b, nr, N_KV * VT_ROWS, tr), BF),
            sds((b, ATTN_W, t), BF),
            sds((b, ATTN_W, t), BF),
            sds((b, t, IDX_D), BF),
            sds((b, IDX_H, t), F32),
        ],
        input_output_aliases={9: 0, 10: 1, 11: 2},
        compiler_params=_cp("parallel", "parallel"),
        name="attn_prep",
    )(h3, h3, h3, h3, h3, *tabs, k_st, v_st, ik_st)


def _sortable(x):
    bits = pltpu.bitcast(x, I32)
    return jnp.where(bits >= 0, bits, bits ^ jnp.int32(0x7FFFFFFF))


def _kth_largest(count_ge, shape, k):
    lo = jnp.where(count_ge(jnp.zeros(shape, I32)) >= k, 0, INT_MIN).astype(I32)

    def body(it, lo):
        cand = lo + (jnp.int32(1) << (30 - it))
        return jnp.where(count_ge(cand) >= k, cand, lo)

    return lax.fori_loop(0, 31, body, lo)


def _dsa_prompt_kernel(qt, iqt, iwt, kb, vt, ikb, o_ref, keys, bias, m_ref, l_ref, acc_ref, s_ref, *, tq, topk):
    i = pl.program_id(1)
    nch = i + 1
    row = lax.broadcasted_iota(I32, (tq, tq), 0)
    lane = lax.broadcasted_iota(I32, (tq, tq), 1)

    def off_of(c):
        return pl.multiple_of(c * tq, tq)

    def causal_of(c):
        return (c * tq + row) <= (i * tq + lane)

    def index_rows(c, n):
        off = off_of(c)
        ikc = ikb[pl.ds(off, n * tq), :]
        acc = jnp.zeros((n * tq, tq), F32)
        for h in range(IDX_H):
            s = _dot(ikc, iqt[h * IDX_D:(h + 1) * IDX_D, :])
            acc = acc + jnp.maximum(s, 0.0) * iwt[h:h + 1, :]
        for j in range(n):
            part = jnp.where(causal_of(c + j), acc[j * tq:(j + 1) * tq], NEG_INF)
            keys[pl.ds(pl.multiple_of(off + j * tq, tq), tq), :] = _sortable(part)

    def indexer2(j, carry):
        index_rows(2 * j, 2)
        return carry

    lax.fori_loop(0, nch // 2, indexer2, 0)

    @pl.when(nch % 2 == 1)
    def _():
        index_rows(nch - 1, 1)

    def count_ge(cand):
        def hits(c):
            hit = jnp.where(keys[pl.ds(off_of(c), tq), :] >= cand, 1, 0)
            parts = [hit[r * 8:(r + 1) * 8] for r in range(tq // 8)]
            while len(parts) > 1:
                parts = [parts[j] + parts[j + 1] for j in range(0, len(parts), 2)]
            return parts[0]

        cnt = lax.fori_loop(0, nch // 2, lambda j, cnt: cnt + (hits(2 * j) + hits(2 * j + 1)),
                            jnp.zeros((8, tq), I32))
        cnt = cnt + lax.cond(nch % 2 == 1, lambda: hits(nch - 1), lambda: jnp.zeros((8, tq), I32))
        return cnt.sum(axis=0, keepdims=True)

    theta = _kth_largest(count_ge, (1, tq), topk)

    def make_bias(c, cnt):
        off = off_of(c)
        sel = keys[pl.ds(off, tq), :] >= theta
        bias[pl.ds(off, tq), :] = jnp.where(causal_of(c), jnp.where(sel, 0.0, MASKED), MASKED)
        parts = [jnp.where(sel[r * 8:(r + 1) * 8], 1, 0) for r in range(tq // 8)]
        while len(parts) > 1:
            parts = [parts[j] + parts[j + 1] for j in range(0, len(parts), 2)]
        return cnt + parts[0]

    n_ge = lax.fori_loop(0, nch, make_bias, jnp.zeros((8, tq), I32))
    tied = jnp.max(n_ge.sum(axis=0, keepdims=True)) > topk

    @pl.when(tied)
    def _():
        n_gt = jnp.where(theta == INT_MAX, 0, count_ge(jnp.where(theta == INT_MAX, theta, theta + 1)))
        quota = (topk - n_gt).astype(F32)
        before = jnp.where(row > lane, 1.0, 0.0).astype(BF)

        def make_bias(c, seen):
            off = off_of(c)
            kc = keys[pl.ds(off, tq), :]
            tie = jnp.where(kc == theta, 1.0, 0.0)
            rank = seen + _dot(before, tie.astype(BF))
            sel = jnp.where(kc > theta, 1.0, jnp.where(rank < quota, tie, 0.0))
            bias[pl.ds(off, tq), :] = jnp.where(causal_of(c), jnp.where(sel > 0.5, 0.0, MASKED), MASKED)
            return seen + jnp.sum(tie, axis=0, keepdims=True)

        lax.fori_loop(0, nch, make_bias, jnp.zeros((1, tq), F32))

    m_ref[...] = jnp.full_like(m_ref, NEG_INF)
    l_ref[...] = jnp.zeros_like(l_ref)
    acc_ref[...] = jnp.zeros_like(acc_ref)

    def attend_rows(c, n):
        off = off_of(c)
        m_new = []
        for h in range(N_Q):
            gsl = slice(h // (N_Q // N_KV) * HEAD_DIM, (h // (N_Q // N_KV) + 1) * HEAD_DIM)
            qh = qt[h * HEAD_DIM:(h + 1) * HEAD_DIM, :]
            m8 = jnp.full((8, tq), NEG_INF, F32)
            for sb in range(n * tq // SUB):
                rows = pl.ds(pl.multiple_of(off + sb * SUB, SUB), SUB)
                s = _dot(kb[rows, gsl], qh) + bias[rows, :]
                s_ref[h, sb * SUB:(sb + 1) * SUB, :] = s
                m8 = jnp.maximum(m8, s.reshape(SUB // 8, 8, tq).max(axis=0))
            m_new.append(jnp.maximum(m_ref[h], m8.max(axis=0, keepdims=True)))
        for h in range(N_Q):
            g = h // (N_Q // N_KV)
            alpha = jnp.exp2(m_ref[h] - m_new[h])
            pv = None
            for j in range(n):
                p = jnp.exp2(s_ref[h, j * tq:(j + 1) * tq, :] - m_new[h]).astype(BF)
                d = _dot(vt[c + j, g * VT_ROWS:(g + 1) * VT_ROWS, :], p)
                pv = d if pv is None else pv + d
            acc_ref[h] = acc_ref[h] * alpha + pv[0:HEAD_DIM]
            l_ref[h] = l_ref[h] * alpha + pv[HEAD_DIM:HEAD_DIM + 1]
            m_ref[h] = m_new[h]

    def attend2(j, carry):
        attend_rows(2 * j, 2)
        return carry

    lax.fori_loop(0, nch // 2, attend2, 0)

    @pl.when(nch % 2 == 1)
    def _():
        attend_rows(nch - 1, 1)

    for h in range(N_Q):
        o_ref[:, h * HEAD_DIM:(h + 1) * HEAD_DIM] = (acc_ref[h] * (1.0 / l_ref[h])).T.astype(BF)


def _dsa_prompt(qt, iqt, iwt, kb, vt, ikb, tq):
    b, _, t = qt.shape
    topk = min(TOPK_MAX, t // 4)
    kern = functools.partial(_dsa_prompt_kernel, tq=tq, topk=topk)
    return pl.pallas_call(
        kern,
        grid=(b, t // tq),
        in_specs=[
            pl.BlockSpec((None, ATTN_W, tq), lambda bi, i: (bi, 0, i)),
            pl.BlockSpec((None, ATTN_W, tq), lambda bi, i: (bi, 0, i)),
            pl.BlockSpec((None, IDX_H, tq), lambda bi, i: (bi, 0, i)),
            pl.BlockSpec((None, t, KV_W), lambda bi, i: (bi, 0, 0)),
            pl.BlockSpec((None, t // tq, N_KV * VT_ROWS, tq), lambda bi, i: (bi, 0, 0, 0)),
            pl.BlockSpec((None, t, IDX_D), lambda bi, i: (bi, 0, 0)),
        ],
        out_specs=pl.BlockSpec((None, tq, ATTN_W), lambda bi, i: (bi, i, 0)),
        out_shape=jax.ShapeDtypeStruct((b, t, ATTN_W), BF),
        scratch_shapes=[pltpu.VMEM((t, tq), I32), pltpu.VMEM((t, tq), F32),
                        pltpu.VMEM((N_Q, 1, tq), F32), pltpu.VMEM((N_Q, 1, tq), F32),
                        pltpu.VMEM((N_Q, HEAD_DIM, tq), F32), pltpu.VMEM((N_Q, 2 * tq, tq), F32)],
        compiler_params=_cp("parallel", "arbitrary"),
        name="dsa_prompt",
    )(qt, iqt, iwt, kb, vt, ikb)


def _l2norm(x):
    return x * lax.rsqrt(jnp.sum(x * x, -1, keepdims=True) + NORM_EPS)


def _gdn_activations(y, small, alog, dtb):
    y = y * _sigmoid(y)
    qs, ks, gs, bs = [], [], [], []
    r = y.shape[0]
    for h in range(GDN_H):
        sl = slice(h * GDN_D, (h + 1) * GDN_D)
        qs.append(_l2norm(y[:, sl]) * (GDN_D ** -0.5))
        ks.append(_l2norm(y[:, GDN_W + h * GDN_D:GDN_W + (h + 1) * GDN_D]))
        b_raw = jnp.broadcast_to(small[:, SM_GB + h:SM_GB + h + 1], (r, GDN_D))
        a_raw = jnp.broadcast_to(small[:, SM_GA + h:SM_GA + h + 1], (r, GDN_D))
        bs.append(_sigmoid(b_raw))
        gs.append(-jnp.exp(alog[:, sl]) * _softplus(a_raw + dtb[:, sl]))
    return qs, ks, y[:, 2 * GDN_W:], gs, bs


def _mix_prep_kernel(gq, gq_prev, sm, sb, sc, sx, sc_prev, sx_prev, gw, sw, alog, dtb,
                     qn_o, kn_o, vv_o, g_o, beta_o, scm_o, utail_o, xbuf, ubuf, *, tr):
    r = pl.program_id(1)
    first = r == 0
    xbuf[0:8, :] = jnp.where(first, 0.0, gq_prev[...])
    xbuf[8:, :] = gq[...]
    y = xbuf[5:5 + tr, :] * gw[0:1, :]
    for j in range(1, GDN_CONV):
        y = y + xbuf[5 + j:5 + j + tr, :] * gw[j:j + 1, :]
    qs, ks, v, gs, bs = _gdn_activations(y, sm[...], alog[...], dtb[...])
    for h in range(GDN_H):
        sl = slice(h * GDN_D, (h + 1) * GDN_D)
        qn_o[:, sl] = qs[h]
        kn_o[:, sl] = ks[h]
        g_o[:, sl] = gs[h]
        beta_o[:, sl] = bs[h]
    vv_o[...] = v

    u = sc[...] * sx[...]
    ubuf[0:8, :] = jnp.where(first, 0.0, sc_prev[...] * sx_prev[...])
    ubuf[8:, :] = u
    y = ubuf[6:6 + tr, :] * sw[0:1, :]
    for j in range(1, SC_CONV):
        y = y + ubuf[6 + j:6 + j + tr, :] * sw[j:j + 1, :]
    scm_o[...] = (sb[...] * y).astype(BF)

    @pl.when(r == pl.num_programs(1) - 1)
    def _():
        utail_o[...] = u[tr - 8:, :]


def _mix_prep(h3, gw, sw, alog, dtb, tr):
    b, t, _ = h3.shape
    nr = t // tr
    col = lambda w, off: pl.BlockSpec((None, tr, w), lambda bi, r, o=off // w: (bi, r, o))
    prev = lambda w, off: pl.BlockSpec(
        (None, 8, w), lambda bi, r, o=off // w: (bi, jnp.maximum(r * (tr // 8) - 1, 0), o))
    full = lambda a: pl.BlockSpec(a.shape, lambda bi, r: (0, 0))
    out = pl.BlockSpec((None, tr, GDN_W), lambda bi, r: (bi, r, 0))
    act = jax.ShapeDtypeStruct((b, t, GDN_W), F32)
    return pl.pallas_call(
        functools.partial(_mix_prep_kernel, tr=tr),
        grid=(b, nr),
        in_specs=[col(GDN_CONV_DIM, C_GQKV), prev(GDN_CONV_DIM, C_GQKV), col(128, C_SM),
                  col(SC_W, C_SB), col(SC_W, C_SC), col(SC_W, C_SX), prev(SC_W, C_SC), prev(SC_W, C_SX),
                  full(gw), full(sw), full(alog), full(dtb)],
        out_specs=[out, out, out, out, out, out,
                   pl.BlockSpec((None, 8, SC_W), lambda bi, r: (bi, 0, 0))],
        out_shape=[act, act, act, act, act,
                   jax.ShapeDtypeStruct((b, t, SC_W), BF),
                   jax.ShapeDtypeStruct((b, 8, SC_W), F32)],
        scratch_shapes=[pltpu.VMEM((tr + 8, GDN_CONV_DIM), F32), pltpu.VMEM((tr + 8, SC_W), F32)],
        compiler_params=_cp("parallel", "arbitrary"),
        name="mix_prep",
    )(h3, h3, h3, h3, h3, h3, h3, h3, gw, sw, alog, dtb)


def _cumsum_rows(x):
    row = lax.broadcasted_iota(I32, x.shape, 0)
    s = 1
    while s < x.shape[0]:
        x = x + jnp.where(row >= s, pltpu.roll(x, s, 0), 0.0)
        s *= 2
    return x


def _split_bf(x):
    hi = x.astype(BF)
    return hi, (x - hi.astype(F32)).astype(BF)


def _dot3(ah, al, bh, bl):
    m = ah.shape[0]
    r = _dot(jnp.concatenate([ah, al], axis=0), bh)
    return r[:m] + r[m:] + _dot(ah, bl)


def _gated_rmsnorm(o, z, norm_g):
    on = o * lax.rsqrt(jnp.mean(o * o, -1, keepdims=True) + NORM_EPS) * norm_g
    return on * (z * _sigmoid(z))


def _gdn_chunk_kernel(qn, kn, vv, gb, betab, z, norm_g, o_ref, s_out, s_ref, *, nb):
    c = pl.program_id(0)
    C = GDN_CHUNK
    W = GDN_H * C

    @pl.when(c == 0)
    def _():
        s_ref[...] = jnp.zeros_like(s_ref)

    ii = lax.broadcasted_iota(I32, (C, W), 0)
    jj = lax.broadcasted_iota(I32, (C, W), 1) % C
    lower = ii >= jj
    strict = ii > jj
    eye = jnp.where(ii == jj, 1.0, 0.0).astype(F32)
    blockmask = (lax.broadcasted_iota(I32, (W, W), 0) // C) == (lax.broadcasted_iota(I32, (W, W), 1) // C)
    zpad = jnp.zeros((GDN_D - C, GDN_D), F32)
    ng = norm_g[...]
    heads = [slice(h * GDN_D, (h + 1) * GDN_D) for h in range(GDN_H)]
    lanes = lambda parts: jnp.concatenate(parts, axis=1)

    blockmask = jnp.where(blockmask, 1.0, 0.0).astype(BF)

    def block_diag(m):
        return jnp.concatenate([m] * GDN_H, axis=0) * blockmask

    st = []
    for b, rows in [(b, slice(j * C, (j + 1) * C)) for b in range(nb) for j in range(GDN_CPS)]:
        q = [qn[b, rows, sl] for sl in heads]
        k = [kn[b, rows, sl] for sl in heads]
        beta = [betab[b, rows, sl] for sl in heads]
        gc = [_cumsum_rows(gb[b, rows, sl]) for sl in heads]
        kbeta = [k[h] * beta[h] for h in range(GDN_H)]
        gcol = lanes([g[:, 0:C] for g in gc])
        grow = lanes([jnp.concatenate([g, g], axis=0).T[0:C, 0:C] for g in gc])
        decay = jnp.where(lower, jnp.exp(jnp.where(lower, gcol - grow, 0.0)), 0.0)
        kk, qk = [], []
        for h in range(GDN_H):
            r = _dot_nt(jnp.concatenate([kbeta[h], q[h]], axis=0).astype(BF), k[h].astype(BF))
            kk.append(r[:C])
            qk.append(r[C:])
        a = jnp.where(strict, lanes(kk) * decay, 0.0)
        intra = lanes(qk) * decay
        rhs = jnp.concatenate(
            [lanes([vv[b, rows, heads[h]] * beta[h], kbeta[h] * jnp.exp(gc[h])]) for h in range(GDN_H)], axis=0)
        st.append(dict(b=b, rows=rows, q=q, k=k, gc=gc, intra=intra, rhs=rhs, nm=-a))

    for d in st:
        nh, nl = _split_bf(d["nm"])
        d["x"] = eye + d["nm"]
        d["p"] = _dot3(nh, nl, block_diag(nh), block_diag(nl))
    step = 4
    while step < C:
        for d in st:
            lh, ll = _split_bf(jnp.concatenate([d["x"], d["p"]], axis=0))
            r = _dot3(lh, ll, block_diag(lh[C:]), block_diag(ll[C:]))
            d["x"] = d["x"] + r[:C]
            d["p"] = r[C:]
        step *= 2
    for d in st:
        xh, xl = _split_bf(d["x"])
        p_hi, p_lo = _split_bf(d["p"])
        th, tl = _split_bf(d["x"] + _dot3(xh, xl, block_diag(p_hi), block_diag(p_lo)))
        rh, rl = _split_bf(d["rhs"])
        d["uw"] = _dot3(block_diag(th), block_diag(tl), rh, rl)

    for d in st:
        b, rows = d["b"], d["rows"]
        for h in range(GDN_H):
            u = d["uw"][h * C:(h + 1) * C, 0:GDN_D]
            w = d["uw"][h * C:(h + 1) * C, GDN_D:]
            gc = d["gc"][h]
            s = s_ref[b, h]
            r = _dot(jnp.concatenate([w, d["q"][h] * jnp.exp(gc)], axis=0).astype(BF), s.astype(BF))
            v_new = u - r[:C]
            o = r[C:] + _dot(d["intra"][:, h * C:(h + 1) * C].astype(BF), v_new.astype(BF))
            g_last = gc[C - 1:C, :]
            kd = d["k"][h] * jnp.exp(g_last - gc)
            kdt = jnp.concatenate([kd, zpad], axis=0).T.astype(BF)
            vnp = jnp.concatenate([v_new, zpad], axis=0).astype(BF)
            s_ref[b, h] = s * jnp.exp(g_last) + _dot(kdt, vnp)
            o_ref[b, rows, heads[h]] = _gated_rmsnorm(o, z[b, rows, heads[h]], ng).astype(BF)

    @pl.when(c == pl.num_programs(0) - 1)
    def _():
        s_out[...] = s_ref[...]


def _gdn_chunked(qn, kn, vv, gb, betab, h3, norm_g):
    b, t, _ = qn.shape
    C = GDN_CHUNK * GDN_CPS
    blk = pl.BlockSpec((b, C, GDN_W), lambda c: (0, c, 0))
    return pl.pallas_call(
        functools.partial(_gdn_chunk_kernel, nb=b),
        grid=(t // C,),
        in_specs=[blk, blk, blk, blk, blk,
                  pl.BlockSpec((b, C, GDN_W), lambda c: (0, c, C_GZ // GDN_W)),
                  pl.BlockSpec((1, GDN_D), lambda c: (0, 0))],
        out_specs=[pl.BlockSpec((b, C, GDN_W), lambda c: (0, c, 0)),
                   pl.BlockSpec((b, GDN_H, GDN_D, GDN_D), lambda c: (0, 0, 0, 0))],
        out_shape=[jax.ShapeDtypeStruct((b, t, GDN_W), BF),
                   jax.ShapeDtypeStruct((b, GDN_H, GDN_D, GDN_D), F32)],
        scratch_shapes=[pltpu.VMEM((b, GDN_H, GDN_D, GDN_D), F32)],
        compiler_params=_cp("arbitrary"),
        name="gdn_chunked",
    )(qn, kn, vv, gb, betab, h3, norm_g)


def _layer_norm(y, g, b):
    mu = jnp.mean(y, -1, keepdims=True)
    yc = y - mu
    var = jnp.mean(yc * yc, -1, keepdims=True)
    return yc * lax.rsqrt(var + LN_EPS) * g + b


def _out_proj_kernel(attn, gdn, scm, x, wo, g, b, o_ref):
    y = _dot(attn[...], wo[0:ATTN_W, :])
    y = y + _dot(gdn[...], wo[ATTN_W:ATTN_W + GDN_W, :])
    y = y + _dot(scm[...], wo[ATTN_W + GDN_W:, :])
    o_ref[...] = _layer_norm(DN_ALPHA * x[...] + y, g[...], b[...])


def _out_proj(attn, gdn, scm, x, wo, layer, g, b, tm):
    m = x.shape[0]
    row = lambda w: pl.BlockSpec((tm, w), lambda i: (i, 0))
    full = lambda a: pl.BlockSpec((None,) + a.shape[1:], lambda i: (layer, 0, 0))
    return pl.pallas_call(
        _out_proj_kernel,
        grid=(m // tm,),
        in_specs=[row(ATTN_W), row(GDN_W), row(SC_W), row(D_MODEL), full(wo), full(g), full(b)],
        out_specs=row(D_MODEL),
        out_shape=jax.ShapeDtypeStruct((m, D_MODEL), F32),
        compiler_params=_cp("parallel"),
        name="out_proj_ln",
    )(attn, gdn, scm, x, wo, g, b)


def _ffn_kernel(h, wg, wu, wd, g, b, o_ref, hb, acc):
    f = pl.program_id(1)

    @pl.when(f == 0)
    def _():
        hb[...] = h[...].astype(BF)
        acc[...] = jnp.zeros_like(acc)

    x = hb[...]
    gate = _dot(x, wg[...])
    hid = gate * _sigmoid(gate) * _dot(x, wu[...])
    acc[...] += _dot(hid.astype(BF), wd[...])

    @pl.when(f == pl.num_programs(1) - 1)
    def _():
        o_ref[...] = _layer_norm(DN_ALPHA * h[...] + acc[...], g[...], b[...])


def _ffn(h, wg, wu, wd, layer, g, b, tm, tf):
    m = h.shape[0]
    full = lambda a: pl.BlockSpec((None,) + a.shape[1:], lambda i, f: (layer, 0, 0))
    return pl.pallas_call(
        _ffn_kernel,
        grid=(m // tm, FFN_DIM // tf),
        in_specs=[pl.BlockSpec((tm, D_MODEL), lambda i, f: (i, 0)),
                  pl.BlockSpec((None, D_MODEL, tf), lambda i, f: (layer, 0, f)),
                  pl.BlockSpec((None, D_MODEL, tf), lambda i, f: (layer, 0, f)),
                  pl.BlockSpec((None, tf, D_MODEL), lambda i, f: (layer, f, 0)),
                  full(g), full(b)],
        out_specs=pl.BlockSpec((tm, D_MODEL), lambda i, f: (i, 0)),
        out_shape=jax.ShapeDtypeStruct((m, D_MODEL), F32),
        scratch_shapes=[pltpu.VMEM((tm, D_MODEL), BF), pltpu.VMEM((tm, D_MODEL), F32)],
        compiler_params=_cp("parallel", "arbitrary"),
        name="ffn_ln",
    )(h, wg, wu, wd, g, b)


def _head_rows(v):
    return jnp.repeat(v.astype(F32), GDN_D)[None, :]


def _prompt_layer(xp, tabs, lw, stacks, tm_in=1024, tn_in=768, tr=256, tm_out=512, tm_ffn=512, tf=512):
    b, t, _ = xp.shape
    x2 = xp.reshape(b * t, D_MODEL)
    layer = lw["layer"]
    h = _matmul(x2, lw["w_in"], layer, min(tm_in, b * t), tn_in)
    h3 = h.reshape(b, t, IN_PACKED)
    k_st, v_st, ik_st, kb, vt, qt, iqt, ikb, iwt = _attn_prep(h3, tabs, tr, layer, *stacks)
    attn = _dsa_prompt(qt, iqt, iwt, kb, vt, ikb, tr)
    qn, kn, vv, gb, betab, scm, utail = _mix_prep(h3, lw["gdn_conv_w"], lw["sc_conv_w"],
                                                 lw["alog"], lw["dtb"], tr)
    gdn, s_new = _gdn_chunked(qn, kn, vv, gb, betab, h3, lw["norm_g"])
    h1 = _out_proj(attn.reshape(b * t, ATTN_W), gdn.reshape(b * t, GDN_W), scm.reshape(b * t, SC_W),
                   x2, lw["w_out"], layer, lw["ln1_g"], lw["ln1_b"], min(tm_out, b * t))
    y = _ffn(h1, lw["w_gate"], lw["w_up"], lw["w_down"], layer, lw["ln2_g"], lw["ln2_b"],
             min(tm_ffn, b * t), tf)
    outs = dict(
        stacks=(k_st, v_st, ik_st),
        s=s_new,
        gconv=h3[:, t - (GDN_CONV - 1):, C_GQKV:C_GQKV + GDN_CONV_DIM],
        sconv=utail[:, 8 - (SC_CONV - 1):, :],
        attn=attn, gdn=gdn, scm=scm,
    )
    return y.reshape(b, t, D_MODEL), outs


def _sample_prep_kernel(h, c128, s128, c64, s64, gst, sst, s_in, gw, sw, alog, dtb, norm_g,
                        q_o, k_o, iq_o, ik_o, iw_o, gst_o, sst_o, s_o, gdn_o, scm_o, o_buf, *, nb):
    cos = c128[...]
    sin = s128[...]
    for hh in range(N_Q):
        sl = slice(hh * HEAD_DIM, (hh + 1) * HEAD_DIM)
        q_o[:, sl] = _rope128(h[:, C_AQ + hh * HEAD_DIM:C_AQ + (hh + 1) * HEAD_DIM], cos, sin) * (HEAD_DIM ** -0.5)
    for hh in range(N_KV):
        sl = slice(hh * HEAD_DIM, (hh + 1) * HEAD_DIM)
        k_o[:, sl] = _rope128(h[:, C_AK + hh * HEAD_DIM:C_AK + (hh + 1) * HEAD_DIM], cos, sin)
    cos = c64[...]
    sin = s64[...]
    for j in range(IDX_H * IDX_D // 128):
        sl = slice(j * 128, (j + 1) * 128)
        iq_o[:, sl] = _rope64x2(h[:, C_IQ + j * 128:C_IQ + (j + 1) * 128], cos, sin) * (IDX_D ** -0.5)
    small = h[:, C_SM:C_SM + 128]
    ik_o[...] = _rope64x2(small, cos, sin)[:, SM_IK:SM_IK + IDX_D]
    iw_o[...] = small[:, SM_IW:SM_IW + IDX_H] * (IDX_H ** -0.5)

    gq = h[:, C_GQKV:C_GQKV + GDN_CONV_DIM]
    y = gq * gw[GDN_CONV - 1:GDN_CONV, :]
    for j in range(GDN_CONV - 1):
        y = y + gst[j] * gw[j:j + 1, :]
    for j in range(GDN_CONV - 2):
        gst_o[j] = gst[j + 1]
    gst_o[GDN_CONV - 2] = gq
    qs, ks, v, gs, bs = _gdn_activations(y, small, alog[...], dtb[...])
    for b in range(nb):
        for hh in range(GDN_H):
            sl = slice(hh * GDN_D, (hh + 1) * GDN_D)
            kc = jnp.broadcast_to(ks[hh][b:b + 1, :], (GDN_D, GDN_D)).T
            qc = jnp.broadcast_to(qs[hh][b:b + 1, :], (GDN_D, GDN_D)).T
            s = s_in[b, hh] * jnp.exp(gs[hh][b:b + 1, :])
            ks_row = jnp.sum(kc * s, axis=0, keepdims=True)
            delta = (v[b:b + 1, sl] - ks_row) * bs[hh][b:b + 1, :]
            s = s + kc * delta
            s_o[b, hh] = s
            o_buf[b:b + 1, sl] = jnp.sum(qc * s, axis=0, keepdims=True)
    ng = norm_g[...]
    for hh in range(GDN_H):
        sl = slice(hh * GDN_D, (hh + 1) * GDN_D)
        z = h[:, C_GZ + hh * GDN_D:C_GZ + (hh + 1) * GDN_D]
        gdn_o[:, sl] = _gated_rmsnorm(o_buf[:, sl], z, ng).astype(BF)

    u = h[:, C_SC:C_SC + SC_W] * h[:, C_SX:C_SX + SC_W]
    y = u * sw[SC_CONV - 1:SC_CONV, :]
    for j in range(SC_CONV - 1):
        y = y + sst[j] * sw[j:j + 1, :]
    for j in range(SC_CONV - 2):
        sst_o[j] = sst[j + 1]
    sst_o[SC_CONV - 2] = u
    scm_o[...] = (h[:, C_SB:C_SB + SC_W] * y).astype(BF)


def _sample_prep(h, tabs, gst, sst, s_in, gw, sw, alog, dtb, norm_g):
    nb = h.shape[0]
    sds = jax.ShapeDtypeStruct
    return pl.pallas_call(
        functools.partial(_sample_prep_kernel, nb=nb),
        out_shape=[sds((nb, ATTN_W), F32), sds((nb, KV_W), F32), sds((nb, IDX_H * IDX_D), F32),
                   sds((nb, IDX_D), F32), sds((nb, IDX_H), F32),
                   sds(gst.shape, F32), sds(sst.shape, F32), sds(s_in.shape, F32),
                   sds((nb, GDN_W), BF), sds((nb, SC_W), BF)],
        scratch_shapes=[pltpu.VMEM((nb, GDN_W), F32)],
        compiler_params=pltpu.CompilerParams(vmem_limit_bytes=VMEM_LIMIT),
        name="sample_prep",
    )(h, *tabs, gst, sst, s_in, gw, sw, alog, dtb, norm_g)


def _sample_index_kernel(pt, *refs, pps, topk):
    pages = refs[:pps]
    iq, iw, ikn, expand, bias_o, seln_o, keys = refs[pps:]
    s = pl.program_id(1)
    iqb = iq[...]
    w = jnp.broadcast_to(iw[...], (IDX_H, PAGE))
    for j in range(pps):
        sc = jnp.maximum(_dot(iqb, pages[j][...].astype(BF)), 0.0)
        keys[pl.ds(s * pps + j, 1), :] = _sortable(jnp.sum(sc * w, axis=0, keepdims=True))

    @pl.when(s == pl.num_programs(1) - 1)
    def _():
        prod = iqb.astype(F32) * ikn[...].astype(BF).astype(F32)
        sn = jnp.sum(prod, axis=1, keepdims=True)
        sn = jnp.sum(jnp.maximum(sn, 0.0) * iw[...], axis=0, keepdims=True)
        key_new = jnp.broadcast_to(_sortable(sn), (1, PAGE))
        kall = keys[...]

        def count_ge(cand):
            c = jnp.where(kall >= cand, 1.0, 0.0).sum(axis=0, keepdims=True).sum(axis=1, keepdims=True)
            return jnp.broadcast_to(c, (1, PAGE)) + jnp.where(key_new >= cand, 1.0, 0.0)

        theta = _kth_largest(count_ge, (1, PAGE), float(topk))
        gt = jnp.where(kall > theta, 1.0, 0.0)
        tie = jnp.where(kall == theta, 1.0, 0.0)
        new_gt = jnp.where(key_new > theta, 1.0, 0.0)
        n_gt = jnp.broadcast_to(gt.sum(axis=0, keepdims=True).sum(axis=1, keepdims=True), (1, PAGE)) + new_gt
        quota = float(topk) - n_gt
        ii = lax.broadcasted_iota(I32, (PAGE, PAGE), 0)
        jj = lax.broadcasted_iota(I32, (PAGE, PAGE), 1)
        in_page = _dot(tie.astype(BF), jnp.where(ii < jj, 1.0, 0.0).astype(BF))
        per_page = jnp.broadcast_to(jnp.sum(tie, axis=1, keepdims=True), (kall.shape[0], PAGE))
        npg = kall.shape[0]
        pi = lax.broadcasted_iota(I32, (npg, npg), 0)
        pj = lax.broadcasted_iota(I32, (npg, npg), 1)
        earlier = _dot(jnp.where(pj < pi, 1.0, 0.0).astype(BF), per_page.astype(BF))
        taken = jnp.where(in_page + earlier < quota, tie, 0.0)
        n_tie = jnp.broadcast_to(tie.sum(axis=0, keepdims=True).sum(axis=1, keepdims=True), (1, PAGE))
        new_sel = jnp.where(key_new > theta, 1.0, jnp.where((key_new == theta) & (n_tie < quota), 1.0, 0.0))
        sel = _dot((gt + taken).astype(BF), expand[...])
        bias_o[...] = jnp.where(sel > 0.5, 0.0, MASKED)
        seln_o[...] = jnp.where(new_sel > 0.5, 0.0, MASKED)


def _expand_matrix():
    return jnp.asarray(np.arange(PAGE)[:, None] == (np.arange(PAGE * N_KV)[None, :] // N_KV), BF)


def _sample_index(page_table, kidx_t, layer, iq3, iw3, ikn3, pps=16):
    nb, npages = page_table.shape
    topk = min(TOPK_MAX, (npages * PAGE + 1) // 4)
    page = lambda j: pl.BlockSpec((None, None, IDX_D, PAGE),
                                  lambda b, s, pt, j=j: (layer, pt[b, s * pps + j], 0, 0))
    per_b = lambda shp: pl.BlockSpec((None,) + shp, lambda b, s, pt: (b, 0, 0))
    grid_spec = pltpu.PrefetchScalarGridSpec(
        num_scalar_prefetch=1,
        grid=(nb, npages // pps),
        in_specs=[page(j) for j in range(pps)]
                 + [per_b((IDX_H, IDX_D)), per_b((IDX_H, 1)), per_b((1, IDX_D)),
                    pl.BlockSpec((PAGE, PAGE * N_KV), lambda b, s, pt: (0, 0))],
        out_specs=[per_b((npages, PAGE * N_KV)), per_b((1, PAGE))],
        scratch_shapes=[pltpu.VMEM((npages, PAGE), I32)],
    )
    return pl.pallas_call(
        functools.partial(_sample_index_kernel, pps=pps, topk=topk),
        grid_spec=grid_spec,
        out_shape=[jax.ShapeDtypeStruct((nb, npages, PAGE * N_KV), F32), jax.ShapeDtypeStruct((nb, 1, PAGE), F32)],
        compiler_params=_cp("parallel", "arbitrary"),
        name="sample_index",
    )(page_table, *([kidx_t] * pps), iq3, iw3, ikn3, _expand_matrix())


def _sample_attn_kernel(pt, *refs, pps):
    kp = refs[:pps]
    vp = refs[pps:2 * pps]
    q8, bias, seln, kn8, vn8, o_ref, m_ref, l_ref, acc_ref = refs[2 * pps:]
    s = pl.program_id(1)
    rows = PAGE * N_KV

    @pl.when(s == 0)
    def _():
        m_ref[...] = jnp.full_like(m_ref, NEG_INF)
        l_ref[...] = jnp.zeros_like(l_ref)
        acc_ref[...] = jnp.zeros_like(acc_ref)

    q = q8[...]
    lane_head = lax.broadcasted_iota(I32, (N_Q, rows), 1) % N_KV
    own = lane_head == lax.broadcasted_iota(I32, (N_Q, rows), 0) // (N_Q // N_KV)
    head_mask = jnp.where(own, 0.0, MASKED)
    lg = [_dot_nt(q, kp[j][...].astype(BF)) + (bias[pl.ds(s * pps + j, 1), :] + head_mask) for j in range(pps)]
    m_old = m_ref[...]
    m_new = m_old
    for x in lg:
        m_new = jnp.maximum(m_new, jnp.max(x, axis=1, keepdims=True))
    alpha = jnp.exp(m_old - m_new)
    l_new = l_ref[...] * alpha
    acc = acc_ref[...] * alpha
    for j in range(pps):
        p = jnp.exp(lg[j] - m_new)
        l_new = l_new + jnp.sum(p, axis=1, keepdims=True)
        acc = acc + _dot(p.astype(BF), vp[j][...].astype(BF))
    m_ref[...] = m_new
    l_ref[...] = l_new
    acc_ref[...] = acc

    @pl.when(s == pl.num_programs(1) - 1)
    def _():
        knb = kn8[...].astype(BF).astype(F32)
        vnb = vn8[...].astype(BF).astype(F32)
        x = jnp.sum(q.astype(F32) * knb, axis=1, keepdims=True) + seln[:, 0:1]
        m_fin = jnp.maximum(m_new, x)
        a = jnp.exp(m_new - m_fin)
        p = jnp.exp(x - m_fin)
        l_fin = l_new * a + p
        o_ref[...] = ((acc * a + p.astype(BF).astype(F32) * vnb) / l_fin).astype(BF)


def _sample_attn(page_table, ck2, cv2, layer, q8, bias, seln, kn8, vn8, pps=8):
    nb, npages = page_table.shape
    rows = PAGE * N_KV
    page = lambda j: pl.BlockSpec((None, None, rows, HEAD_DIM),
                                  lambda b, s, pt, j=j: (layer, pt[b, s * pps + j], 0, 0))
    per_b = lambda shp: pl.BlockSpec((None,) + shp, lambda b, s, pt: (b, 0, 0))
    grid_spec = pltpu.PrefetchScalarGridSpec(
        num_scalar_prefetch=1,
        grid=(nb, npages // pps),
        in_specs=[page(j) for j in range(pps)] * 2
                 + [per_b((N_Q, HEAD_DIM)), per_b((npages, rows)), per_b((1, PAGE)),
                    per_b((N_Q, HEAD_DIM)), per_b((N_Q, HEAD_DIM))],
        out_specs=per_b((N_Q, HEAD_DIM)),
        scratch_shapes=[pltpu.VMEM((N_Q, 1), F32), pltpu.VMEM((N_Q, 1), F32), pltpu.VMEM((N_Q, HEAD_DIM), F32)],
    )
    return pl.pallas_call(
        functools.partial(_sample_attn_kernel, pps=pps),
        grid_spec=grid_spec,
        out_shape=jax.ShapeDtypeStruct((nb, N_Q, HEAD_DIM), BF),
        compiler_params=_cp("parallel", "arbitrary"),
        name="sample_attn",
    )(page_table, *([ck2] * pps), *([cv2] * pps), q8, bias, seln, kn8, vn8)


SAMPLE_ROWS = 16


def _sample_layer(xs, tabs, lw, layer, page_table, cache_k, cache_v, cache_kidx, gst, sst, s_in):
    nb = page_table.shape[0]
    h = _matmul(xs, lw["w_in"], layer, SAMPLE_ROWS, 768)[:nb]
    q, k, iq, ik, iw, gst_n, sst_n, s_n, gdn, scm = _sample_prep(
        h, tabs, gst, sst, s_in, lw["gdn_conv_w"], lw["sc_conv_w"], lw["alog"], lw["dtb"], lw["norm_g"])
    v = h[:, C_AV:C_AV + KV_W]
    bias, seln = _sample_index(page_table, cache_kidx, layer, iq.reshape(nb, IDX_H, IDX_D).astype(BF),
                               iw.reshape(nb, IDX_H, 1), ik.reshape(nb, 1, IDX_D))
    own_kv = lambda a: jnp.repeat(a.reshape(nb, N_KV, HEAD_DIM), N_Q // N_KV, axis=1)
    attn = _sample_attn(page_table, cache_k, cache_v, layer, q.reshape(nb, N_Q, HEAD_DIM).astype(BF),
                        bias, seln, own_kv(k), own_kv(v)).reshape(nb, ATTN_W)
    padr = lambda a: jnp.pad(a, ((0, SAMPLE_ROWS - nb), (0, 0)))
    h1 = _out_proj(padr(attn), padr(gdn), padr(scm), xs, lw["w_out"], layer, lw["ln1_g"], lw["ln1_b"],
                   SAMPLE_ROWS)
    y = _ffn(h1, lw["w_gate"], lw["w_up"], lw["w_down"], layer, lw["ln2_g"], lw["ln2_b"], SAMPLE_ROWS, 512)
    outs = dict(k=k.reshape(nb, 1, N_KV, HEAD_DIM), v=v.reshape(nb, 1, N_KV, HEAD_DIM), kidx=ik.reshape(nb, 1, IDX_D),
                s=s_n, gconv=jnp.swapaxes(gst_n, 0, 1), sconv=jnp.swapaxes(sst_n, 0, 1),
                attn=attn, gdn=gdn, scm=scm)
    return y, outs


def kernel(x_prompt, x_sample, cache_k, cache_v, cache_kidx, page_table, state_gdn, state_gdn_conv,
           state_sc_conv, w_in, gdn_conv_w, gdn_a_log, gdn_dt_bias, gdn_norm_g, sc_conv_w, w_out,
           ln1_g, ln1_b, w_gate, w_up, w_down, ln2_g, ln2_b):
    bp, tp, _ = x_prompt.shape
    nb, ts, _ = x_sample.shape
    assert ts == 1
    past = page_table.shape[1] * PAGE
    tabs_p = _rope_tables(jnp.arange(tp))
    tabs_s = _rope_tables(past + jnp.arange(ts))
    w_in_p = _pack_w_in(w_in)
    w_out_b, w_gate_b, w_up_b, w_down_b = (w.astype(BF) for w in (w_out, w_gate, w_up, w_down))
    depth, pool = cache_k.shape[:2]
    cache_k = cache_k.reshape(depth, pool, PAGE * N_KV, HEAD_DIM)
    cache_v = cache_v.reshape(depth, pool, PAGE * N_KV, HEAD_DIM)
    cache_kidx = jnp.swapaxes(cache_kidx, 2, 3)

    xp = x_prompt
    xs = jnp.pad(x_sample.reshape(nb, D_MODEL), ((0, SAMPLE_ROWS - nb), (0, 0)))
    stacks = (jnp.zeros((DEPTH, bp, tp * N_KV, HEAD_DIM), F32), jnp.zeros((DEPTH, bp, tp * N_KV, HEAD_DIM), F32),
              jnp.zeros((DEPTH, bp, IDX_D, tp), F32))
    po, so = [], []
    for l in range(DEPTH):
        lw = dict(layer=l, w_in=w_in_p, gdn_conv_w=gdn_conv_w[l], sc_conv_w=sc_conv_w[l],
                  alog=_head_rows(gdn_a_log[l]), dtb=_head_rows(gdn_dt_bias[l]), norm_g=gdn_norm_g[l][None, :],
                  w_out=w_out_b, ln1_g=ln1_g[:, None, :], ln1_b=ln1_b[:, None, :],
                  w_gate=w_gate_b, w_up=w_up_b, w_down=w_down_b,
                  ln2_g=ln2_g[:, None, :], ln2_b=ln2_b[:, None, :])
        xp, o = _prompt_layer(xp, tabs_p, lw, stacks)
        stacks = o["stacks"]
        po.append(o)
        xs, o = _sample_layer(xs, tabs_s, lw, l, page_table, cache_k, cache_v, cache_kidx,
                              jnp.swapaxes(state_gdn_conv[l], 0, 1), jnp.swapaxes(state_sc_conv[l], 0, 1),
                              state_gdn[l])
        so.append(o)

    st = lambda outs, n: jnp.stack([o[n] for o in outs])
    k_st, v_st, ik_st = stacks
    return (xp, xs[:nb].reshape(nb, ts, D_MODEL),
            k_st.reshape(DEPTH, bp, tp, N_KV, HEAD_DIM), v_st.reshape(DEPTH, bp, tp, N_KV, HEAD_DIM),
            jnp.swapaxes(ik_st, 2, 3), st(so, "k"), st(so, "v"), st(so, "kidx"),
            st(po, "s"), st(so, "s"), st(po, "gconv"), st(so, "gconv"), st(po, "sconv"), st(so, "sconv"))
```
